```python
import math
import functools
import jax
import jax.numpy as jnp
from jax import lax
import numpy as np

D_MODEL = 1024
BATCH = 4
SEQ = 4096
DEPTH = 4
DEC_BATCH = 32
DEC_SEQ = 4
PAST_LEN = 8192
PAGE_SIZE = 128

HEAD_DIM = 64
ROPE_THETA = 500000.0
RET_THETA = 10000.0
A_HEADS = 8
A_KV_HEADS = 2
IDX_HEADS = 4
IDX_DIM = 64
TOPK_MAX = 256
B_HEADS = 8
C_HEADS = 8
D_HEADS = 4
D_VDIM = 2 * HEAD_DIM
D_FF = 2816
CONV_W = 3
CHUNK = 128
Q_BLOCK = 128
EPS = 1e-6
N_AB = (DEPTH + 1) // 2
N_CD = DEPTH // 2
AB_SIZES = (A_HEADS * HEAD_DIM, A_KV_HEADS * HEAD_DIM, A_KV_HEADS * HEAD_DIM, IDX_HEADS * IDX_DIM, IDX_DIM, IDX_HEADS,
            B_HEADS * HEAD_DIM, B_HEADS * HEAD_DIM, B_HEADS * HEAD_DIM, B_HEADS * HEAD_DIM)
CD_SIZES = (C_HEADS * HEAD_DIM, C_HEADS * HEAD_DIM, C_HEADS * HEAD_DIM, C_HEADS, C_HEADS, C_HEADS * HEAD_DIM,
            2 * D_HEADS * HEAD_DIM, 2 * D_HEADS * HEAD_DIM, D_HEADS * D_VDIM)
AB_IN = sum(AB_SIZES)
CD_IN = sum(CD_SIZES)
AB_MIX = A_HEADS * HEAD_DIM + B_HEADS * HEAD_DIM
CD_MIX = C_HEADS * HEAD_DIM + D_HEADS * D_VDIM

kernel_name = 'hybrid_dsa_retention_mlstm_diffattn_decoder_step'


def _split(u, sizes):
    return jnp.split(u, [int(s) for s in np.cumsum(sizes)[:-1]], axis=-1)


def _rmsnorm(x, g):
    xf = x.astype(jnp.float32)
    y = xf * lax.rsqrt(jnp.mean(xf * xf, axis=-1, keepdims=True) + EPS) * g.astype(jnp.float32)
    return y.astype(x.dtype)


def _head_layernorm(o, gain):
    mu = jnp.mean(o, axis=-1, keepdims=True)
    var = jnp.mean(jnp.square(o - mu), axis=-1, keepdims=True)
    return (o - mu) * lax.rsqrt(var + EPS) * gain.astype(jnp.float32).reshape(o.shape[-2:])


def _head_rmsnorm(o, gain):
    return o * lax.rsqrt(jnp.mean(o * o, axis=-1, keepdims=True) + EPS) * gain.astype(jnp.float32)


def _modulation(c, w, b):
    m = jax.nn.silu(c.astype(jnp.float32)) @ w.astype(jnp.float32) + b.astype(jnp.float32)
    return [t[:, None, :] for t in jnp.split(m, 6, axis=-1)]


def _ada_norm(x, g, shift, scale):
    return (_rmsnorm(x, g).astype(jnp.float32) * (1.0 + scale) + shift).astype(x.dtype)


def _gated_residual(x, y, g, gate):
    return x + (gate * _rmsnorm(y, g).astype(jnp.float32)).astype(x.dtype)


def _rope(x, pos, rot_dims, theta):
    half = rot_dims // 2
    inv = jnp.power(theta, -jnp.arange(half, dtype=jnp.float32) / half)
    ang = pos.astype(jnp.float32)[:, None] * inv
    cos = jnp.cos(ang)[:, None, :]
    sin = jnp.sin(ang)[:, None, :]
    xf = x.astype(jnp.float32)
    x1 = xf[..., :half]
    x2 = xf[..., half:rot_dims]
    out = jnp.concatenate([x1 * cos - x2 * sin, x1 * sin + x2 * cos, xf[..., rot_dims:]], axis=-1)
    return out.astype(x.dtype)


def _to_chunks(a, chunk):
    B, L = a.shape[:2]
    a = a.reshape(B, L // chunk, chunk, *a.shape[2:])
    return jnp.moveaxis(a, (1, 3), (0, 2))


def _from_chunks(a):
    a = jnp.moveaxis(a, (0, 2), (1, 3))
    return a.reshape(a.shape[0], a.shape[1] * a.shape[2], *a.shape[3:])


def _indexer_topk(qi, wi, ki, q_pos, k_pos, topk):
    f32 = jnp.float32
    logits = jnp.einsum('bthd,bld->bthl', qi.astype(f32), ki.astype(f32)) * IDX_DIM ** -0.5
    score = jnp.einsum('bthl,bth->btl', jax.nn.relu(logits), wi.astype(f32) * IDX_HEADS ** -0.5)
    score = jnp.where(k_pos[None, None, :] <= q_pos[None, :, None], score, -jnp.inf)
    return lax.top_k(score, topk)[1]


def _sparse_attend(q, kg, vg, valid):
    B, T = q.shape[:2]
    qg = q.reshape(B, T, A_KV_HEADS, A_HEADS // A_KV_HEADS, HEAD_DIM)
    s = jnp.einsum('btkgd,btnkd->btkgn', qg, kg).astype(jnp.float32) * HEAD_DIM ** -0.5
    s = jnp.where(valid[:, :, None, None, :], s, -jnp.inf)
    p = jax.nn.softmax(s, axis=-1).astype(vg.dtype)
    o = jnp.einsum('btkgn,btnkd->btkgd', p, vg)
    return o.reshape(B, T, A_HEADS * HEAD_DIM)


def _dsa_prompt(q, kv, qi, ki, wi):
    B, S = q.shape[:2]
    topk = min(TOPK_MAX, S // 4)
    k_pos = jnp.arange(S)

    def block(i):
        s0 = i * Q_BLOCK
        sl = lambda a: lax.dynamic_slice_in_dim(a, s0, Q_BLOCK, axis=1)
        q_pos = s0 + jnp.arange(Q_BLOCK)
        idx = _indexer_topk(sl(qi), sl(wi), ki, q_pos, k_pos, topk)
        g = jax.vmap(lambda rows, ix: rows[ix])(kv, idx)
        return _sparse_attend(sl(q), g[:, :, :, 0], g[:, :, :, 1], idx <= q_pos[None, :, None])

    o = lax.map(block, jnp.arange(S // Q_BLOCK))
    return jnp.moveaxis(o, 0, 1).reshape(B, S, A_HEADS * HEAD_DIM)


def _dsa_sample(q, kv_new, qi, ki_new, wi, cache_kv, cache_idx, layer, page_table):
    DB, T = q.shape[:2]
    past = page_table.shape[1] * PAGE_SIZE
    L = past + T
    topk = min(TOPK_MAX, L // 4)
    ki_past = cache_idx[layer, page_table].reshape(DB, past, IDX_DIM)
    ki_all = jnp.concatenate([ki_past, ki_new.astype(ki_past.dtype)], axis=1)
    q_pos = past + jnp.arange(T)
    idx = _indexer_topk(qi, wi, ki_all, q_pos, jnp.arange(L), topk)
    in_past = idx < past
    pidx = jnp.minimum(idx, past - 1)
    phys = jnp.take_along_axis(page_table, (pidx // PAGE_SIZE).reshape(DB, -1), axis=1).reshape(pidx.shape)
    past_rows = cache_kv[layer, phys, pidx % PAGE_SIZE]
    new_rows = jax.vmap(lambda rows, ix: rows[ix])(kv_new, jnp.clip(idx - past, 0, T - 1))
    g = jnp.where(in_past[..., None, None, None], past_rows, new_rows.astype(past_rows.dtype))
    return _sparse_attend(q, g[:, :, :, 0], g[:, :, :, 1], idx <= q_pos[None, :, None])


def _retention(q, k, v, S0, chunk):
    f32 = jnp.float32
    H = q.shape[2]
    log_g = jnp.log1p(-jnp.exp2(-5.0 - jnp.arange(H, dtype=f32)))
    pos = jnp.arange(chunk, dtype=f32)
    diff = pos[:, None] - pos[None, :]
    dmat = jnp.where(diff >= 0, jnp.exp(log_g[:, None, None] * jnp.maximum(diff, 0.0)), 0.0)
    q_dec = jnp.exp(log_g[:, None] * (pos + 1.0))[:, :, None]
    k_dec = jnp.exp(log_g[:, None] * (chunk - 1.0 - pos))[:, :, None]
    c_dec = jnp.exp(log_g * chunk)[:, None, None]

    def step(S, xs):
        qb, kb, vb = xs
        att = jnp.einsum('bhid,bhjd->bhij', qb, kb) * dmat
        o = jnp.einsum('bhij,bhjv->bhiv', att, vb) + jnp.einsum('bhid,bhdv->bhiv', qb, S) * q_dec
        S = S * c_dec + jnp.einsum('bhjd,bhjv->bhdv', kb * k_dec, vb)
        return S, o

    xs = tuple(_to_chunks(a.astype(f32), chunk) for a in (q, k, v))
    S, o = lax.scan(step, S0.astype(f32), xs)
    return _from_chunks(o), S


def _ab_mixer(h, pos, w_in, w_out, ret_gain, S0, chunk, attn):
    B, T, _ = h.shape
    aq, ak, av, iq, ik, iw, bq, bk, bv, bg = _split(h @ w_in, AB_SIZES)
    prope = lambda a, n, dh: _rope(a.reshape(B, T, n, dh), pos, dh // 4, ROPE_THETA)
    q_a = prope(aq, A_HEADS, HEAD_DIM)
    kv_a = jnp.stack([prope(ak, A_KV_HEADS, HEAD_DIM), av.reshape(B, T, A_KV_HEADS, HEAD_DIM)], axis=2)
    q_i = prope(iq, IDX_HEADS, IDX_DIM)
    k_i = prope(ik, 1, IDX_DIM)[:, :, 0]
    a_out = attn(q_a, kv_a, q_i, k_i, iw)
    q_b = _rope(bq.reshape(B, T, B_HEADS, HEAD_DIM), pos, HEAD_DIM, RET_THETA)
    k_b = _rope(bk.reshape(B, T, B_HEADS, HEAD_DIM), pos, HEAD_DIM, RET_THETA) * HEAD_DIM ** -0.5
    o_b, S = _retention(q_b, k_b, bv.reshape(B, T, B_HEADS, HEAD_DIM), S0, chunk)
    b_out = jax.nn.silu(bg.astype(jnp.float32)).reshape(B, T, B_HEADS, HEAD_DIM) * _head_layernorm(o_b, ret_gain)
    mix = jnp.concatenate([a_out, b_out.reshape(B, T, -1).astype(a_out.dtype)], axis=-1)
    return mix @ w_out, kv_a, k_i, S


def _mlstm(q, k, v, i_pre, f_pre, C0, n0, m0, chunk):
    f32 = jnp.float32
    causal = jnp.tril(jnp.ones((chunk, chunk), bool))

    def step(carry, xs):
        C, n, m = carry
        qb, kb, vb, ib, lfb = xs
        F = jnp.cumsum(lfb, axis=-1)
        a = ib - F
        m_t = F + jnp.maximum(m[..., None], lax.cummax(a, axis=2))
        dmat = jnp.exp(jnp.where(causal, F[..., :, None] + a[..., None, :] - m_t[..., :, None], -jnp.inf))
        inter = jnp.exp(F + m[..., None] - m_t)
        s = jnp.einsum('bhtd,bhjd->bhtj', qb, kb) * dmat
        num = jnp.einsum('bhtj,bhjv->bhtv', s, vb) + inter[..., None] * jnp.einsum('bhtd,bhdv->bhtv', qb, C)
        den = jnp.sum(s, axis=-1) + inter * jnp.einsum('bhtd,bhd->bht', qb, n)
        hh = num / jnp.maximum(jnp.abs(den), jnp.exp(-m_t))[..., None]
        m_end = m_t[..., -1]
        w = jnp.exp(F[..., -1:] + a - m_end[..., None])[..., None]
        dec = jnp.exp(F[..., -1] + m - m_end)
        C = dec[..., None, None] * C + jnp.einsum('bhjd,bhjv->bhdv', kb * w, vb)
        n = dec[..., None] * n + jnp.sum(kb * w, axis=2)
        return (C, n, m_end), hh

    lf = jax.nn.log_sigmoid(f_pre.astype(f32))
    xs = tuple(_to_chunks(a.astype(f32), chunk) for a in (q, k, v, i_pre, lf))
    (C, n, m), hh = lax.scan(step, (C0.astype(f32), n0.astype(f32), m0.astype(f32)), xs)
    return _from_chunks(hh), (C, n, m)


def _diff_core(q, k, v, q_pos, k_pos, lam):
    s = jnp.einsum('btmhd,blmhd->bmhtl', q, k).astype(jnp.float32) * HEAD_DIM ** -0.5
    s = jnp.where(k_pos[None, :] <= q_pos[:, None], s, -jnp.inf)
    pr = jax.nn.softmax(s, axis=-1)
    a = pr[:, 0] - lam * pr[:, 1]
    return jnp.einsum('bhtl,blhv->bthv', a.astype(v.dtype), v)


def _diff_prompt(q, k, v, lam):
    B, S = q.shape[:2]
    k_pos = jnp.arange(S)

    def block(i):
        s0 = i * Q_BLOCK
        qb = lax.dynamic_slice_in_dim(q, s0, Q_BLOCK, axis=1)
        return _diff_core(qb, k, v, s0 + jnp.arange(Q_BLOCK), k_pos, lam)

    o = lax.map(block, jnp.arange(S // Q_BLOCK))
    return jnp.moveaxis(o, 0, 1).reshape(B, S, D_HEADS, D_VDIM)


def _diff_sample(q, k, v, lam, cache_k, cache_v, layer, page_table):
    DB, T = q.shape[:2]
    past = page_table.shape[1] * PAGE_SIZE
    k_past = cache_k[layer, page_table].reshape(DB, past, 2, D_HEADS, HEAD_DIM)
    v_past = cache_v[layer, page_table].reshape(DB, past, D_HEADS, D_VDIM)
    k_all = jnp.concatenate([k_past, k.astype(k_past.dtype)], axis=1)
    v_all = jnp.concatenate([v_past, v.astype(v_past.dtype)], axis=1)
    return _diff_core(q.astype(k_past.dtype), k_all, v_all, past + jnp.arange(T), jnp.arange(past + T), lam)


def _cd_mixer(h, pos, w_in, w_out, gate_bias, c_gain, lam_par, subln, lam_init, C0, n0, m0, chunk, attn):
    B, T, _ = h.shape
    f32 = jnp.float32
    cq, ck, cv, ci, cf, co, dq, dk, dv = _split(h @ w_in, CD_SIZES)
    hd = lambda a: a.reshape(B, T, C_HEADS, HEAD_DIM)
    i_pre = ci.astype(f32) + gate_bias[:C_HEADS].astype(f32)
    f_pre = cf.astype(f32) + gate_bias[C_HEADS:].astype(f32)
    hc, mstate = _mlstm(hd(cq), hd(ck) * HEAD_DIM ** -0.5, hd(cv), i_pre, f_pre, C0, n0, m0, chunk)
    hc = jax.nn.sigmoid(hd(co).astype(f32)) * hc
    c_out = _head_layernorm(hc, c_gain).reshape(B, T, -1)
    qd = _rope(dq.reshape(B, T, 2 * D_HEADS, HEAD_DIM), pos, HEAD_DIM // 4, ROPE_THETA).reshape(B, T, 2, D_HEADS, HEAD_DIM)
    kd = _rope(dk.reshape(B, T, 2 * D_HEADS, HEAD_DIM), pos, HEAD_DIM // 4, ROPE_THETA).reshape(B, T, 2, D_HEADS, HEAD_DIM)
    vd = dv.reshape(B, T, D_HEADS, D_VDIM)
    lp = lam_par.astype(f32)
    lam = jnp.exp(jnp.sum(lp[0] * lp[1])) - jnp.exp(jnp.sum(lp[2] * lp[3])) + lam_init
    od = attn(qd, kd, vd, lam)
    d_out = _head_rmsnorm(od.astype(f32), subln) * (1.0 - lam_init)
    mix = jnp.concatenate([c_out, d_out.reshape(B, T, -1)], axis=-1).astype(h.dtype)
    return mix @ w_out, kd, vd, mstate


def _conv_ffn(h, conv_state, w_up, conv_w, conv_b, w_down):
    T = h.shape[1]
    a, b = jnp.split(h @ w_up, 2, axis=-1)
    a_ext = jnp.concatenate([conv_state.astype(a.dtype), a], axis=1)
    conv = conv_b + a_ext[:, 0:T] * conv_w[0]
    for j in range(1, CONV_W):
        conv = conv + a_ext[:, j:j + T] * conv_w[j]
    y = jax.nn.gelu(conv, approximate=True) * b
    return y @ w_down, a_ext[:, T:]


def setup_inputs(seed: int = 0) -> dict:
    key = jax.random.key(seed)
    ks = iter(jax.random.split(key, 64))
    f32 = jnp.float32
    D = D_MODEL
    n_pages = PAST_LEN // PAGE_SIZE
    n_used = DEC_BATCH * n_pages
    n_phys = n_used + max(1, n_used // 4)

    def nrm(shape, scale=1.0):
        return jax.random.normal(next(ks), shape, f32) * scale

    def gain(shape):
        return 1.0 + 0.02 * jax.random.normal(next(ks), shape, f32)

    x_prompt = nrm((BATCH, SEQ, D))
    x_sample = nrm((DEC_BATCH, DEC_SEQ, D))
    c_prompt = nrm((BATCH, D))
    c_sample = nrm((DEC_BATCH, D))
    page_table = jax.random.permutation(next(ks), n_phys)[:n_used].reshape(DEC_BATCH, n_pages).astype(jnp.int32)
    cache_a_kv = nrm((N_AB, n_phys, PAGE_SIZE, 2, A_KV_HEADS, HEAD_DIM))
    cache_a_idx = nrm((N_AB, n_phys, PAGE_SIZE, IDX_DIM))
    cache_d_k = nrm((N_CD, n_phys, PAGE_SIZE, 2, D_HEADS, HEAD_DIM))
    cache_d_v = nrm((N_CD, n_phys, PAGE_SIZE, D_HEADS, D_VDIM))
    state_ret = nrm((N_AB, DEC_BATCH, B_HEADS, HEAD_DIM, HEAD_DIM), 0.5)
    state_mlstm_c = nrm((N_CD, DEC_BATCH, C_HEADS, HEAD_DIM, HEAD_DIM), 0.5)
    state_mlstm_n = nrm((N_CD, DEC_BATCH, C_HEADS, HEAD_DIM), 0.5)
    state_mlstm_m = nrm((N_CD, DEC_BATCH, C_HEADS))
    state_ffn_conv = nrm((DEPTH, DEC_BATCH, CONV_W - 1, D_FF))
    forget_bias = jnp.linspace(3.0, 6.0, C_HEADS, dtype=f32)
    c_gate_bias = jnp.concatenate([nrm((N_CD, C_HEADS), 0.1), forget_bias + nrm((N_CD, C_HEADS), 0.1)], axis=-1)
    return {
        'x_prompt': x_prompt, 'x_sample': x_sample, 'c_prompt': c_prompt, 'c_sample': c_sample,
        'page_table': page_table,
        'cache_a_kv': cache_a_kv, 'cache_a_idx': cache_a_idx, 'cache_d_k': cache_d_k, 'cache_d_v': cache_d_v,
        'state_ret': state_ret, 'state_mlstm_c': state_mlstm_c, 'state_mlstm_n': state_mlstm_n,
        'state_mlstm_m': state_mlstm_m, 'state_ffn_conv': state_ffn_conv,
        'norm_pre_mix': gain((DEPTH, D)), 'norm_post_mix': gain((DEPTH, D)),
        'norm_pre_ffn': gain((DEPTH, D)), 'norm_post_ffn': gain((DEPTH, D)),
        'w_ada': nrm((DEPTH, D, 6 * D), 0.5 * D ** -0.5), 'b_ada': nrm((DEPTH, 6 * D), 0.02),
        'w_ab_in': nrm((N_AB, D, AB_IN), D ** -0.5), 'w_ab_out': nrm((N_AB, AB_MIX, D), AB_MIX ** -0.5),
        'ret_gain': gain((N_AB, B_HEADS * HEAD_DIM)),
        'w_cd_in': nrm((N_CD, D, CD_IN), D ** -0.5), 'w_cd_out': nrm((N_CD, CD_MIX, D), CD_MIX ** -0.5),
        'c_gate_bias': c_gate_bias, 'c_gain': gain((N_CD, C_HEADS * HEAD_DIM)),
        'd_lambda': nrm((N_CD, 4, HEAD_DIM), 0.1), 'd_subln': gain((N_CD, D_VDIM)),
        'ffn_w_up': nrm((DEPTH, D, 2 * D_FF), D ** -0.5), 'ffn_conv_w': nrm((DEPTH, CONV_W, D_FF), CONV_W ** -0.5),
        'ffn_conv_b': nrm((DEPTH, D_FF), 0.02), 'ffn_w_down': nrm((DEPTH, D_FF, D), D_FF ** -0.5),
    }


def reference(x_prompt, x_sample, c_prompt, c_sample, page_table,
              cache_a_kv, cache_a_idx, cache_d_k, cache_d_v,
              state_ret, state_mlstm_c, state_mlstm_n, state_mlstm_m, state_ffn_conv,
              norm_pre_mix, norm_post_mix, norm_pre_ffn, norm_post_ffn, w_ada, b_ada,
              w_ab_in, w_ab_out, ret_gain, w_cd_in, w_cd_out, c_gate_bias, c_gain, d_lambda, d_subln,
              ffn_w_up, ffn_conv_w, ffn_conv_b, ffn_w_down):
    f32 = jnp.float32
    B, S, _ = x_prompt.shape
    DB, T, _ = x_sample.shape
    past = page_table.shape[1] * PAGE_SIZE
    pos_p = jnp.arange(S)
    pos_s = past + jnp.arange(T)
    xp, xs = x_prompt, x_sample
    pa_kv, pa_idx, p_ret, pd_k, pd_v, p_mc, p_mn, p_mm, p_cv = [], [], [], [], [], [], [], [], []
    sa_kv, sa_idx, s_rt, sd_k, sd_v, s_mc, s_mn, s_mm, s_cv = [], [], [], [], [], [], [], [], []
    for l in range(DEPTH):
        p = l // 2
        mp = _modulation(c_prompt, w_ada[l], b_ada[l])
        ms = _modulation(c_sample, w_ada[l], b_ada[l])
        hp = _ada_norm(xp, norm_pre_mix[l], mp[0], mp[1])
        hs = _ada_norm(xs, norm_pre_mix[l], ms[0], ms[1])
        if l % 2 == 0:
            yp, kv, ki, st = _ab_mixer(hp, pos_p, w_ab_in[p], w_ab_out[p], ret_gain[p],
                                       jnp.zeros((B, B_HEADS, HEAD_DIM, HEAD_DIM), f32), CHUNK, _dsa_prompt)
            pa_kv.append(kv); pa_idx.append(ki); p_ret.append(st)
            attn_s = functools.partial(_dsa_sample, cache_kv=cache_a_kv, cache_idx=cache_a_idx, layer=p, page_table=page_table)
            ys, kv, ki, st = _ab_mixer(hs, pos_s, w_ab_in[p], w_ab_out[p], ret_gain[p], state_ret[p], T, attn_s)
            sa_kv.append(kv); sa_idx.append(ki); s_rt.append(st)
        else:
            lam_init = 0.8 - 0.6 * math.exp(-0.3 * l)
            yp, kd, vd, (mc, mn, mm) = _cd_mixer(
                hp, pos_p, w_cd_in[p], w_cd_out[p], c_gate_bias[p], c_gain[p], d_lambda[p], d_subln[p], lam_init,
                jnp.zeros((B, C_HEADS, HEAD_DIM, HEAD_DIM), f32), jnp.zeros((B, C_HEADS, HEAD_DIM), f32),
                jnp.zeros((B, C_HEADS), f32), CHUNK, _diff_prompt)
            pd_k.append(kd); pd_v.append(vd); p_mc.append(mc); p_mn.append(mn); p_mm.append(mm)
            attn_s = functools.partial(_diff_sample, cache_k=cache_d_k, cache_v=cache_d_v, layer=p, page_table=page_table)
            ys, kd, vd, (mc, mn, mm) = _cd_mixer(
                hs, pos_s, w_cd_in[p], w_cd_out[p], c_gate_bias[p], c_gain[p], d_lambda[p], d_subln[p], lam_init,
                state_mlstm_c[p], state_mlstm_n[p], state_mlstm_m[p], T, attn_s)
            sd_k.append(kd); sd_v.append(vd); s_mc.append(mc); s_mn.append(mn); s_mm.append(mm)
        xp = _gated_residual(xp, yp, norm_post_mix[l], mp[2])
        xs = _gated_residual(xs, ys, norm_post_mix[l], ms[2])
        hp = _ada_norm(xp, norm_pre_ffn[l], mp[3], mp[4])
        hs = _ada_norm(xs, norm_pre_ffn[l], ms[3], ms[4])
        fp, cvp = _conv_ffn(hp, jnp.zeros((B, CONV_W - 1, D_FF), hp.dtype), ffn_w_up[l], ffn_conv_w[l], ffn_conv_b[l], ffn_w_down[l])
        fs, cvs = _conv_ffn(hs, state_ffn_conv[l], ffn_w_up[l], ffn_conv_w[l], ffn_conv_b[l], ffn_w_down[l])
        p_cv.append(cvp); s_cv.append(cvs)
        xp = _gated_residual(xp, fp, norm_post_ffn[l], mp[5])
        xs = _gated_residual(xs, fs, norm_post_ffn[l], ms[5])
    dt = x_prompt.dtype
    p_a_kv = jnp.stack(pa_kv).astype(dt)
    p_a_idx = jnp.stack(pa_idx).astype(dt)
    p_d_k = jnp.stack(pd_k).astype(dt)
    p_d_v = jnp.stack(pd_v).astype(dt)
    p_rt = jnp.stack(p_ret).astype(dt)
    p_c = jnp.stack(p_mc).astype(dt)
    p_n = jnp.stack(p_mn).astype(dt)
    p_m = jnp.stack(p_mm).astype(dt)
    p_conv = jnp.stack(p_cv).astype(dt)
    s_a_kv = jnp.stack(sa_kv).astype(dt)
    s_a_idx = jnp.stack(sa_idx).astype(dt)
    s_d_k = jnp.stack(sd_k).astype(dt)
    s_d_v = jnp.stack(sd_v).astype(dt)
    s_ret = jnp.stack(s_rt).astype(dt)
    s_c = jnp.stack(s_mc).astype(dt)
    s_n = jnp.stack(s_mn).astype(dt)
    s_m = jnp.stack(s_mm).astype(dt)
    s_conv = jnp.stack(s_cv).astype(dt)
    return (xp, xs, p_a_kv, p_a_idx, p_d_k, p_d_v, p_rt, p_c, p_n, p_m, p_conv,
            s_a_kv, s_a_idx, s_d_k, s_d_v, s_ret, s_c, s_n, s_m, s_conv)
```

```python
import functools
import math

import numpy as np
import jax
import jax.numpy as jnp
from jax import lax
from jax.experimental import pallas as pl
from jax.experimental.pallas import tpu as pltpu

F32 = jnp.float32
BF16 = jnp.bfloat16

PAGE_SIZE = 128
HEAD_DIM = 64
ROPE_THETA = 500000.0
RET_THETA = 10000.0
A_HEADS = 8
A_KV_HEADS = 2
IDX_HEADS = 4
IDX_DIM = 64
TOPK_MAX = 256
B_HEADS = 8
C_HEADS = 8
D_HEADS = 4
D_VDIM = 2 * HEAD_DIM
CONV_W = 3
CHUNK = 128
EPS = 1e-6
NEG = -1e30
SLOT = 8
SLOT_LO = CONV_W - 1
LANES = 128
VMEM_LIMIT = 56 * 1024 * 1024

_NT = (((1,), (1,)), ((), ()))


def _cp(*sem):
    return pltpu.CompilerParams(dimension_semantics=sem, vmem_limit_bytes=VMEM_LIMIT)


def _dot(a, b):
    return jnp.dot(a, b, preferred_element_type=F32)


def _dot_nt(a, b):
    return lax.dot_general(a, b, _NT, preferred_element_type=F32)


def _iota(shape, dim):
    return lax.broadcasted_iota(jnp.int32, shape, dim)


def _eye_bf16(n):
    return jnp.where(_iota((n, n), 0) == _iota((n, n), 1), 1.0, 0.0).astype(BF16)


def _dot_tn(a_bf16, b_bf16):
    at = _dot_nt(_eye_bf16(a_bf16.shape[1]), a_bf16).astype(BF16)
    return _dot(at, b_bf16)


def _split3(x):
    hi = x.astype(BF16)
    r1 = x - hi.astype(F32)
    mid = r1.astype(BF16)
    lo = (r1 - mid.astype(F32)).astype(BF16)
    return hi, mid, lo


def _rope_tables(pos, rot, theta):
    half = rot // 2
    inv = theta ** (-np.arange(half, dtype=np.float64) / half)
    ang = np.asarray(pos, np.float64)[:, None] * inv[None]
    cos, sin = np.cos(ang), np.sin(ang)
    n = len(pos)
    t0 = np.ones((n, HEAD_DIM)); t1 = np.zeros((n, HEAD_DIM)); t2 = np.zeros((n, HEAD_DIM))
    t0[:, :half] = cos; t0[:, half:rot] = cos
    t1[:, :half] = -sin
    t2[:, half:rot] = sin
    tab = np.stack([np.tile(t, (1, LANES // HEAD_DIM)) for t in (t0, t1, t2)])
    return jnp.asarray(tab, F32)


def _retention_tables(chunk, lo, hi):
    n = hi - lo
    log_g = np.log1p(-np.exp2(-5.0 - np.arange(B_HEADS, dtype=np.float64)))
    r = np.arange(chunk)
    ok = (r >= lo) & (r < hi)
    pos = (r - lo).astype(np.float64)
    diff = pos[:, None] - pos[None, :]
    dm = np.where((diff >= 0) & ok[:, None] & ok[None, :], np.exp(log_g[:, None, None] * np.maximum(diff, 0.0)), 0.0)
    qd = np.exp(log_g[:, None] * (pos + 1.0))[:, :, None] * np.ones((1, 1, HEAD_DIM))
    kd = np.where(ok, np.exp(log_g[:, None] * (n - 1.0 - pos)), 0.0)[:, :, None] * np.ones((1, 1, HEAD_DIM))
    cd = np.exp(log_g * n)[:, None, None] * np.ones((1, 1, HEAD_DIM))
    return tuple(jnp.asarray(a, F32) for a in (dm, qd, kd, cd))


def _mod_body(c_ref, w_ref, b_ref, o_ref):
    c = c_ref[...]
    s = (c * jax.nn.sigmoid(c)).astype(BF16)
    o_ref[0] = _dot(s, w_ref[0].astype(BF16)) + b_ref[0]


def _modulation(c_all, w_ada, b_ada):
    depth, d, n = w_ada.shape
    r = c_all.shape[0]
    tn = 1536
    return pl.pallas_call(
        _mod_body, grid=(depth, n // tn),
        in_specs=[pl.BlockSpec((r, d), lambda l, j: (0, 0)),
                  pl.BlockSpec((1, d, tn), lambda l, j: (l, 0, j)),
                  pl.BlockSpec((1, 1, tn), lambda l, j: (l, 0, j))],
        out_specs=pl.BlockSpec((1, r, tn), lambda l, j: (l, 0, j)),
        out_shape=jax.ShapeDtypeStruct((depth, r, n), F32),
        compiler_params=_cp("arbitrary", "arbitrary"), name="modulation",
    )(c_all, w_ada, b_ada.reshape(depth, 1, n))


def _ada_norm_bf16(x_ref, g_ref, sh_ref, sc_ref):
    x = x_ref[...]
    xn = x * lax.rsqrt(jnp.mean(x * x, axis=-1, keepdims=True) + EPS) * g_ref[...]
    return (xn * (1.0 + sc_ref[0]) + sh_ref[0]).astype(BF16)


def _rope128(x, tab, shift):
    xl = pltpu.roll(x, LANES - shift, 1)
    xr = pltpu.roll(x, shift, 1)
    return x * tab[0] + xl * tab[1] + xr * tab[2]


def _rope(u, tab, shift):
    return jnp.concatenate(
        [_rope128(u[:, c * LANES:(c + 1) * LANES], tab, shift) for c in range(u.shape[1] // LANES)], axis=1)


def _ab_in_body(x_ref, g_ref, sh_ref, sc_ref, tp_ref, tf_ref,
                wqa, wkv, wqi, wki, wbq, wbk, wbv, wbg,
                qa_o, kv_o, qi_o, kiw_o, bq_o, bk_o, bv_o, bg_o):
    h = _ada_norm_bf16(x_ref, g_ref, sh_ref, sc_ref)
    tp = tp_ref[...]
    tf = tf_ref[...]
    pshift, fshift = HEAD_DIM // 8, HEAD_DIM // 2
    qa_o[...] = _rope(_dot(h, wqa[...]), tp, pshift).astype(qa_o.dtype)
    u = _dot(h, wkv[...])
    kv_o[...] = jnp.concatenate([_rope128(u[:, :LANES], tp, pshift), u[:, LANES:]], axis=1)
    qi_o[...] = _rope(_dot(h, wqi[...]), tp, pshift).astype(qi_o.dtype)
    u = _dot(h, wki[...])
    kiw_o[...] = jnp.where(_iota(u.shape, 1) < IDX_DIM, _rope128(u, tp, pshift), u)
    bq_o[...] = _rope(_dot(h, wbq[...]), tf, fshift).astype(bq_o.dtype)
    bk_o[...] = _rope(_dot(h, wbk[...]), tf, fshift) * HEAD_DIM ** -0.5
    bv_o[...] = _dot(h, wbv[...]).astype(bv_o.dtype)
    bg_o[...] = _dot(h, wbg[...])


def _cd_in_body(x_ref, g_ref, sh_ref, sc_ref, tp_ref, gb_ref,
                wcq, wck, wcv, wco, wgt, wdq, wdk, wdv,
                cq_o, ck_o, cv_o, co_o, gt_o, dq_o, dk_o, dv_o):
    h = _ada_norm_bf16(x_ref, g_ref, sh_ref, sc_ref)
    tp = tp_ref[...]
    pshift = HEAD_DIM // 8
    cq_o[...] = _dot(h, wcq[...]).astype(cq_o.dtype)
    ck_o[...] = _dot(h, wck[...]) * HEAD_DIM ** -0.5
    cv_o[...] = _dot(h, wcv[...]).astype(cv_o.dtype)
    co_o[...] = _dot(h, wco[...])
    gt_o[...] = _dot(h, wgt[...]) + gb_ref[...]
    dq_o[...] = _rope(_dot(h, wdq[...]), tp, pshift).astype(dq_o.dtype)
    dk_o[...] = _rope(_dot(h, wdk[...]), tp, pshift)
    dv_o[...] = _dot(h, wdv[...])


def _in_proj(body, name, x, g, shift, scale, tables, extra, weights, out_defs, tm, tps):
    m, d = x.shape
    r = shift.shape[1]
    mod_spec = pl.BlockSpec((1, r, d), lambda i: (i // tps, 0, 0))
    in_specs = [pl.BlockSpec((tm, d), lambda i: (i, 0)), pl.BlockSpec((1, d), lambda i: (0, 0)), mod_spec, mod_spec]
    in_specs += [pl.BlockSpec((3, tm, LANES), lambda i: (0, i % tps, 0)) for _ in tables]
    in_specs += [pl.BlockSpec(e.shape, lambda i: (0, 0)) for e in extra]
    in_specs += [pl.BlockSpec(w.shape, lambda i: (0, 0)) for w in weights]
    return pl.pallas_call(
        body, grid=(m // tm,), in_specs=in_specs,
        out_specs=[pl.BlockSpec((tm, n), lambda i: (i, 0)) for n, _ in out_defs],
        out_shape=[jax.ShapeDtypeStruct((m, n), dt) for n, dt in out_defs],
        compiler_params=_cp("arbitrary"), name=name,
    )(x, g.reshape(1, d), shift, scale, *tables, *extra, *weights)


def _out_proj_body(x_ref, a_ref, b_ref, wa_ref, wb_ref, g_ref, gate_ref, o_ref):
    y = _dot(a_ref[...].astype(BF16), wa_ref[...]) + _dot(b_ref[...].astype(BF16), wb_ref[...])
    yn = y * lax.rsqrt(jnp.mean(y * y, axis=-1, keepdims=True) + EPS) * g_ref[...]
    o_ref[...] = x_ref[...] + gate_ref[0] * yn


def _out_proj(x, a, b, wa, wb, g, gate, tm, tps):
    m, d = x.shape
    r = gate.shape[1]
    return pl.pallas_call(
        _out_proj_body, grid=(m // tm,),
        in_specs=[pl.BlockSpec((tm, d), lambda i: (i, 0)),
                  pl.BlockSpec((tm, a.shape[1]), lambda i: (i, 0)),
                  pl.BlockSpec((tm, b.shape[1]), lambda i: (i, 0)),
                  pl.BlockSpec(wa.shape, lambda i: (0, 0)), pl.BlockSpec(wb.shape, lambda i: (0, 0)),
                  pl.BlockSpec((1, d), lambda i: (0, 0)),
                  pl.BlockSpec((1, r, d), lambda i: (i // tps, 0, 0))],
        out_specs=pl.BlockSpec((tm, d), lambda i: (i, 0)),
        out_shape=jax.ShapeDtypeStruct((m, d), F32),
        compiler_params=_cp("arbitrary"), name="out_proj",
    )(x, a, b, wa, wb, g.reshape(1, d), gate)


def _ffn_body(x_ref, g1_ref, sh_ref, sc_ref, wa_ref, wb_ref, wd_ref, cw_ref, cb_ref, inj_ref, g2_ref, gate_ref,
              o_ref, aux_ref, h_sc, acc_sc, carry_sc, *, tps, tf, inject):
    i = pl.program_id(0)
    f = pl.program_id(1)
    tm = x_ref.shape[0]

    @pl.when(f == 0)
    def _():
        h_sc[...] = _ada_norm_bf16(x_ref, g1_ref, sh_ref, sc_ref)
        acc_sc[...] = jnp.zeros_like(acc_sc)

    h = h_sc[...]
    a = _dot(h, wa_ref[...])
    b = _dot(h, wb_ref[...])
    row = _iota((tm, tf), 0)
    if inject:
        a = jnp.where(row % SLOT < SLOT_LO, inj_ref[...], a)
        aux_ref[...] = a
        p1 = jnp.zeros((1, tf), F32)
        p2 = jnp.zeros((2, tf), F32)
    else:
        @pl.when(i % tps == 0)
        def _():
            carry_sc[f] = jnp.zeros((2, tf), F32)

        prev = carry_sc[f]
        carry_sc[f] = a[tm - 2:, :]
        aux_ref[0] = a[tm - 2:, :]
        p1 = prev[1:2, :]
        p2 = prev
    a1 = jnp.where(row == 0, p1, pltpu.roll(a, 1, 0))
    a2 = pltpu.roll(a, 2, 0)
    a2 = jnp.where(row == 0, p2[0:1, :], jnp.where(row == 1, p2[1:2, :], a2))
    cw = cw_ref[...]
    conv = cb_ref[...] + a2 * cw[0:1, :]
    conv = conv + a1 * cw[1:2, :]
    conv = conv + a * cw[2:3, :]
    y = jax.nn.gelu(conv, approximate=True) * b
    acc_sc[...] += _dot(y.astype(BF16), wd_ref[...])

    @pl.when(f == pl.num_programs(1) - 1)
    def _():
        yv = acc_sc[...]
        yn = yv * lax.rsqrt(jnp.mean(yv * yv, axis=-1, keepdims=True) + EPS) * g2_ref[...]
        o_ref[...] = x_ref[...] + gate_ref[0] * yn


def _ffn(x, g1, shift, scale, wa, wb, wd, cw, cb, inj, g2, gate, tm, tps, tf):
    m, d = x.shape
    ff = wa.shape[1]
    nf = ff // tf
    r = shift.shape[1]
    inject = inj is not None
    if inject:
        aux_shape, aux_spec = (m, ff), pl.BlockSpec((tm, tf), lambda i, f: (i, f))
        inj_spec = pl.BlockSpec((tm, tf), lambda i, f: (i, f))
    else:
        aux_shape, aux_spec = (m // tm, 2, ff), pl.BlockSpec((1, 2, tf), lambda i, f: (i, 0, f))
        inj = jnp.zeros((8, LANES), F32)
        inj_spec = pl.BlockSpec((8, LANES), lambda i, f: (0, 0))
    mod_spec = pl.BlockSpec((1, r, d), lambda i, f: (i // tps, 0, 0))
    return pl.pallas_call(
        functools.partial(_ffn_body, tps=tps, tf=tf, inject=inject),
        grid=(m // tm, nf),
        in_specs=[pl.BlockSpec((tm, d), lambda i, f: (i, 0)), pl.BlockSpec((1, d), lambda i, f: (0, 0)),
                  mod_spec, mod_spec,
                  pl.BlockSpec((d, tf), lambda i, f: (0, f)), pl.BlockSpec((d, tf), lambda i, f: (0, f)),
                  pl.BlockSpec((tf, d), lambda i, f: (f, 0)),
                  pl.BlockSpec((CONV_W, tf), lambda i, f: (0, f)), pl.BlockSpec((1, tf), lambda i, f: (0, f)),
                  inj_spec, pl.BlockSpec((1, d), lambda i, f: (0, 0)), mod_spec],
        out_specs=[pl.BlockSpec((tm, d), lambda i, f: (i, 0)), aux_spec],
        out_shape=[jax.ShapeDtypeStruct((m, d), F32), jax.ShapeDtypeStruct(aux_shape, F32)],
        scratch_shapes=[pltpu.VMEM((tm, d), BF16), pltpu.VMEM((tm, d), F32), pltpu.VMEM((nf, 2, tf), F32)],
        compiler_params=_cp("arbitrary", "arbitrary"), name="conv_ffn",
    )(x, g1.reshape(1, d), shift, scale, wa, wb, wd, cw, cb.reshape(1, ff), inj, g2.reshape(1, d), gate)


def _kth_largest(stats_fn, cmin0, cmax0, k):
    def flag(lo, hi):
        return jnp.max(jnp.where(lo < hi, 1.0, 0.0))

    def body(st):
        cmin, cmax, _ = st
        mid = cmin + 0.5 * (cmax - cmin)
        p = jnp.where(mid > cmin, jnp.where(mid <= cmax, mid, cmax), cmax)
        cnt, mn_ge, mx_lt = stats_fn(p)
        is_open = cmin < cmax
        take_lo = cnt >= k
        ncmin = jnp.where(is_open, jnp.where(take_lo, mn_ge, cmin), cmin)
        ncmax = jnp.where(is_open, jnp.where(take_lo, cmax, mx_lt), cmax)
        return ncmin, ncmax, flag(ncmin, ncmax)

    cmin, _, _ = lax.while_loop(lambda st: st[2] > 0.0, body, (cmin0, cmax0, flag(cmin0, cmax0)))
    return cmin


def _lane_stats(x, p, st):
    cnt, mnge, mxlt = st
    for s in range(x.shape[1] // LANES):
        xs = x[:, s * LANES:(s + 1) * LANES]
        ge = xs >= p
        cnt = cnt + jnp.where(ge, 1.0, 0.0)
        mnge = jnp.minimum(mnge, jnp.where(ge, xs, jnp.inf))
        mxlt = jnp.maximum(mxlt, jnp.where(ge, -jnp.inf, xs))
    return cnt, mnge, mxlt


def _softmax_step(s, v_bf16, m_ref, l_ref, acc_ref, idx):
    m_prev = m_ref[idx]
    m_new = jnp.maximum(m_prev, jnp.max(s, axis=1, keepdims=True))
    alpha = jnp.exp(m_prev - m_new)
    p = jnp.exp(s - m_new[:, :1])
    l_ref[idx] = alpha * l_ref[idx] + jnp.sum(p, axis=1, keepdims=True)
    dv = acc_ref.shape[-1]
    a = alpha if dv == LANES else jnp.concatenate([alpha] * (dv // LANES), axis=1) if dv > LANES else alpha[:, :dv]
    acc_ref[idx] = a * acc_ref[idx] + _dot(p.astype(BF16), v_bf16)
    m_ref[idx] = m_new


def _tie_bias(x, t, need, carry, tri):
    eq = x == t
    pref = _dot(jnp.where(eq, 1.0, 0.0).astype(BF16), tri) + carry
    bias = jnp.where(x > t, 0.0, jnp.where(eq, jnp.where(pref <= need, 0.0, NEG), NEG))
    return bias, pref[:, x.shape[1] - 1:]


def _dsa_prompt_body(qa_ref, qi_ref, kiwq_ref, kiwk_ref, kv_ref, tri_ref, o_ref,
                     sc, m_sc, l_sc, acc_sc, *, tq, ch, topk):
    j = pl.program_id(1)
    q0 = j * tq
    nc = (q0 + tq + ch - 1) // ch
    row = q0 + _iota((tq, 1), 0)
    qi = qi_ref[0]
    qi4 = jnp.concatenate([qi[:, h * IDX_DIM:(h + 1) * IDX_DIM] for h in range(IDX_HEADS)], axis=0).astype(BF16)
    w = kiwq_ref[0][:, IDX_DIM:IDX_DIM + IDX_HEADS] * IDX_HEADS ** -0.5

    def score_chunk(c, st):
        mx, mn = st
        c0 = pl.multiple_of(c * ch, ch)
        kc = kiwk_ref[0, pl.ds(c0, ch), :][:, :IDX_DIM].astype(BF16)
        lg = _dot_nt(qi4, kc) * IDX_DIM ** -0.5
        s = jnp.zeros((tq, ch), F32)
        for h in range(IDX_HEADS):
            s = s + jnp.maximum(lg[h * tq:(h + 1) * tq], 0.0) * w[:, h:h + 1]
        s = jnp.where(s == 0.0, 0.0, s)
        valid = (c0 + _iota((tq, ch), 1)) <= row
        sc[c] = jnp.where(valid, s, -jnp.inf)
        mx = jnp.maximum(mx, jnp.max(jnp.where(valid, s, -jnp.inf), axis=1, keepdims=True))
        mn = jnp.minimum(mn, jnp.min(jnp.where(valid, s, jnp.inf), axis=1, keepdims=True))
        return mx, mn

    mx, mn = lax.fori_loop(0, nc, score_chunk, (jnp.full((tq, 1), -jnp.inf, F32), jnp.full((tq, 1), jnp.inf, F32)))
    active = row + 1 > topk
    cmin0 = jnp.where(active, mn, -jnp.inf)
    cmax0 = jnp.where(active, mx, -jnp.inf)

    def stats(p):
        init = (jnp.zeros((tq, LANES), F32), jnp.full((tq, LANES), jnp.inf, F32), jnp.full((tq, LANES), -jnp.inf, F32))
        cnt, mnge, mxlt = lax.fori_loop(0, nc, lambda c, st: _lane_stats(sc[c], p, st), init)
        return (jnp.sum(cnt, axis=1, keepdims=True), jnp.min(mnge, axis=1, keepdims=True),
                jnp.max(mxlt, axis=1, keepdims=True))

    t = _kth_largest(stats, cmin0, cmax0, float(topk))

    def count_gt(c, cnt):
        x = sc[c]
        for s in range(ch // LANES):
            cnt = cnt + jnp.where(x[:, s * LANES:(s + 1) * LANES] > t, 1.0, 0.0)
        return cnt

    n_gt = jnp.sum(lax.fori_loop(0, nc, count_gt, jnp.zeros((tq, LANES), F32)), axis=1, keepdims=True)
    need = float(topk) - n_gt

    qa = qa_ref[0]
    hpg = A_HEADS // A_KV_HEADS
    qg = [jnp.concatenate([qa[:, (g * hpg + h) * HEAD_DIM:(g * hpg + h + 1) * HEAD_DIM] for h in range(hpg)],
                          axis=0).astype(BF16) for g in range(A_KV_HEADS)]
    m_sc[...] = jnp.full(m_sc.shape, NEG, F32)
    l_sc[...] = jnp.zeros(l_sc.shape, F32)
    acc_sc[...] = jnp.zeros(acc_sc.shape, F32)
    tri = tri_ref[...]

    def attend(c, carry):
        c0 = pl.multiple_of(c * ch, ch)
        bias, carry = _tie_bias(sc[c], t, need, carry, tri)
        bias = jnp.where((c0 + _iota((tq, ch), 1)) <= row, bias, NEG)
        bias4 = jnp.concatenate([bias] * hpg, axis=0)
        kvc = kv_ref[0, pl.ds(c0, ch), :]
        for g in range(A_KV_HEADS):
            kc = kvc[:, g * HEAD_DIM:(g + 1) * HEAD_DIM].astype(BF16)
            vc = kvc[:, (A_KV_HEADS + g) * HEAD_DIM:(A_KV_HEADS + g + 1) * HEAD_DIM].astype(BF16)
            s = _dot_nt(qg[g], kc) * HEAD_DIM ** -0.5 + bias4
            _softmax_step(s, vc, m_sc, l_sc, acc_sc, g)
        return carry

    lax.fori_loop(0, nc, attend, jnp.zeros((tq, 1), F32))
    outs = []
    for g in range(A_KV_HEADS):
        og = acc_sc[g] / l_sc[g][:, :HEAD_DIM]
        outs += [og[h * tq:(h + 1) * tq] for h in range(hpg)]
    o_ref[0] = jnp.concatenate(outs, axis=1).astype(o_ref.dtype)


def _upper_tri_bf16(n):
    return jnp.asarray(np.triu(np.ones((n, n), np.float32)), BF16)


def _dsa_prompt(qa, qi, kiw, kv, tq=128, ch=512):
    b, s, _ = qa.shape
    ch = min(ch, s)
    topk = min(TOPK_MAX, s // 4)
    hpg = A_HEADS // A_KV_HEADS
    return pl.pallas_call(
        functools.partial(_dsa_prompt_body, tq=tq, ch=ch, topk=topk),
        grid=(b, s // tq),
        in_specs=[pl.BlockSpec((1, tq, qa.shape[2]), lambda i, j: (i, j, 0)),
                  pl.BlockSpec((1, tq, qi.shape[2]), lambda i, j: (i, j, 0)),
                  pl.BlockSpec((1, tq, LANES), lambda i, j: (i, j, 0)),
                  pl.BlockSpec((1, s, LANES), lambda i, j: (i, 0, 0)),
                  pl.BlockSpec((1, s, kv.shape[2]), lambda i, j: (i, 0, 0)),
                  pl.BlockSpec((ch, ch), lambda i, j: (0, 0))],
        out_specs=pl.BlockSpec((1, tq, A_HEADS * HEAD_DIM), lambda i, j: (i, j, 0)),
        out_shape=jax.ShapeDtypeStruct((b, s, A_HEADS * HEAD_DIM), BF16),
        scratch_shapes=[pltpu.VMEM((s // ch, tq, ch), F32),
                        pltpu.VMEM((A_KV_HEADS, hpg * tq, LANES), F32),
                        pltpu.VMEM((A_KV_HEADS, hpg * tq, LANES), F32),
                        pltpu.VMEM((A_KV_HEADS, hpg * tq, HEAD_DIM), F32)],
        compiler_params=_cp("arbitrary", "arbitrary"), name="dsa_prompt",
    )(qa, qi, kiw, kiw, kv, _upper_tri_bf16(ch))


def _head_layernorm(o, gain):
    mu = jnp.mean(o, axis=-1, keepdims=True)
    var = jnp.mean(jnp.square(o - mu), axis=-1, keepdims=True)
    return (o - mu) * lax.rsqrt(var + EPS) * gain


def _retention_body(q_ref, k_ref, v_ref, g_ref, s0_ref, dm_ref, qd_ref, kd_ref, cd_ref, gain_ref,
                    o_ref, sf_ref, s_sc):
    c = pl.program_id(1)

    @pl.when(c == 0)
    def _():
        s_sc[...] = s0_ref[0]

    q = q_ref[0].astype(BF16)
    k = k_ref[0]
    v = v_ref[0].astype(BF16)
    gt = g_ref[0]
    gain = gain_ref[...]
    outs = []
    for h in range(B_HEADS):
        sl = slice(h * HEAD_DIM, (h + 1) * HEAD_DIM)
        qh, kh, vh = q[:, sl], k[:, sl], v[:, sl]
        st = s_sc[h]
        att = _dot_nt(qh, kh.astype(BF16)) * dm_ref[h]
        o = _dot(att.astype(BF16), vh) + _dot(qh, st.astype(BF16)) * qd_ref[h]
        s_sc[h] = st * cd_ref[h] + _dot_tn((kh * kd_ref[h]).astype(BF16), vh)
        gh = gt[:, sl]
        outs.append(gh * jax.nn.sigmoid(gh) * _head_layernorm(o, gain[:, sl]))
    o_ref[0] = jnp.concatenate(outs, axis=1).astype(o_ref.dtype)

    @pl.when(c == pl.num_programs(1) - 1)
    def _():
        sf_ref[0] = s_sc[...]


def _retention(q, k, v, g, s0, gain, chunk, lo, hi, out_dtype):
    b, s, w = q.shape
    dm, qd, kd, cd = _retention_tables(chunk, lo, hi)
    row = lambda i, c: (i, c, 0)
    fix3 = lambda i, c: (0, 0, 0)
    st_spec = pl.BlockSpec((1, B_HEADS, HEAD_DIM, HEAD_DIM), lambda i, c: (i, 0, 0, 0))
    return pl.pallas_call(
        _retention_body, grid=(b, s // chunk),
        in_specs=[pl.BlockSpec((1, chunk, w), row)] * 4 + [st_spec] +
                 [pl.BlockSpec(t.shape, fix3) for t in (dm, qd, kd, cd)] + [pl.BlockSpec((1, w), lambda i, c: (0, 0))],
        out_specs=[pl.BlockSpec((1, chunk, w), row), st_spec],
        out_shape=[jax.ShapeDtypeStruct((b, s, w), out_dtype),
                   jax.ShapeDtypeStruct((b, B_HEADS, HEAD_DIM, HEAD_DIM), F32)],
        scratch_shapes=[pltpu.VMEM((B_HEADS, HEAD_DIM, HEAD_DIM), F32)],
        compiler_params=_cp("arbitrary", "arbitrary"), name="retention",
    )(q, k, v, g, s0, dm, qd, kd, cd, gain.reshape(1, w))


def _mlstm_body(q_ref, k_ref, v_ref, og_ref, gt_ref, c0_ref, n0_ref, m0_ref, gain_ref,
                o_ref, cf_ref, nf_ref, mf_ref, c_sc, n_sc, m_sc, *, lo, hi):
    c = pl.program_id(1)
    ck = q_ref.shape[1]
    nh = C_HEADS

    @pl.when(c == 0)
    def _():
        c_sc[...] = c0_ref[0]
        n_sc[...] = n0_ref[0]
        m0 = m0_ref[0]
        for h in range(nh):
            m_sc[h] = jnp.broadcast_to(m0[:, h:h + 1], (1, LANES))

    q = q_ref[0].astype(BF16)
    k = k_ref[0]
    v = v_ref[0].astype(BF16)
    og = og_ref[0]
    gates = gt_ref[0]
    gain = gain_ref[...]
    r1 = _iota((ck, 1), 0)
    tok = (r1 >= lo) & (r1 < hi)
    log_sig = jnp.minimum(gates, 0.0) - jnp.log1p(jnp.exp(-jnp.abs(gates)))
    lf = jnp.where(tok, log_sig, 0.0)
    ii = jnp.where(tok, gates, NEG)
    tril = jnp.where(_iota((ck, ck), 0) >= _iota((ck, ck), 1), 1.0, 0.0).astype(BF16)
    fcum = sum(_dot(tril, part) for part in _split3(lf))
    lane = _iota((ck, LANES), 1)
    z = jnp.where(lane < nh, ii, fcum)
    sel = jnp.where(_iota((2 * nh, LANES), 0) == _iota((2 * nh, LANES), 1), 1.0, 0.0).astype(BF16)
    zt = sum(_dot_nt(sel, part) for part in _split3(z))
    causal = _iota((ck, ck), 0) >= _iota((ck, ck), 1)
    outs = []
    for h in range(nh):
        sl = slice(h * HEAD_DIM, (h + 1) * HEAD_DIM)
        qh, kh, vh = q[:, sl], k[:, sl], v[:, sl]
        i_col, f_col = z[:, h:h + 1], z[:, nh + h:nh + h + 1]
        a_col = i_col - f_col
        a_row = zt[h:h + 1, :] - zt[nh + h:nh + h + 1, :]
        m_prev = m_sc[h][:, :1]
        cmax = jnp.max(jnp.where(causal, a_row, -jnp.inf), axis=1, keepdims=True)
        m_t = f_col + jnp.maximum(m_prev, cmax)
        dmat = jnp.exp(jnp.where(causal, f_col + a_row - m_t, -jnp.inf))
        inter = jnp.exp(f_col + m_prev - m_t)
        cst = c_sc[h]
        nst = n_sc[h]
        s = _dot_nt(qh, kh.astype(BF16)) * dmat
        num = _dot(s.astype(BF16), vh) + inter * _dot(qh, cst.astype(BF16))
        den = jnp.sum(s, axis=1, keepdims=True) + inter * jnp.sum(qh.astype(F32) * nst, axis=1, keepdims=True)
        hh = num / jnp.maximum(jnp.abs(den), jnp.exp(-m_t))
        f_last = f_col[ck - 1:, :]
        m_end = m_t[ck - 1:, :]
        wgt = jnp.exp(f_last + a_col - m_end)
        dec = jnp.exp(f_last + m_prev - m_end)
        kw = kh * wgt
        c_sc[h] = dec * cst + _dot_tn(kw.astype(BF16), vh)
        n_sc[h] = dec * nst + jnp.sum(kw, axis=0, keepdims=True)
        m_sc[h] = jnp.broadcast_to(m_end, (1, LANES))
        outs.append(_head_layernorm(jax.nn.sigmoid(og[:, sl]) * hh, gain[:, sl]))
    o_ref[0] = jnp.concatenate(outs, axis=1).astype(o_ref.dtype)

    @pl.when(c == pl.num_programs(1) - 1)
    def _():
        cf_ref[0] = c_sc[...]
        nf_ref[0] = n_sc[...]
        mf_ref[0] = jnp.concatenate([m_sc[h][:, :1] for h in range(nh)], axis=1)


def _mlstm(q, k, v, og, gates, c0, n0, m0, gain, chunk, lo, hi, out_dtype):
    b, s, w = q.shape
    nh = C_HEADS
    row = lambda i, c: (i, c, 0)
    c_spec = pl.BlockSpec((1, nh, HEAD_DIM, HEAD_DIM), lambda i, c: (i, 0, 0, 0))
    n_spec = pl.BlockSpec((1, nh, 1, HEAD_DIM), lambda i, c: (i, 0, 0, 0))
    m_spec = pl.BlockSpec((1, 1, nh), lambda i, c: (i, 0, 0))
    o, cf, nf, mf = pl.pallas_call(
        functools.partial(_mlstm_body, lo=lo, hi=hi), grid=(b, s // chunk),
        in_specs=[pl.BlockSpec((1, chunk, w), row)] * 4 + [pl.BlockSpec((1, chunk, LANES), row), c_spec, n_spec, m_spec,
                                                          pl.BlockSpec((1, w), lambda i, c: (0, 0))],
        out_specs=[pl.BlockSpec((1, chunk, w), row), c_spec, n_spec, m_spec],
        out_shape=[jax.ShapeDtypeStruct((b, s, w), out_dtype),
                   jax.ShapeDtypeStruct((b, nh, HEAD_DIM, HEAD_DIM), F32),
                   jax.ShapeDtypeStruct((b, nh, 1, HEAD_DIM), F32),
                   jax.ShapeDtypeStruct((b, 1, nh), F32)],
        scratch_shapes=[pltpu.VMEM((nh, HEAD_DIM, HEAD_DIM), F32), pltpu.VMEM((nh, 1, HEAD_DIM), F32),
                        pltpu.VMEM((nh, 1, LANES), F32)],
        compiler_params=_cp("arbitrary", "arbitrary"), name="mlstm",
    )(q, k, v, og, gates, c0, n0.reshape(b, nh, 1, HEAD_DIM), m0.reshape(b, 1, nh), gain.reshape(1, w))
    return o, cf, nf.reshape(b, nh, HEAD_DIM), mf.reshape(b, nh)


def _diff_lambda(lam_ref, lam_init):
    lp = lam_ref[...]
    s01 = jnp.sum(lp[0:1] * lp[1:2], axis=1, keepdims=True)
    s23 = jnp.sum(lp[2:3] * lp[3:4], axis=1, keepdims=True)
    return jnp.exp(s01) - jnp.exp(s23) + lam_init


def _diff_finish(o0, l0, o1, l1, lam, subln, lam_init):
    od = o0 / l0 - lam * (o1 / l1)
    return od * lax.rsqrt(jnp.mean(od * od, axis=-1, keepdims=True) + EPS) * subln * (1.0 - lam_init)


def _diff_prompt_body(q_ref, k_ref, v_ref, lam_ref, sub_ref, o_ref, m_sc, l_sc, acc_sc, *, lam_init):
    i = pl.program_id(1)
    j = pl.program_id(2)
    t = q_ref.shape[1]

    @pl.when(j == 0)
    def _():
        m_sc[...] = jnp.full(m_sc.shape, NEG, F32)
        l_sc[...] = jnp.zeros(l_sc.shape, F32)
        acc_sc[...] = jnp.zeros(acc_sc.shape, F32)

    def step(diag):
        q = q_ref[0]
        k = k_ref[0].astype(BF16)
        v = v_ref[0].astype(BF16)
        keep = _iota((t, t), 0) >= _iota((t, t), 1)
        for idx in range(2 * D_HEADS):
            h = idx % D_HEADS
            sl = slice(idx * HEAD_DIM, (idx + 1) * HEAD_DIM)
            s = _dot_nt(q[:, sl], k[:, sl]) * HEAD_DIM ** -0.5
            if diag:
                s = jnp.where(keep, s, NEG)
            _softmax_step(s, v[:, h * D_VDIM:(h + 1) * D_VDIM], m_sc, l_sc, acc_sc, idx)

    @pl.when(j < i)
    def _():
        step(False)

    @pl.when(j == i)
    def _():
        step(True)
        lam = _diff_lambda(lam_ref, lam_init)
        outs = [_diff_finish(acc_sc[h], l_sc[h], acc_sc[D_HEADS + h], l_sc[D_HEADS + h], lam, sub_ref[...], lam_init)
                for h in range(D_HEADS)]
        o_ref[0] = jnp.concatenate(outs, axis=1).astype(o_ref.dtype)


def _diff_prompt(q, k, v, lam_par, subln, lam_init, t=512):
    b, s, w = q.shape
    t = min(t, s)
    n = s // t
    return pl.pallas_call(
        functools.partial(_diff_prompt_body, lam_init=lam_init), grid=(b, n, n),
        in_specs=[pl.BlockSpec((1, t, w), lambda bi, i, j: (bi, i, 0)),
                  pl.BlockSpec((1, t, w), lambda bi, i, j: (bi, jnp.minimum(i, j), 0)),
                  pl.BlockSpec((1, t, w), lambda bi, i, j: (bi, jnp.minimum(i, j), 0)),
                  pl.BlockSpec(lam_par.shape, lambda bi, i, j: (0, 0)),
                  pl.BlockSpec((1, D_VDIM), lambda bi, i, j: (0, 0))],
        out_specs=pl.BlockSpec((1, t, w), lambda bi, i, j: (bi, i, 0)),
        out_shape=jax.ShapeDtypeStruct((b, s, w), BF16),
        scratch_shapes=[pltpu.VMEM((2 * D_HEADS, t, LANES), F32), pltpu.VMEM((2 * D_HEADS, t, LANES), F32),
                        pltpu.VMEM((2 * D_HEADS, t, D_VDIM), F32)],
        compiler_params=_cp("arbitrary", "arbitrary", "arbitrary"), name="diff_prompt",
    )(q, k, v, lam_par, subln.reshape(1, D_VDIM))


def _page_specs(width, layer, group, n_pages):
    def spec(g):
        def imap(b, j, pt):
            return (layer, pt[b, jnp.minimum(j * group + g, n_pages - 1)], 0, 0)
        return pl.BlockSpec((None, None, PAGE_SIZE, width), imap)
    return [spec(g) for g in range(group)]


def _new_key_valid(nq, nk):
    rq = _iota((nq, nk), 0) % SLOT
    rk = _iota((nq, nk), 1)
    return (rk >= SLOT_LO) & (rk < SLOT_LO + (SLOT - 2 * SLOT_LO)) & (rk <= rq)


def _idx_scores(qi4, w, keys_bf16):
    lg = _dot_nt(qi4, keys_bf16) * IDX_DIM ** -0.5
    s = jnp.zeros((SLOT, keys_bf16.shape[0]), F32)
    for h in range(IDX_HEADS):
        s = s + jnp.maximum(lg[h * SLOT:(h + 1) * SLOT], 0.0) * w[:, h:h + 1]
    return jnp.where(s == 0.0, 0.0, s)


def _dsa_sample_scores_body(pt_ref, qi_ref, kiw_ref, knew_ref, *rest, group):
    pages, o_ref = rest[:group], rest[group]
    j = pl.program_id(1)
    last = pl.num_programs(1) - 1
    qi = qi_ref[0]
    qi4 = jnp.concatenate([qi[:, h * IDX_DIM:(h + 1) * IDX_DIM] for h in range(IDX_HEADS)], axis=0).astype(BF16)
    w = kiw_ref[0][:, IDX_DIM:IDX_DIM + IDX_HEADS] * IDX_HEADS ** -0.5

    @pl.when(j < last)
    def _():
        keys = jnp.concatenate([p[...] for p in pages], axis=0).astype(BF16)
        o_ref[0, 0] = _idx_scores(qi4, w, keys)

    @pl.when(j == last)
    def _():
        s = _idx_scores(qi4, w, knew_ref[0][:, :IDX_DIM].astype(BF16))
        s = jnp.where(_new_key_valid(SLOT, PAGE_SIZE), s, -jnp.inf)
        pad = jnp.full((SLOT, (group - 1) * PAGE_SIZE), -jnp.inf, F32)
        o_ref[0, 0] = jnp.concatenate([s, pad], axis=1) if group > 1 else s


def _dsa_sample_scores(page_table, qi, kiw, knew, cache_idx, layer, group):
    db, n_pages = page_table.shape
    nch = n_pages // group + 1
    gw = group * PAGE_SIZE
    grid_spec = pltpu.PrefetchScalarGridSpec(
        num_scalar_prefetch=1, grid=(db, nch),
        in_specs=[pl.BlockSpec((1, SLOT, qi.shape[2]), lambda b, j, pt: (b, 0, 0)),
                  pl.BlockSpec((1, SLOT, LANES), lambda b, j, pt: (b, 0, 0)),
                  pl.BlockSpec((1, PAGE_SIZE, LANES), lambda b, j, pt: (b, 0, 0))]
                 + _page_specs(IDX_DIM, layer, group, n_pages),
        out_specs=pl.BlockSpec((1, 1, SLOT, gw), lambda b, j, pt: (b, j, 0, 0)))
    return pl.pallas_call(
        functools.partial(_dsa_sample_scores_body, group=group), grid_spec=grid_spec,
        out_shape=jax.ShapeDtypeStruct((db, nch, SLOT, gw), F32),
        compiler_params=_cp("arbitrary", "arbitrary"), name="dsa_sample_scores",
    )(page_table, qi, kiw, knew, *([cache_idx] * group))


def _dsa_sample_attend_body(pt_ref, qa_ref, sc_ref, kvnew_ref, tri_ref, *rest, group, topk):
    pages, o_ref = rest[:group], rest[group]
    t_sc, need_sc, carry_sc, m_sc, l_sc, acc_sc = rest[group + 1:]
    j = pl.program_id(1)
    last = pl.num_programs(1) - 1
    hpg = A_HEADS // A_KV_HEADS

    @pl.when(j == 0)
    def _():
        x = sc_ref[0]

        def stats(p):
            ge = x >= p[None]
            cnt = jnp.sum(jnp.sum(jnp.where(ge, 1.0, 0.0), axis=0), axis=1, keepdims=True)
            mnge = jnp.min(jnp.min(jnp.where(ge, x, jnp.inf), axis=0), axis=1, keepdims=True)
            mxlt = jnp.max(jnp.max(jnp.where(ge, -jnp.inf, x), axis=0), axis=1, keepdims=True)
            return cnt, mnge, mxlt

        cmax0 = jnp.max(jnp.max(x, axis=0), axis=1, keepdims=True)
        cmin0 = jnp.min(jnp.min(jnp.where(x == -jnp.inf, jnp.inf, x), axis=0), axis=1, keepdims=True)
        t = _kth_largest(stats, cmin0, cmax0, float(topk))
        n_gt = jnp.sum(jnp.sum(jnp.where(x > t[None], 1.0, 0.0), axis=0), axis=1, keepdims=True)
        t_sc[...] = jnp.broadcast_to(t, t_sc.shape)
        need_sc[...] = jnp.broadcast_to(float(topk) - n_gt, need_sc.shape)
        carry_sc[...] = jnp.zeros(carry_sc.shape, F32)
        m_sc[...] = jnp.full(m_sc.shape, NEG, F32)
        l_sc[...] = jnp.zeros(l_sc.shape, F32)
        acc_sc[...] = jnp.zeros(acc_sc.shape, F32)

    qa = qa_ref[0]
    qg = [jnp.concatenate([qa[:, (g * hpg + h) * HEAD_DIM:(g * hpg + h + 1) * HEAD_DIM] for h in range(hpg)],
                          axis=0).astype(BF16) for g in range(A_KV_HEADS)]
    t = t_sc[...][:, :1]
    need = need_sc[...][:, :1]
    tri = tri_ref[...]

    def attend(x, kv):
        carry = carry_sc[...][:, :1]
        biases = []
        for s in range(x.shape[1] // PAGE_SIZE):
            bias, carry = _tie_bias(x[:, s * PAGE_SIZE:(s + 1) * PAGE_SIZE], t, need, carry, tri)
            biases.append(bias)
        carry_sc[...] = jnp.broadcast_to(carry, carry_sc.shape)
        bias = jnp.concatenate(biases, axis=1) if len(biases) > 1 else biases[0]
        bias4 = jnp.concatenate([bias] * hpg, axis=0)
        for g in range(A_KV_HEADS):
            kc = kv[:, g * HEAD_DIM:(g + 1) * HEAD_DIM].astype(BF16)
            vc = kv[:, (A_KV_HEADS + g) * HEAD_DIM:(A_KV_HEADS + g + 1) * HEAD_DIM].astype(BF16)
            s = _dot_nt(qg[g], kc) * HEAD_DIM ** -0.5 + bias4
            _softmax_step(s, vc, m_sc, l_sc, acc_sc, g)

    @pl.when(j < last)
    def _():
        attend(sc_ref[0, j], jnp.concatenate([p[...] for p in pages], axis=0))

    @pl.when(j == last)
    def _():
        attend(sc_ref[0, j][:, :PAGE_SIZE], kvnew_ref[0])
        outs = []
        for g in range(A_KV_HEADS):
            og = acc_sc[g] / l_sc[g][:, :HEAD_DIM]
            outs += [og[h * SLOT:(h + 1) * SLOT] for h in range(hpg)]
        o_ref[0] = jnp.concatenate(outs, axis=1)


def _dsa_sample_attend(page_table, qa, scores, kvnew, cache_kv, layer, group, topk):
    db, n_pages = page_table.shape
    nch = n_pages // group + 1
    gw = group * PAGE_SIZE
    hpg = A_HEADS // A_KV_HEADS
    kvw = 2 * A_KV_HEADS * HEAD_DIM
    grid_spec = pltpu.PrefetchScalarGridSpec(
        num_scalar_prefetch=1, grid=(db, nch),
        in_specs=[pl.BlockSpec((1, SLOT, qa.shape[2]), lambda b, j, pt: (b, 0, 0)),
                  pl.BlockSpec((1, nch, SLOT, gw), lambda b, j, pt: (b, 0, 0, 0)),
                  pl.BlockSpec((1, PAGE_SIZE, kvw), lambda b, j, pt: (b, 0, 0)),
                  pl.BlockSpec((PAGE_SIZE, PAGE_SIZE), lambda b, j, pt: (0, 0))]
                 + _page_specs(kvw, layer, group, n_pages),
        out_specs=pl.BlockSpec((1, SLOT, A_HEADS * HEAD_DIM), lambda b, j, pt: (b, 0, 0)),
        scratch_shapes=[pltpu.VMEM((SLOT, LANES), F32), pltpu.VMEM((SLOT, LANES), F32), pltpu.VMEM((SLOT, LANES), F32),
                        pltpu.VMEM((A_KV_HEADS, hpg * SLOT, LANES), F32),
                        pltpu.VMEM((A_KV_HEADS, hpg * SLOT, LANES), F32),
                        pltpu.VMEM((A_KV_HEADS, hpg * SLOT, HEAD_DIM), F32)])
    return pl.pallas_call(
        functools.partial(_dsa_sample_attend_body, group=group, topk=topk), grid_spec=grid_spec,
        out_shape=jax.ShapeDtypeStruct((db, SLOT, A_HEADS * HEAD_DIM), F32),
        compiler_params=_cp("arbitrary", "arbitrary"), name="dsa_sample_attend",
    )(page_table, qa, scores, kvnew, _upper_tri_bf16(PAGE_SIZE), *([cache_kv] * group))


def _diff_sample_body(pt_ref, q_ref, knew_ref, vnew_ref, lam_ref, sub_ref, *rest, group, lam_init):
    kpages, vpages, o_ref = rest[:group], rest[group:2 * group], rest[2 * group]
    m_sc, l_sc, acc_sc = rest[2 * group + 1:]
    j = pl.program_id(1)
    last = pl.num_programs(1) - 1
    nmap = 2 * D_HEADS
    w = nmap * HEAD_DIM

    @pl.when(j == 0)
    def _():
        m_sc[...] = jnp.full(m_sc.shape, NEG, F32)
        l_sc[...] = jnp.zeros(l_sc.shape, F32)
        acc_sc[...] = jnp.zeros(acc_sc.shape, F32)

    q = q_ref[0]
    lane_map = _iota((SLOT, w), 1) // HEAD_DIM
    qb = jnp.concatenate([jnp.where(lane_map == idx, q, 0.0) for idx in range(nmap)], axis=0).astype(BF16)

    def step(k, v, mask):
        s = _dot_nt(qb, k.astype(BF16)) * HEAD_DIM ** -0.5
        if mask is not None:
            s = jnp.where(mask, s, NEG)
        _softmax_step(s, v.astype(BF16), m_sc, l_sc, acc_sc, 0)

    @pl.when(j < last)
    def _():
        step(jnp.concatenate([p[...] for p in kpages], axis=0), jnp.concatenate([p[...] for p in vpages], axis=0), None)

    @pl.when(j == last)
    def _():
        step(knew_ref[0], vnew_ref[0], _new_key_valid(nmap * SLOT, PAGE_SIZE))
        lam = _diff_lambda(lam_ref, lam_init)
        acc = acc_sc[0]
        l = l_sc[0]
        outs = []
        for h in range(D_HEADS):
            r0, r1 = h * SLOT, (D_HEADS + h) * SLOT
            vs = slice(h * D_VDIM, (h + 1) * D_VDIM)
            outs.append(_diff_finish(acc[r0:r0 + SLOT, vs], l[r0:r0 + SLOT], acc[r1:r1 + SLOT, vs], l[r1:r1 + SLOT],
                                     lam, sub_ref[...], lam_init))
        o_ref[0] = jnp.concatenate(outs, axis=1)


def _diff_sample(page_table, q, knew, vnew, lam_par, subln, cache_k, cache_v, layer, group, lam_init):
    db, n_pages = page_table.shape
    nch = n_pages // group + 1
    nmap = 2 * D_HEADS
    w = q.shape[2]
    vw = D_HEADS * D_VDIM
    grid_spec = pltpu.PrefetchScalarGridSpec(
        num_scalar_prefetch=1, grid=(db, nch),
        in_specs=[pl.BlockSpec((1, SLOT, w), lambda b, j, pt: (b, 0, 0)),
                  pl.BlockSpec((1, PAGE_SIZE, w), lambda b, j, pt: (b, 0, 0)),
                  pl.BlockSpec((1, PAGE_SIZE, vw), lambda b, j, pt: (b, 0, 0)),
                  pl.BlockSpec(lam_par.shape, lambda b, j, pt: (0, 0)),
                  pl.BlockSpec((1, D_VDIM), lambda b, j, pt: (0, 0))]
                 + _page_specs(w, layer, group, n_pages) + _page_specs(vw, layer, group, n_pages),
        out_specs=pl.BlockSpec((1, SLOT, vw), lambda b, j, pt: (b, 0, 0)),
        scratch_shapes=[pltpu.VMEM((1, nmap * SLOT, LANES), F32), pltpu.VMEM((1, nmap * SLOT, LANES), F32),
                        pltpu.VMEM((1, nmap * SLOT, vw), F32)])
    return pl.pallas_call(
        functools.partial(_diff_sample_body, group=group, lam_init=lam_init), grid_spec=grid_spec,
        out_shape=jax.ShapeDtypeStruct((db, SLOT, vw), F32),
        compiler_params=_cp("arbitrary", "arbitrary"), name="diff_sample",
    )(page_table, q, knew, vnew, lam_par, subln.reshape(1, D_VDIM), *([cache_k] * group), *([cache_v] * group))


def _split_cols(w, sizes):
    out, o = [], 0
    for s in sizes:
        out.append(w[:, o:o + s])
        o += s
    return out


def _pad_cols(w, n):
    return jnp.pad(w, ((0, 0), (0, n - w.shape[1])))


def _pad_rows(a, n):
    return jnp.pad(a, ((0, 0), (0, n - a.shape[1]), (0, 0)))


def _ab_weights(w_in):
    hd = HEAD_DIM
    aq, ak, av, iq, ik, iw, bq, bk, bv, bg = _split_cols(
        w_in, (A_HEADS * hd, A_KV_HEADS * hd, A_KV_HEADS * hd, IDX_HEADS * IDX_DIM, IDX_DIM, IDX_HEADS,
               B_HEADS * hd, B_HEADS * hd, B_HEADS * hd, B_HEADS * hd))
    segs = [aq, jnp.concatenate([ak, av], axis=1), iq, _pad_cols(jnp.concatenate([ik, iw], axis=1), LANES), bq, bk, bv, bg]
    return [s.astype(BF16) for s in segs]


def _cd_weights(w_in):
    hd = HEAD_DIM
    cq, ck, cv, ci, cf, co, dq, dk, dv = _split_cols(
        w_in, (C_HEADS * hd, C_HEADS * hd, C_HEADS * hd, C_HEADS, C_HEADS, C_HEADS * hd,
               2 * D_HEADS * hd, 2 * D_HEADS * hd, D_HEADS * D_VDIM))
    segs = [cq, ck, cv, co, _pad_cols(jnp.concatenate([ci, cf], axis=1), LANES), dq, dk, dv]
    return [s.astype(BF16) for s in segs]


def kernel(x_prompt, x_sample, c_prompt, c_sample, page_table, cache_a_kv, cache_a_idx, cache_d_k, cache_d_v, state_ret, state_mlstm_c, state_mlstm_n, state_mlstm_m, state_ffn_conv, norm_pre_mix, norm_post_mix, norm_pre_ffn, norm_post_ffn, w_ada, b_ada, w_ab_in, w_ab_out, ret_gain, w_cd_in, w_cd_out, c_gate_bias, c_gain, d_lambda, d_subln, ffn_w_up, ffn_conv_w, ffn_conv_b, ffn_w_down):
    b, s, d = x_prompt.shape
    db, t, _ = x_sample.shape
    depth = w_ada.shape[0]
    n_pages = page_table.shape[1]
    past = n_pages * PAGE_SIZE
    d_ff = ffn_w_down.shape[1]
    assert t == SLOT - 2 * SLOT_LO and past >= TOPK_MAX and s % 1024 == 0
    hi = SLOT_LO + t
    ms = db * SLOT
    tm_in, tm_ffn, tf = 512, 1024, 256
    group = 16
    topk_s = min(TOPK_MAX, (past + t) // 4)

    n_c = b + db
    c_all = jnp.pad(jnp.concatenate([c_prompt, c_sample], axis=0), ((0, -n_c % 8), (0, 0)))
    mods = _modulation(c_all, w_ada, b_ada)

    xp = x_prompt.reshape(b * s, d)
    xs = jnp.pad(x_sample, ((0, 0), (SLOT_LO, SLOT - hi), (0, 0))).reshape(ms, d)

    pos_s = past + (np.arange(ms) % SLOT) - SLOT_LO
    tp_p, tf_p = _rope_tables(np.arange(s), HEAD_DIM // 4, ROPE_THETA), _rope_tables(np.arange(s), HEAD_DIM, RET_THETA)
    tp_s, tf_s = _rope_tables(pos_s, HEAD_DIM // 4, ROPE_THETA), _rope_tables(pos_s, HEAD_DIM, RET_THETA)

    cache_a_kv = cache_a_kv.reshape(*cache_a_kv.shape[:3], -1)
    cache_d_k = cache_d_k.reshape(*cache_d_k.shape[:3], -1)
    cache_d_v = cache_d_v.reshape(*cache_d_v.shape[:3], -1)

    hd = HEAD_DIM
    w512 = 8 * hd
    ab_defs = lambda dt: [(w512, dt), (4 * hd, F32), (4 * hd, dt), (LANES, F32), (w512, dt), (w512, F32), (w512, dt), (w512, F32)]
    cd_defs = lambda dt: [(w512, dt), (w512, F32), (w512, dt), (w512, F32), (LANES, F32), (w512, dt), (w512, F32), (w512, F32)]
    zeros = lambda *shape: jnp.zeros(shape, F32)

    outs = {k: [] for k in ("pa_kv", "pa_idx", "pd_k", "pd_v", "p_ret", "p_mc", "p_mn", "p_mm", "p_cv",
                            "sa_kv", "sa_idx", "sd_k", "sd_v", "s_ret", "s_mc", "s_mn", "s_mm", "s_cv")}
    for l in range(depth):
        p = l // 2
        m6 = mods[l, :n_c].reshape(n_c, 6, d)
        mp = [m6[:b, i][:, None, :] for i in range(6)]
        msm = [jnp.repeat(m6[b:, i], SLOT, axis=0)[None] for i in range(6)]
        if l % 2 == 0:
            wts = _ab_weights(w_ab_in[p])
            wo = w_ab_out[p].astype(BF16)
            qa, kv, qi, kiw, bq, bk, bv, bg = _in_proj(_ab_in_body, "ab_in", xp, norm_pre_mix[l], mp[0], mp[1],
                                                       [tp_p, tf_p], [], wts, ab_defs(BF16), tm_in, s // tm_in)
            r3 = lambda a: a.reshape(b, s, a.shape[-1])
            a_out = _dsa_prompt(r3(qa), r3(qi), r3(kiw), r3(kv))
            b_out, st = _retention(r3(bq), r3(bk), r3(bv), r3(bg), zeros(b, B_HEADS, hd, hd), ret_gain[p],
                                   CHUNK, 0, CHUNK, BF16)
            outs["pa_kv"].append(kv.reshape(b, s, 2, A_KV_HEADS, hd))
            outs["pa_idx"].append(r3(kiw)[:, :, :IDX_DIM])
            outs["p_ret"].append(st)
            xp = _out_proj(xp, a_out.reshape(b * s, -1), b_out.reshape(b * s, -1), wo[:w512], wo[w512:],
                           norm_post_mix[l], mp[2], tm_in, s // tm_in)
            qa, kv, qi, kiw, bq, bk, bv, bg = _in_proj(_ab_in_body, "ab_in", xs, norm_pre_mix[l], msm[0], msm[1],
                                                       [tp_s, tf_s], [], wts, ab_defs(F32), ms, 1)
            r3 = lambda a: a.reshape(db, SLOT, a.shape[-1])
            scores = _dsa_sample_scores(page_table, r3(qi), r3(kiw), _pad_rows(r3(kiw), PAGE_SIZE), cache_a_idx, p, group)
            a_out = _dsa_sample_attend(page_table, r3(qa), scores, _pad_rows(r3(kv), PAGE_SIZE), cache_a_kv, p, group, topk_s)
            b_out, st = _retention(r3(bq), r3(bk), r3(bv), r3(bg), state_ret[p], ret_gain[p], SLOT, SLOT_LO, hi, F32)
            outs["sa_kv"].append(r3(kv)[:, SLOT_LO:hi].reshape(db, t, 2, A_KV_HEADS, hd))
            outs["sa_idx"].append(r3(kiw)[:, SLOT_LO:hi, :IDX_DIM])
            outs["s_ret"].append(st)
            xs = _out_proj(xs, a_out.reshape(ms, -1), b_out.reshape(ms, -1), wo[:w512], wo[w512:],
                           norm_post_mix[l], msm[2], ms, 1)
        else:
            lam_init = 0.8 - 0.6 * math.exp(-0.3 * l)
            wts = _cd_weights(w_cd_in[p])
            wo = w_cd_out[p].astype(BF16)
            gbias = jnp.pad(c_gate_bias[p], (0, LANES - 2 * C_HEADS)).reshape(1, LANES)
            cq, ck, cv, co, gt, dq, dk, dv = _in_proj(_cd_in_body, "cd_in", xp, norm_pre_mix[l], mp[0], mp[1],
                                                      [tp_p], [gbias], wts, cd_defs(BF16), tm_in, s // tm_in)
            r3 = lambda a: a.reshape(b, s, a.shape[-1])
            c_out, mc, mn, mm = _mlstm(r3(cq), r3(ck), r3(cv), r3(co), r3(gt), zeros(b, C_HEADS, hd, hd),
                                       zeros(b, C_HEADS, hd), zeros(b, C_HEADS), c_gain[p], CHUNK, 0, CHUNK, BF16)
            d_out = _diff_prompt(r3(dq), r3(dk), r3(dv), d_lambda[p], d_subln[p], lam_init)
            outs["pd_k"].append(dk.reshape(b, s, 2, D_HEADS, hd))
            outs["pd_v"].append(dv.reshape(b, s, D_HEADS, D_VDIM))
            outs["p_mc"].append(mc); outs["p_mn"].append(mn); outs["p_mm"].append(mm)
            xp = _out_proj(xp, c_out.reshape(b * s, -1), d_out.reshape(b * s, -1), wo[:w512], wo[w512:],
                           norm_post_mix[l], mp[2], tm_in, s // tm_in)
            cq, ck, cv, co, gt, dq, dk, dv = _in_proj(_cd_in_body, "cd_in", xs, norm_pre_mix[l], msm[0], msm[1],
                                                      [tp_s], [gbias], wts, cd_defs(F32), ms, 1)
            r3 = lambda a: a.reshape(db, SLOT, a.shape[-1])
            c_out, mc, mn, mm = _mlstm(r3(cq), r3(ck), r3(cv), r3(co), r3(gt), state_mlstm_c[p], state_mlstm_n[p],
                                       state_mlstm_m[p], c_gain[p], SLOT, SLOT_LO, hi, F32)
            d_out = _diff_sample(page_table, r3(dq), _pad_rows(r3(dk), PAGE_SIZE), _pad_rows(r3(dv), PAGE_SIZE),
                                 d_lambda[p], d_subln[p], cache_d_k, cache_d_v, p, group, lam_init)
            outs["sd_k"].append(r3(dk)[:, SLOT_LO:hi].reshape(db, t, 2, D_HEADS, hd))
            outs["sd_v"].append(r3(dv)[:, SLOT_LO:hi].reshape(db, t, D_HEADS, D_VDIM))
            outs["s_mc"].append(mc); outs["s_mn"].append(mn); outs["s_mm"].append(mm)
            xs = _out_proj(xs, c_out.reshape(ms, -1), d_out.reshape(ms, -1), wo[:w512], wo[w512:],
                           norm_post_mix[l], msm[2], ms, 1)

        wup = ffn_w_up[l].astype(BF16)
        wa, wb, wd = wup[:, :d_ff], wup[:, d_ff:], ffn_w_down[l].astype(BF16)
        xp, tails = _ffn(xp, norm_pre_ffn[l], mp[3], mp[4], wa, wb, wd, ffn_conv_w[l], ffn_conv_b[l], None,
                         norm_post_ffn[l], mp[5], tm_ffn, s // tm_ffn, tf)
        outs["p_cv"].append(tails.reshape(b, s // tm_ffn, 2, d_ff)[:, -1])
        inj = jnp.pad(state_ffn_conv[l], ((0, 0), (0, SLOT - SLOT_LO), (0, 0))).reshape(ms, d_ff)
        xs, a_all = _ffn(xs, norm_pre_ffn[l], msm[3], msm[4], wa, wb, wd, ffn_conv_w[l], ffn_conv_b[l], inj,
                         norm_post_ffn[l], msm[5], ms, 1, tf)
        outs["s_cv"].append(a_all.reshape(db, SLOT, d_ff)[:, hi - (CONV_W - 1):hi])

    st = lambda k: jnp.stack(outs[k])
    y_sample = xs.reshape(db, SLOT, d)[:, SLOT_LO:hi]
    return (xp.reshape(b, s, d), y_sample,
            st("pa_kv"), st("pa_idx"), st("pd_k"), st("pd_v"), st("p_ret"), st("p_mc"), st("p_mn"), st("p_mm"), st("p_cv"),
            st("sa_kv"), st("sa_idx"), st("sd_k"), st("sd_v"), st("s_ret"), st("s_mc"), st("s_mn"), st("s_mm"), st("s_cv"))
```

```python
import functools
import math

import numpy as np
import jax
import jax.numpy as jnp
from jax import lax
from jax.experimental import pallas as pl
from jax.experimental.pallas import tpu as pltpu

F32 = jnp.float32
BF16 = jnp.bfloat16

PAGE_SIZE = 128
HEAD_DIM = 64
ROPE_THETA = 500000.0
RET_THETA = 10000.0
A_HEADS = 8
A_KV_HEADS = 2
IDX_HEADS = 4
IDX_DIM = 64
TOPK_MAX = 256
B_HEADS = 8
C_HEADS = 8
D_HEADS = 4
D_VDIM = 2 * HEAD_DIM
CONV_W = 3
CHUNK = 128
EPS = 1e-6
NEG = -1e30
LOG2E = 1.4426950408889634
SLOT = 8
SLOT_LO = CONV_W - 1
LANES = 128
VMEM_LIMIT = 56 * 1024 * 1024

_NT = (((1,), (1,)), ((), ()))


def _cp(*sem):
    return pltpu.CompilerParams(dimension_semantics=sem, vmem_limit_bytes=VMEM_LIMIT)


def _dot(a, b):
    return jnp.dot(a, b, preferred_element_type=F32)


def _dot_nt(a, b):
    return lax.dot_general(a, b, _NT, preferred_element_type=F32)


def _iota(shape, dim):
    return lax.broadcasted_iota(jnp.int32, shape, dim)


def _eye_bf16(n):
    return jnp.where(_iota((n, n), 0) == _iota((n, n), 1), 1.0, 0.0).astype(BF16)


def _dot_tn(a_bf16, b_bf16):
    at = _dot_nt(_eye_bf16(a_bf16.shape[1]), a_bf16).astype(BF16)
    return _dot(at, b_bf16)


def _split3(x):
    hi = x.astype(BF16)
    r1 = x - hi.astype(F32)
    mid = r1.astype(BF16)
    lo = (r1 - mid.astype(F32)).astype(BF16)
    return hi, mid, lo


def _rope_tables(pos, rot, theta):
    half = rot // 2
    inv = theta ** (-np.arange(half, dtype=np.float64) / half)
    ang = np.asarray(pos, np.float64)[:, None] * inv[None]
    cos, sin = np.cos(ang), np.sin(ang)
    n = len(pos)
    t0 = np.ones((n, HEAD_DIM)); t1 = np.zeros((n, HEAD_DIM)); t2 = np.zeros((n, HEAD_DIM))
    t0[:, :half] = cos; t0[:, half:rot] = cos
    t1[:, :half] = -sin
    t2[:, half:rot] = sin
    tab = np.stack([np.tile(t, (1, LANES // HEAD_DIM)) for t in (t0, t1, t2)])
    return jnp.asarray(tab, F32)


def _retention_tables(chunk, lo, hi):
    n = hi - lo
    log_g = np.log1p(-np.exp2(-5.0 - np.arange(B_HEADS, dtype=np.float64)))
    r = np.arange(chunk)
    ok = (r >= lo) & (r < hi)
    pos = (r - lo).astype(np.float64)
    diff = pos[:, None] - pos[None, :]
    dm = np.where((diff >= 0) & ok[:, None] & ok[None, :], np.exp(log_g[:, None, None] * np.maximum(diff, 0.0)), 0.0)
    qd = np.exp(log_g[:, None] * (pos + 1.0))[:, :, None] * np.ones((1, 1, HEAD_DIM))
    kd = np.where(ok, np.exp(log_g[:, None] * (n - 1.0 - pos)), 0.0)[:, :, None] * np.ones((1, 1, HEAD_DIM))
    cd = np.exp(log_g * n)[:, None, None] * np.ones((1, 1, HEAD_DIM))
    return tuple(jnp.asarray(a, F32) for a in (dm, qd, kd, cd))


def _mod_body(c_ref, w_ref, b_ref, o_ref):
    c = c_ref[...]
    s = (c * jax.nn.sigmoid(c)).astype(BF16)
    o_ref[0] = _dot(s, w_ref[0].astype(BF16)) + b_ref[0]


def _modulation(c_all, w_ada, b_ada):
    depth, d, n = w_ada.shape
    r = c_all.shape[0]
    tn = 1536
    return pl.pallas_call(
        _mod_body, grid=(depth, n // tn),
        in_specs=[pl.BlockSpec((r, d), lambda l, j: (0, 0)),
                  pl.BlockSpec((1, d, tn), lambda l, j: (l, 0, j)),
                  pl.BlockSpec((1, 1, tn), lambda l, j: (l, 0, j))],
        out_specs=pl.BlockSpec((1, r, tn), lambda l, j: (l, 0, j)),
        out_shape=jax.ShapeDtypeStruct((depth, r, n), F32),
        compiler_params=_cp("arbitrary", "arbitrary"), name="modulation",
    )(c_all, w_ada, b_ada.reshape(depth, 1, n))


def _ada_norm_bf16(x_ref, g_ref, sh_ref, sc_ref):
    x = x_ref[...]
    xn = x * lax.rsqrt(jnp.mean(x * x, axis=-1, keepdims=True) + EPS) * g_ref[...]
    return (xn * (1.0 + sc_ref[0]) + sh_ref[0]).astype(BF16)


def _rope128(x, tab, shift):
    xl = pltpu.roll(x, LANES - shift, 1)
    xr = pltpu.roll(x, shift, 1)
    return x * tab[0] + xl * tab[1] + xr * tab[2]


def _rope(u, tab, shift):
    return jnp.concatenate(
        [_rope128(u[:, c * LANES:(c + 1) * LANES], tab, shift) for c in range(u.shape[1] // LANES)], axis=1)


def _ab_in_body(x_ref, g_ref, sh_ref, sc_ref, tp_ref, tf_ref,
                wqa, wkv, wqi, wki, wbq, wbk, wbv, wbg,
                qa_o, kv_o, qi_o, kiw_o, bq_o, bk_o, bv_o, bg_o):
    h = _ada_norm_bf16(x_ref, g_ref, sh_ref, sc_ref)
    tp = tp_ref[...]
    tf = tf_ref[...]
    pshift, fshift = HEAD_DIM // 8, HEAD_DIM // 2
    qa_o[...] = _rope(_dot(h, wqa[...]), tp, pshift).astype(qa_o.dtype)
    u = _dot(h, wkv[...])
    kv_o[...] = jnp.concatenate([_rope128(u[:, :LANES], tp, pshift), u[:, LANES:]], axis=1)
    qi_o[...] = _rope(_dot(h, wqi[...]), tp, pshift).astype(qi_o.dtype)
    u = _dot(h, wki[...])
    kiw_o[...] = jnp.where(_iota(u.shape, 1) < IDX_DIM, _rope128(u, tp, pshift), u)
    bq_o[...] = _rope(_dot(h, wbq[...]), tf, fshift).astype(bq_o.dtype)
    bk_o[...] = _rope(_dot(h, wbk[...]), tf, fshift) * HEAD_DIM ** -0.5
    bv_o[...] = _dot(h, wbv[...]).astype(bv_o.dtype)
    bg_o[...] = _dot(h, wbg[...])


def _cd_in_body(x_ref, g_ref, sh_ref, sc_ref, tp_ref, gb_ref,
                wcq, wck, wcv, wco, wgt, wdq, wdk, wdv,
                cq_o, ck_o, cv_o, co_o, gt_o, dq_o, dk_o, dv_o):
    h = _ada_norm_bf16(x_ref, g_ref, sh_ref, sc_ref)
    tp = tp_ref[...]
    pshift = HEAD_DIM // 8
    cq_o[...] = _dot(h, wcq[...]).astype(cq_o.dtype)
    ck_o[...] = _dot(h, wck[...]) * HEAD_DIM ** -0.5
    cv_o[...] = _dot(h, wcv[...]).astype(cv_o.dtype)
    co_o[...] = _dot(h, wco[...])
    gt_o[...] = _dot(h, wgt[...]) + gb_ref[...]
    dq_o[...] = _rope(_dot(h, wdq[...]), tp, pshift).astype(dq_o.dtype)
    dk_o[...] = _rope(_dot(h, wdk[...]), tp, pshift)
    dv_o[...] = _dot(h, wdv[...])


def _in_proj(body, name, x, g, shift, scale, tables, extra, weights, out_defs, tm, tps):
    m, d = x.shape
    r = shift.shape[1]
    mod_spec = pl.BlockSpec((1, r, d), lambda i: (i // tps, 0, 0))
    in_specs = [pl.BlockSpec((tm, d), lambda i: (i, 0)), pl.BlockSpec((1, d), lambda i: (0, 0)), mod_spec, mod_spec]
    in_specs += [pl.BlockSpec((3, tm, LANES), lambda i: (0, i % tps, 0)) for _ in tables]
    in_specs += [pl.BlockSpec(e.shape, lambda i: (0, 0)) for e in extra]
    in_specs += [pl.BlockSpec(w.shape, lambda i: (0, 0)) for w in weights]
    return pl.pallas_call(
        body, grid=(m // tm,), in_specs=in_specs,
        out_specs=[pl.BlockSpec((tm, n), lambda i: (i, 0)) for n, _ in out_defs],
        out_shape=[jax.ShapeDtypeStruct((m, n), dt) for n, dt in out_defs],
        compiler_params=_cp("arbitrary"), name=name,
    )(x, g.reshape(1, d), shift, scale, *tables, *extra, *weights)


def _out_proj_body(x_ref, a_ref, b_ref, wa_ref, wb_ref, g_ref, gate_ref, o_ref):
    y = _dot(a_ref[...].astype(BF16), wa_ref[...]) + _dot(b_ref[...].astype(BF16), wb_ref[...])
    yn = y * lax.rsqrt(jnp.mean(y * y, axis=-1, keepdims=True) + EPS) * g_ref[...]
    o_ref[...] = x_ref[...] + gate_ref[0] * yn


def _out_proj(x, a, b, wa, wb, g, gate, tm, tps):
    m, d = x.shape
    r = gate.shape[1]
    return pl.pallas_call(
        _out_proj_body, grid=(m // tm,),
        in_specs=[pl.BlockSpec((tm, d), lambda i: (i, 0)),
                  pl.BlockSpec((tm, a.shape[1]), lambda i: (i, 0)),
                  pl.BlockSpec((tm, b.shape[1]), lambda i: (i, 0)),
                  pl.BlockSpec(wa.shape, lambda i: (0, 0)), pl.BlockSpec(wb.shape, lambda i: (0, 0)),
                  pl.BlockSpec((1, d), lambda i: (0, 0)),
                  pl.BlockSpec((1, r, d), lambda i: (i // tps, 0, 0))],
        out_specs=pl.BlockSpec((tm, d), lambda i: (i, 0)),
        out_shape=jax.ShapeDtypeStruct((m, d), F32),
        compiler_params=_cp("arbitrary"), name="out_proj",
    )(x, a, b, wa, wb, g.reshape(1, d), gate)


def _ffn_body(x_ref, g1_ref, sh_ref, sc_ref, wa_ref, wb_ref, wd_ref, cw_ref, cb_ref, inj_ref, g2_ref, gate_ref,
              o_ref, aux_ref, h_sc, acc_sc, carry_sc, *, tps, tf, inject):
    i = pl.program_id(0)
    f = pl.program_id(1)
    tm = x_ref.shape[0]

    @pl.when(f == 0)
    def _():
        h_sc[...] = _ada_norm_bf16(x_ref, g1_ref, sh_ref, sc_ref)
        acc_sc[...] = jnp.zeros_like(acc_sc)

    h = h_sc[...]
    a = _dot(h, wa_ref[...])
    b = _dot(h, wb_ref[...])
    row = _iota((tm, tf), 0)
    if inject:
        a = jnp.where(row % SLOT < SLOT_LO, inj_ref[...], a)
        aux_ref[...] = a
        p1 = jnp.zeros((1, tf), F32)
        p2 = jnp.zeros((2, tf), F32)
    else:
        @pl.when(i % tps == 0)
        def _():
            carry_sc[f] = jnp.zeros((2, tf), F32)

        prev = carry_sc[f]
        carry_sc[f] = a[tm - 2:, :]
        aux_ref[0] = a[tm - 2:, :]
        p1 = prev[1:2, :]
        p2 = prev
    a1 = jnp.where(row == 0, p1, pltpu.roll(a, 1, 0))
    a2 = pltpu.roll(a, 2, 0)
    a2 = jnp.where(row == 0, p2[0:1, :], jnp.where(row == 1, p2[1:2, :], a2))
    cw = cw_ref[...]
    conv = cb_ref[...] + a2 * cw[0:1, :]
    conv = conv + a1 * cw[1:2, :]
    conv = conv + a * cw[2:3, :]
    y = jax.nn.gelu(conv, approximate=True) * b
    acc_sc[...] += _dot(y.astype(BF16), wd_ref[...])

    @pl.when(f == pl.num_programs(1) - 1)
    def _():
        yv = acc_sc[...]
        yn = yv * lax.rsqrt(jnp.mean(yv * yv, axis=-1, keepdims=True) + EPS) * g2_ref[...]
        o_ref[...] = x_ref[...] + gate_ref[0] * yn


def _ffn(x, g1, shift, scale, wa, wb, wd, cw, cb, inj, g2, gate, tm, tps, tf):
    m, d = x.shape
    ff = wa.shape[1]
    nf = ff // tf
    r = shift.shape[1]
    inject = inj is not None
    if inject:
        aux_shape, aux_spec = (m, ff), pl.BlockSpec((tm, tf), lambda i, f: (i, f))
        inj_spec = pl.BlockSpec((tm, tf), lambda i, f: (i, f))
    else:
        aux_shape, aux_spec = (m // tm, 2, ff), pl.BlockSpec((1, 2, tf), lambda i, f: (i, 0, f))
        inj = jnp.zeros((8, LANES), F32)
        inj_spec = pl.BlockSpec((8, LANES), lambda i, f: (0, 0))
    mod_spec = pl.BlockSpec((1, r, d), lambda i, f: (i // tps, 0, 0))
    return pl.pallas_call(
        functools.partial(_ffn_body, tps=tps, tf=tf, inject=inject),
        grid=(m // tm, nf),
        in_specs=[pl.BlockSpec((tm, d), lambda i, f: (i, 0)), pl.BlockSpec((1, d), lambda i, f: (0, 0)),
                  mod_spec, mod_spec,
                  pl.BlockSpec((d, tf), lambda i, f: (0, f)), pl.BlockSpec((d, tf), lambda i, f: (0, f)),
                  pl.BlockSpec((tf, d), lambda i, f: (f, 0)),
                  pl.BlockSpec((CONV_W, tf), lambda i, f: (0, f)), pl.BlockSpec((1, tf), lambda i, f: (0, f)),
                  inj_spec, pl.BlockSpec((1, d), lambda i, f: (0, 0)), mod_spec],
        out_specs=[pl.BlockSpec((tm, d), lambda i, f: (i, 0)), aux_spec],
        out_shape=[jax.ShapeDtypeStruct((m, d), F32), jax.ShapeDtypeStruct(aux_shape, F32)],
        scratch_shapes=[pltpu.VMEM((tm, d), BF16), pltpu.VMEM((tm, d), F32), pltpu.VMEM((nf, 2, tf), F32)],
        compiler_params=_cp("arbitrary", "arbitrary"), name="conv_ffn",
    )(x, g1.reshape(1, d), shift, scale, wa, wb, wd, cw, cb.reshape(1, ff), inj, g2.reshape(1, d), gate)


def _kth_largest(stats_fn, cmin0, cmax0, k):
    def flag(lo, hi):
        return jnp.max(jnp.where(lo < hi, 1.0, 0.0))

    def body(st):
        cmin, cmax, _ = st
        mid = cmin + 0.5 * (cmax - cmin)
        p = jnp.where(mid > cmin, jnp.where(mid <= cmax, mid, cmax), cmax)
        cnt, mn_ge, mx_lt = stats_fn(p)
        is_open = cmin < cmax
        take_lo = cnt >= k
        ncmin = jnp.where(is_open, jnp.where(take_lo, mn_ge, cmin), cmin)
        ncmax = jnp.where(is_open, jnp.where(take_lo, cmax, mx_lt), cmax)
        return ncmin, ncmax, flag(ncmin, ncmax)

    return lax.while_loop(lambda st: st[2] > 0.0, body, (cmin0, cmax0, flag(cmin0, cmax0)))[0]


def _lane_blocks(s):
    return [s[:, c * LANES:(c + 1) * LANES] for c in range(s.shape[1] // LANES)]


def _softmax_probs(blocks, m_prev, l_prev, exp=jnp.exp):
    mloc = functools.reduce(jnp.maximum, blocks)
    m_new = jnp.maximum(m_prev, jnp.max(mloc, axis=1, keepdims=True))
    alpha = exp(m_prev - m_new)
    ps = [exp(b - m_new) for b in blocks]
    l_new = alpha * l_prev + jnp.sum(functools.reduce(jnp.add, ps), axis=1, keepdims=True)
    p = (jnp.concatenate(ps, axis=1) if len(ps) > 1 else ps[0]).astype(BF16)
    return p, alpha, m_new, l_new


def _scale_acc(alpha, acc):
    dv = acc.shape[-1]
    if dv < LANES:
        return alpha[:, :dv] * acc
    return (alpha if dv == LANES else jnp.concatenate([alpha] * (dv // LANES), axis=1)) * acc


def _softmax_step(s, v_bf16, m_ref, l_ref, acc_ref, idx, values_transposed=False, exp=jnp.exp):
    p, alpha, m_new, l_new = _softmax_probs(_lane_blocks(s), m_ref[idx], l_ref[idx], exp)
    pv = _dot_nt(p, v_bf16) if values_transposed else _dot(p, v_bf16)
    acc_ref[idx] = _scale_acc(alpha, acc_ref[idx]) + pv
    m_ref[idx] = m_new
    l_ref[idx] = l_new


def _tie_bias(x, t, need, carry, tri):
    eq = x == t
    pref = _dot(jnp.where(eq, 1.0, 0.0).astype(BF16), tri) + carry
    bias = jnp.where(x > t, 0.0, jnp.where(eq, jnp.where(pref <= need, 0.0, NEG), NEG))
    return bias, pref[:, x.shape[1] - 1:]


def _fold8(x, op):
    parts = [x[i * 8:(i + 1) * 8] for i in range(x.shape[0] // 8)]
    while len(parts) > 1:
        parts = [op(parts[i], parts[i + 1]) for i in range(0, len(parts) - 1, 2)] + (parts[-1:] if len(parts) % 2 else [])
    return parts[0]


def _dsa_prompt_body(qa_ref, qi_ref, kiwq_ref, kiwk_ref, kv_ref, tril_ref, o_ref,
                     sct, sc, m_sc, l_sc, acc_sc, *, tq, ch, topk, light_passes):
    j = pl.program_id(1)
    q0 = j * tq
    nc = (q0 + tq + ch - 1) // ch
    qpos = q0 + _iota((1, tq), 1)
    kf = float(topk)
    inf = jnp.inf
    qi = qi_ref[0] * IDX_DIM ** -0.5
    qi4 = jnp.concatenate([qi[:, h * IDX_DIM:(h + 1) * IDX_DIM] for h in range(IDX_HEADS)], axis=0).astype(BF16)
    pick = jnp.where(_iota((8, LANES), 1) == _iota((8, LANES), 0) + IDX_DIM, 1.0, 0.0).astype(BF16)
    wt = sum(_dot_nt(pick, part) for part in _split3(kiwq_ref[0])) * IDX_HEADS ** -0.5

    def score_chunk(c, st):
        mx, mn = st
        c0 = pl.multiple_of(c * ch, ch)
        kc = kiwk_ref[0, pl.ds(c0, ch), :][:, :IDX_DIM].astype(BF16)
        lg = _dot_nt(kc, qi4)
        s = jnp.zeros((ch, tq), F32)
        for h in range(IDX_HEADS):
            s = s + jnp.maximum(lg[:, h * tq:(h + 1) * tq], 0.0) * wt[h:h + 1, :]
        s = jnp.where(s == 0.0, 0.0, s)
        valid = (c0 + _iota((ch, tq), 0)) <= qpos
        sct[c] = jnp.where(valid, s, -inf)
        mx = jnp.maximum(mx, _fold8(jnp.where(valid, s, -inf), jnp.maximum))
        mn = jnp.minimum(mn, _fold8(jnp.where(valid, s, inf), jnp.minimum))
        return mx, mn

    mx, mn = lax.fori_loop(0, nc, score_chunk, (jnp.full((8, tq), -inf, F32), jnp.full((8, tq), inf, F32)))
    active = qpos + 1 > topk
    row_max = jnp.where(active, jnp.max(mx, axis=0, keepdims=True), 0.0)
    row_min = jnp.where(active, jnp.min(mn, axis=0, keepdims=True), 0.0)

    def count_ge(p):
        def body(c, cnt):
            return cnt + _fold8(jnp.where(sct[c] >= p, 1.0, 0.0), jnp.add)
        return jnp.sum(lax.fori_loop(0, nc, body, jnp.zeros((8, tq), F32)), axis=0, keepdims=True)

    def stats(p):
        def body(c, st):
            cnt, mnge, mxlt = st
            x = sct[c]
            ge = x >= p
            return (cnt + _fold8(jnp.where(ge, 1.0, 0.0), jnp.add),
                    jnp.minimum(mnge, _fold8(jnp.where(ge, x, inf), jnp.minimum)),
                    jnp.maximum(mxlt, _fold8(jnp.where(ge, -inf, x), jnp.maximum)))
        init = (jnp.zeros((8, tq), F32), jnp.full((8, tq), inf, F32), jnp.full((8, tq), -inf, F32))
        cnt, mnge, mxlt = lax.fori_loop(0, nc, body, init)
        return (jnp.sum(cnt, axis=0, keepdims=True), jnp.min(mnge, axis=0, keepdims=True),
                jnp.max(mxlt, axis=0, keepdims=True))

    def light(i, st):
        lo, hi = st
        mid = lo + 0.5 * (hi - lo)
        take = count_ge(mid) >= kf
        return jnp.where(take, mid, lo), jnp.where(take, hi, mid)

    lo, hi = lax.fori_loop(0, light_passes, light, (row_min, row_max))
    _, cmin, _ = stats(lo)
    cnt_hi, mn_hi, mx_hi = stats(hi)
    at_top = cnt_hi >= kf
    cmin = jnp.where(at_top, mn_hi, cmin)
    cmax = jnp.where(at_top, row_max, mx_hi)
    t = jnp.where(active, _kth_largest(stats, cmin, cmax, kf), -inf)

    def count_sel(c, st):
        x = sct[c]
        return (st[0] + _fold8(jnp.where(x > t, 1.0, 0.0), jnp.add),
                st[1] + _fold8(jnp.where(x >= t, 1.0, 0.0), jnp.add))

    gt, ge = lax.fori_loop(0, nc, count_sel, (jnp.zeros((8, tq), F32), jnp.zeros((8, tq), F32)))
    need = kf - jnp.sum(gt, axis=0, keepdims=True)
    over = jnp.sum(ge, axis=0, keepdims=True) > kf
    has_ties = jnp.max(jnp.where(active, jnp.where(over, 1.0, 0.0), 0.0)) > 0.0
    eye = _eye_bf16(tq)

    def store_bias(c, sel_t):
        sel = _dot_nt(eye, sel_t.astype(BF16))
        sc[c] = jnp.where(sel > 0.5, 0.0, NEG)

    @pl.when(has_ties)
    def _():
        tril = tril_ref[...]

        def tie_chunk(c, carry):
            x = sct[c]
            eq = x == t
            pref = _dot(tril, jnp.where(eq, 1.0, 0.0).astype(BF16)) + carry
            store_bias(c, jnp.where(x > t, 1.0, jnp.where(eq, jnp.where(pref <= need, 1.0, 0.0), 0.0)))
            return pref[ch - 1:, :]

        lax.fori_loop(0, nc, tie_chunk, jnp.zeros((1, tq), F32))

    @pl.when(jnp.logical_not(has_ties))
    def _():
        def plain_chunk(c, carry):
            store_bias(c, jnp.where(sct[c] >= t, 1.0, 0.0))
            return carry

        lax.fori_loop(0, nc, plain_chunk, 0)

    cl = nc - 1
    sc[cl] = jnp.where((cl * ch + _iota((tq, ch), 1)) <= q0 + _iota((tq, ch), 0), sc[cl], NEG)

    qa = qa_ref[0].astype(F32) * (HEAD_DIM ** -0.5 * LOG2E)
    hpg = A_HEADS // A_KV_HEADS
    qh = [qa[:, h * HEAD_DIM:(h + 1) * HEAD_DIM].astype(BF16) for h in range(A_HEADS)]
    m_sc[...] = jnp.full(m_sc.shape, NEG, F32)
    l_sc[...] = jnp.zeros(l_sc.shape, F32)
    acc_sc[...] = jnp.zeros(acc_sc.shape, F32)

    def attend(c, carry):
        c0 = pl.multiple_of(c * ch, ch)
        bias = _lane_blocks(sc[c])
        kvc = kv_ref[0, pl.ds(c0, ch), :]
        for g in range(A_KV_HEADS):
            kc = kvc[:, g * HEAD_DIM:(g + 1) * HEAD_DIM].astype(BF16)
            vc = kvc[:, (A_KV_HEADS + g) * HEAD_DIM:(A_KV_HEADS + g + 1) * HEAD_DIM].astype(BF16)
            ps, alphas = [], []
            for h in range(hpg):
                rs = pl.ds(h * tq, tq)
                s = _dot_nt(qh[g * hpg + h], kc)
                blocks = [sb + bb for sb, bb in zip(_lane_blocks(s), bias)]
                p, alpha, m_new, l_new = _softmax_probs(blocks, m_sc[g, rs, :], l_sc[g, rs, :], exp=jnp.exp2)
                m_sc[g, rs, :] = m_new
                l_sc[g, rs, :] = l_new
                ps.append(p)
                alphas.append(alpha)
            acc_sc[g] = _scale_acc(jnp.concatenate(alphas, axis=0), acc_sc[g]) + _dot(jnp.concatenate(ps, axis=0), vc)
        return carry

    lax.fori_loop(0, nc, attend, 0)
    outs = []
    for g in range(A_KV_HEADS):
        og = acc_sc[g] / l_sc[g][:, :HEAD_DIM]
        outs += [og[h * tq:(h + 1) * tq] for h in range(hpg)]
    o_ref[0] = jnp.concatenate(outs, axis=1).astype(o_ref.dtype)


def _upper_tri_bf16(n):
    return jnp.asarray(np.triu(np.ones((n, n), np.float32)), BF16)


def _dsa_prompt(qa, qi, kiw, kv, tq=128, ch=512):
    b, s, _ = qa.shape
    ch = min(ch, s)
    topk = min(TOPK_MAX, s // 4)
    hpg = A_HEADS // A_KV_HEADS
    return pl.pallas_call(
        functools.partial(_dsa_prompt_body, tq=tq, ch=ch, topk=topk, light_passes=10),
        grid=(b, s // tq),
        in_specs=[pl.BlockSpec((1, tq, qa.shape[2]), lambda i, j: (i, j, 0)),
                  pl.BlockSpec((1, tq, qi.shape[2]), lambda i, j: (i, j, 0)),
                  pl.BlockSpec((1, tq, LANES), lambda i, j: (i, j, 0)),
                  pl.BlockSpec((1, s, LANES), lambda i, j: (i, 0, 0)),
                  pl.BlockSpec((1, s, kv.shape[2]), lambda i, j: (i, 0, 0)),
                  pl.BlockSpec((ch, ch), lambda i, j: (0, 0))],
        out_specs=pl.BlockSpec((1, tq, A_HEADS * HEAD_DIM), lambda i, j: (i, j, 0)),
        out_shape=jax.ShapeDtypeStruct((b, s, A_HEADS * HEAD_DIM), BF16),
        scratch_shapes=[pltpu.VMEM((s // ch, ch, tq), F32),
                        pltpu.VMEM((s // ch, tq, ch), F32),
                        pltpu.VMEM((A_KV_HEADS, hpg * tq, LANES), F32),
                        pltpu.VMEM((A_KV_HEADS, hpg * tq, LANES), F32),
                        pltpu.VMEM((A_KV_HEADS, hpg * tq, HEAD_DIM), F32)],
        compiler_params=_cp("arbitrary", "arbitrary"), name="dsa_prompt",
    )(qa, qi, kiw, kiw, kv, jnp.asarray(np.tril(np.ones((ch, ch), np.float32)), BF16))


def _head_layernorm(o, gain):
    mu = jnp.mean(o, axis=-1, keepdims=True)
    var = jnp.mean(jnp.square(o - mu), axis=-1, keepdims=True)
    return (o - mu) * lax.rsqrt(var + EPS) * gain


def _retention_body(q_ref, k_ref, v_ref, g_ref, s0_ref, dm_ref, qd_ref, kd_ref, cd_ref, gain_ref,
                    o_ref, sf_ref, s_sc):
    c = pl.program_id(1)

    @pl.when(c == 0)
    def _():
        s_sc[...] = s0_ref[0]

    q = q_ref[0].astype(BF16)
    k = k_ref[0]
    v = v_ref[0].astype(BF16)
    gt = g_ref[0]
    gain = gain_ref[...]
    outs = []
    for h in range(B_HEADS):
        sl = slice(h * HEAD_DIM, (h + 1) * HEAD_DIM)
        qh, kh, vh = q[:, sl], k[:, sl], v[:, sl]
        st = s_sc[h]
        att = _dot_nt(qh, kh.astype(BF16)) * dm_ref[h]
        o = _dot(att.astype(BF16), vh) + _dot(qh, st.astype(BF16)) * qd_ref[h]
        s_sc[h] = st * cd_ref[h] + _dot_tn((kh * kd_ref[h]).astype(BF16), vh)
        gh = gt[:, sl]
        outs.append(gh * jax.nn.sigmoid(gh) * _head_layernorm(o, gain[:, sl]))
    o_ref[0] = jnp.concatenate(outs, axis=1).astype(o_ref.dtype)

    @pl.when(c == pl.num_programs(1) - 1)
    def _():
        sf_ref[0] = s_sc[...]


def _retention(q, k, v, g, s0, gain, chunk, lo, hi, out_dtype):
    b, s, w = q.shape
    dm, qd, kd, cd = _retention_tables(chunk, lo, hi)
    row = lambda i, c: (i, c, 0)
    fix3 = lambda i, c: (0, 0, 0)
    st_spec = pl.BlockSpec((1, B_HEADS, HEAD_DIM, HEAD_DIM), lambda i, c: (i, 0, 0, 0))
    return pl.pallas_call(
        _retention_body, grid=(b, s // chunk),
        in_specs=[pl.BlockSpec((1, chunk, w), row)] * 4 + [st_spec] +
                 [pl.BlockSpec(t.shape, fix3) for t in (dm, qd, kd, cd)] + [pl.BlockSpec((1, w), lambda i, c: (0, 0))],
        out_specs=[pl.BlockSpec((1, chunk, w), row), st_spec],
        out_shape=[jax.ShapeDtypeStruct((b, s, w), out_dtype),
                   jax.ShapeDtypeStruct((b, B_HEADS, HEAD_DIM, HEAD_DIM), F32)],
        scratch_shapes=[pltpu.VMEM((B_HEADS, HEAD_DIM, HEAD_DIM), F32)],
        compiler_params=_cp("arbitrary", "arbitrary"), name="retention",
    )(q, k, v, g, s0, dm, qd, kd, cd, gain.reshape(1, w))


def _mlstm_body(q_ref, k_ref, v_ref, og_ref, gt_ref, c0_ref, n0_ref, m0_ref, gain_ref,
                o_ref, cf_ref, nf_ref, mf_ref, c_sc, n_sc, m_sc, *, lo, hi):
    c = pl.program_id(1)
    ck = q_ref.shape[1]
    nh = C_HEADS

    @pl.when(c == 0)
    def _():
        c_sc[...] = c0_ref[0]
        n_sc[...] = n0_ref[0]
        m0 = m0_ref[0]
        for h in range(nh):
            m_sc[h] = jnp.broadcast_to(m0[:, h:h + 1], (1, LANES))

    q = q_ref[0].astype(BF16)
    k = k_ref[0]
    v = v_ref[0].astype(BF16)
    og = og_ref[0]
    gates = gt_ref[0]
    gain = gain_ref[...]
    r1 = _iota((ck, 1), 0)
    tok = (r1 >= lo) & (r1 < hi)
    log_sig = jnp.minimum(gates, 0.0) - jnp.log1p(jnp.exp(-jnp.abs(gates)))
    lf = jnp.where(tok, log_sig, 0.0)
    ii = jnp.where(tok, gates, NEG)
    tril = jnp.where(_iota((ck, ck), 0) >= _iota((ck, ck), 1), 1.0, 0.0).astype(BF16)
    fcum = sum(_dot(tril, part) for part in _split3(lf))
    lane = _iota((ck, LANES), 1)
    z = jnp.where(lane < nh, ii, fcum)
    sel = jnp.where(_iota((2 * nh, LANES), 0) == _iota((2 * nh, LANES), 1), 1.0, 0.0).astype(BF16)
    zt = sum(_dot_nt(sel, part) for part in _split3(z))
    causal = _iota((ck, ck), 0) >= _iota((ck, ck), 1)
    outs = []
    for h in range(nh):
        sl = slice(h * HEAD_DIM, (h + 1) * HEAD_DIM)
        qh, kh, vh = q[:, sl], k[:, sl], v[:, sl]
        i_col, f_col = z[:, h:h + 1], z[:, nh + h:nh + h + 1]
        a_col = i_col - f_col
        a_row = zt[h:h + 1, :] - zt[nh + h:nh + h + 1, :]
        m_prev = m_sc[h][:, :1]
        cmax = jnp.max(jnp.where(causal, a_row, -jnp.inf), axis=1, keepdims=True)
        m_t = f_col + jnp.maximum(m_prev, cmax)
        dmat = jnp.exp(jnp.where(causal, f_col + a_row - m_t, -jnp.inf))
        inter = jnp.exp(f_col + m_prev - m_t)
        cst = c_sc[h]
        nst = n_sc[h]
        s = _dot_nt(qh, kh.astype(BF16)) * dmat
        num = _dot(s.astype(BF16), vh) + inter * _dot(qh, cst.astype(BF16))
        den = jnp.sum(s, axis=1, keepdims=True) + inter * jnp.sum(qh.astype(F32) * nst, axis=1, keepdims=True)
        hh = num / jnp.maximum(jnp.abs(den), jnp.exp(-m_t))
        f_last = f_col[ck - 1:, :]
        m_end = m_t[ck - 1:, :]
        wgt = jnp.exp(f_last + a_col - m_end)
        dec = jnp.exp(f_last + m_prev - m_end)
        kw = kh * wgt
        c_sc[h] = dec * cst + _dot_tn(kw.astype(BF16), vh)
        n_sc[h] = dec * nst + jnp.sum(kw, axis=0, keepdims=True)
        m_sc[h] = jnp.broadcast_to(m_end, (1, LANES))
        outs.append(_head_layernorm(jax.nn.sigmoid(og[:, sl]) * hh, gain[:, sl]))
    o_ref[0] = jnp.concatenate(outs, axis=1).astype(o_ref.dtype)

    @pl.when(c == pl.num_programs(1) - 1)
    def _():
        cf_ref[0] = c_sc[...]
        nf_ref[0] = n_sc[...]
        mf_ref[0] = jnp.concatenate([m_sc[h][:, :1] for h in range(nh)], axis=1)


def _mlstm(q, k, v, og, gates, c0, n0, m0, gain, chunk, lo, hi, out_dtype):
    b, s, w = q.shape
    nh = C_HEADS
    row = lambda i, c: (i, c, 0)
    c_spec = pl.BlockSpec((1, nh, HEAD_DIM, HEAD_DIM), lambda i, c: (i, 0, 0, 0))
    n_spec = pl.BlockSpec((1, nh, 1, HEAD_DIM), lambda i, c: (i, 0, 0, 0))
    m_spec = pl.BlockSpec((1, 1, nh), lambda i, c: (i, 0, 0))
    o, cf, nf, mf = pl.pallas_call(
        functools.partial(_mlstm_body, lo=lo, hi=hi), grid=(b, s // chunk),
        in_specs=[pl.BlockSpec((1, chunk, w), row)] * 4 + [pl.BlockSpec((1, chunk, LANES), row), c_spec, n_spec, m_spec,
                                                          pl.BlockSpec((1, w), lambda i, c: (0, 0))],
        out_specs=[pl.BlockSpec((1, chunk, w), row), c_spec, n_spec, m_spec],
        out_shape=[jax.ShapeDtypeStruct((b, s, w), out_dtype),
                   jax.ShapeDtypeStruct((b, nh, HEAD_DIM, HEAD_DIM), F32),
                   jax.ShapeDtypeStruct((b, nh, 1, HEAD_DIM), F32),
                   jax.ShapeDtypeStruct((b, 1, nh), F32)],
        scratch_shapes=[pltpu.VMEM((nh, HEAD_DIM, HEAD_DIM), F32), pltpu.VMEM((nh, 1, HEAD_DIM), F32),
                        pltpu.VMEM((nh, 1, LANES), F32)],
        compiler_params=_cp("arbitrary", "arbitrary"), name="mlstm",
    )(q, k, v, og, gates, c0, n0.reshape(b, nh, 1, HEAD_DIM), m0.reshape(b, 1, nh), gain.reshape(1, w))
    return o, cf, nf.reshape(b, nh, HEAD_DIM), mf.reshape(b, nh)


def _diff_lambda(lam_ref, lam_init):
    lp = lam_ref[...]
    s01 = jnp.sum(lp[0:1] * lp[1:2], axis=1, keepdims=True)
    s23 = jnp.sum(lp[2:3] * lp[3:4], axis=1, keepdims=True)
    return jnp.exp(s01) - jnp.exp(s23) + lam_init


def _diff_finish(o0, l0, o1, l1, lam, subln, lam_init):
    od = o0 / l0 - lam * (o1 / l1)
    return od * lax.rsqrt(jnp.mean(od * od, axis=-1, keepdims=True) + EPS) * subln * (1.0 - lam_init)


def _diff_prompt_body(q_ref, k_ref, v_ref, lam_ref, sub_ref, o_ref, m_sc, l_sc, acc_sc, *, lam_init):
    i = pl.program_id(1)
    j = pl.program_id(2)
    t = q_ref.shape[1]

    @pl.when(j == 0)
    def _():
        m_sc[...] = jnp.full(m_sc.shape, NEG, F32)
        l_sc[...] = jnp.zeros(l_sc.shape, F32)
        acc_sc[...] = jnp.zeros(acc_sc.shape, F32)

    def step(diag):
        q = (q_ref[0].astype(F32) * (HEAD_DIM ** -0.5 * LOG2E)).astype(BF16)
        k = k_ref[0].astype(BF16)
        v = v_ref[0].astype(BF16)
        keep = (_iota((t, t), 0) >= _iota((t, t), 1)) if diag else None
        for idx in range(2 * D_HEADS):
            h = idx % D_HEADS
            sl = slice(idx * HEAD_DIM, (idx + 1) * HEAD_DIM)
            s = _dot_nt(q[:, sl], k[:, sl])
            if diag:
                s = jnp.where(keep, s, NEG)
            _softmax_step(s, v[:, h * D_VDIM:(h + 1) * D_VDIM], m_sc, l_sc, acc_sc, idx, exp=jnp.exp2)

    @pl.when(j < i)
    def _():
        step(False)

    @pl.when(j == i)
    def _():
        step(True)
        lam = _diff_lambda(lam_ref, lam_init)
        outs = [_diff_finish(acc_sc[h], l_sc[h], acc_sc[D_HEADS + h], l_sc[D_HEADS + h], lam, sub_ref[...], lam_init)
                for h in range(D_HEADS)]
        o_ref[0] = jnp.concatenate(outs, axis=1).astype(o_ref.dtype)


def _diff_prompt(q, k, v, lam_par, subln, lam_init, t=512):
    b, s, w = q.shape
    t = min(t, s)
    n = s // t
    return pl.pallas_call(
        functools.partial(_diff_prompt_body, lam_init=lam_init), grid=(b, n, n),
        in_specs=[pl.BlockSpec((1, t, w), lambda bi, i, j: (bi, i, 0)),
                  pl.BlockSpec((1, t, w), lambda bi, i, j: (bi, jnp.minimum(i, j), 0)),
                  pl.BlockSpec((1, t, w), lambda bi, i, j: (bi, jnp.minimum(i, j), 0)),
                  pl.BlockSpec(lam_par.shape, lambda bi, i, j: (0, 0)),
                  pl.BlockSpec((1, D_VDIM), lambda bi, i, j: (0, 0))],
        out_specs=pl.BlockSpec((1, t, w), lambda bi, i, j: (bi, i, 0)),
        out_shape=jax.ShapeDtypeStruct((b, s, w), BF16),
        scratch_shapes=[pltpu.VMEM((2 * D_HEADS, t, LANES), F32), pltpu.VMEM((2 * D_HEADS, t, LANES), F32),
                        pltpu.VMEM((2 * D_HEADS, t, D_VDIM), F32)],
        compiler_params=_cp("arbitrary", "arbitrary", "arbitrary"), name="diff_prompt",
    )(q, k, v, lam_par, subln.reshape(1, D_VDIM))


def _page_specs(rows, layer, group, n_pages):
    def spec(g):
        def imap(b, j, pt):
            return (layer, pt[b, jnp.minimum(j * group + g, n_pages - 1)], 0, 0)
        return pl.BlockSpec((None, None, rows, PAGE_SIZE), imap)
    return [spec(g) for g in range(group)]


def _cat_lanes(refs):
    return jnp.concatenate([r[...] for r in refs], axis=1)


def _new_key_valid(nq, nk):
    rq = _iota((nq, nk), 0) % SLOT
    rk = _iota((nq, nk), 1)
    return (rk >= SLOT_LO) & (rk < SLOT_LO + (SLOT - 2 * SLOT_LO)) & (rk <= rq)


def _idx_scores(qi4, w, keys_t_bf16):
    lg = _dot(qi4, keys_t_bf16)
    s = jnp.zeros((SLOT, keys_t_bf16.shape[1]), F32)
    for h in range(IDX_HEADS):
        s = s + jnp.maximum(lg[h * SLOT:(h + 1) * SLOT], 0.0) * w[:, h:h + 1]
    return jnp.where(s == 0.0, 0.0, s)


def _dsa_sample_scores_body(pt_ref, qi_ref, kiw_ref, knew_ref, *rest, group):
    pages, o_ref = rest[:group], rest[group]
    j = pl.program_id(1)
    last = pl.num_programs(1) - 1
    qi = qi_ref[0] * IDX_DIM ** -0.5
    qi4 = jnp.concatenate([qi[:, h * IDX_DIM:(h + 1) * IDX_DIM] for h in range(IDX_HEADS)], axis=0).astype(BF16)
    w = kiw_ref[0][:, IDX_DIM:IDX_DIM + IDX_HEADS] * IDX_HEADS ** -0.5

    @pl.when(j < last)
    def _():
        o_ref[0, 0] = _idx_scores(qi4, w, _cat_lanes(pages).astype(BF16))

    @pl.when(j == last)
    def _():
        s = _idx_scores(qi4, w, knew_ref[0].astype(BF16))
        s = jnp.where(_new_key_valid(SLOT, PAGE_SIZE), s, -jnp.inf)
        pad = jnp.full((SLOT, (group - 1) * PAGE_SIZE), -jnp.inf, F32)
        o_ref[0, 0] = jnp.concatenate([s, pad], axis=1) if group > 1 else s


def _dsa_sample_scores(page_table, qi, kiw, knew_t, cache_idx_t, layer, group):
    db, n_pages = page_table.shape
    nch = n_pages // group + 1
    gw = group * PAGE_SIZE
    grid_spec = pltpu.PrefetchScalarGridSpec(
        num_scalar_prefetch=1, grid=(db, nch),
        in_specs=[pl.BlockSpec((1, SLOT, qi.shape[2]), lambda b, j, pt: (b, 0, 0)),
                  pl.BlockSpec((1, SLOT, LANES), lambda b, j, pt: (b, 0, 0)),
                  pl.BlockSpec((1, IDX_DIM, PAGE_SIZE), lambda b, j, pt: (b, 0, 0))]
                 + _page_specs(IDX_DIM, layer, group, n_pages),
        out_specs=pl.BlockSpec((1, 1, SLOT, gw), lambda b, j, pt: (b, j, 0, 0)))
    return pl.pallas_call(
        functools.partial(_dsa_sample_scores_body, group=group), grid_spec=grid_spec,
        out_shape=jax.ShapeDtypeStruct((db, nch, SLOT, gw), F32),
        compiler_params=_cp("arbitrary", "arbitrary"), name="dsa_sample_scores",
    )(page_table, qi, kiw, knew_t, *([cache_idx_t] * group))


def _dsa_sample_attend_body(pt_ref, qa_ref, sc_ref, kvnew_ref, tri_ref, *rest, group, topk):
    pages, o_ref = rest[:group], rest[group]
    t_sc, need_sc, carry_sc, m_sc, l_sc, acc_sc = rest[group + 1:]
    j = pl.program_id(1)
    last = pl.num_programs(1) - 1
    hpg = A_HEADS // A_KV_HEADS

    @pl.when(j == 0)
    def _():
        x = sc_ref[0]

        def stats(p):
            ge = x >= p[None]
            cnt = jnp.sum(jnp.sum(jnp.where(ge, 1.0, 0.0), axis=0), axis=1, keepdims=True)
            mnge = jnp.min(jnp.min(jnp.where(ge, x, jnp.inf), axis=0), axis=1, keepdims=True)
            mxlt = jnp.max(jnp.max(jnp.where(ge, -jnp.inf, x), axis=0), axis=1, keepdims=True)
            return cnt, mnge, mxlt

        finite = x > -jnp.inf
        cmax0 = jnp.max(jnp.max(x, axis=0), axis=1, keepdims=True)
        cmin0 = jnp.min(jnp.min(jnp.where(finite, x, jnp.inf), axis=0), axis=1, keepdims=True)
        t = _kth_largest(stats, cmin0, cmax0, float(topk))
        n_gt = jnp.sum(jnp.sum(jnp.where(x > t[None], 1.0, 0.0), axis=0), axis=1, keepdims=True)
        t_sc[...] = jnp.broadcast_to(t, t_sc.shape)
        need_sc[...] = jnp.broadcast_to(float(topk) - n_gt, need_sc.shape)
        carry_sc[...] = jnp.zeros(carry_sc.shape, F32)
        m_sc[...] = jnp.full(m_sc.shape, NEG, F32)
        l_sc[...] = jnp.zeros(l_sc.shape, F32)
        acc_sc[...] = jnp.zeros(acc_sc.shape, F32)

    qa = qa_ref[0] * HEAD_DIM ** -0.5
    qg = [jnp.concatenate([qa[:, (g * hpg + h) * HEAD_DIM:(g * hpg + h + 1) * HEAD_DIM] for h in range(hpg)],
                          axis=0).astype(BF16) for g in range(A_KV_HEADS)]
    t = t_sc[...][:, :1]
    need = need_sc[...][:, :1]
    tri = tri_ref[...]

    def attend(x, kv_t):
        carry = carry_sc[...][:, :1]
        biases = []
        for xb in _lane_blocks(x):
            bias, carry = _tie_bias(xb, t, need, carry, tri)
            biases.append(bias)
        carry_sc[...] = jnp.broadcast_to(carry, carry_sc.shape)
        bias = jnp.concatenate(biases, axis=1) if len(biases) > 1 else biases[0]
        bias4 = jnp.concatenate([bias] * hpg, axis=0)
        for g in range(A_KV_HEADS):
            kc = kv_t[g * HEAD_DIM:(g + 1) * HEAD_DIM].astype(BF16)
            vc = kv_t[(A_KV_HEADS + g) * HEAD_DIM:(A_KV_HEADS + g + 1) * HEAD_DIM].astype(BF16)
            _softmax_step(_dot(qg[g], kc) + bias4, vc, m_sc, l_sc, acc_sc, g, values_transposed=True)

    @pl.when(j < last)
    def _():
        attend(sc_ref[0, j], _cat_lanes(pages))

    @pl.when(j == last)
    def _():
        attend(sc_ref[0, j][:, :PAGE_SIZE], kvnew_ref[0])
        outs = []
        for g in range(A_KV_HEADS):
            og = acc_sc[g] / l_sc[g][:, :HEAD_DIM]
            outs += [og[h * SLOT:(h + 1) * SLOT] for h in range(hpg)]
        o_ref[0] = jnp.concatenate(outs, axis=1)


def _dsa_sample_attend(page_table, qa, scores, kvnew, cache_kv, layer, group, topk):
    db, n_pages = page_table.shape
    nch = n_pages // group + 1
    gw = group * PAGE_SIZE
    hpg = A_HEADS // A_KV_HEADS
    kvw = 2 * A_KV_HEADS * HEAD_DIM
    grid_spec = pltpu.PrefetchScalarGridSpec(
        num_scalar_prefetch=1, grid=(db, nch),
        in_specs=[pl.BlockSpec((1, SLOT, qa.shape[2]), lambda b, j, pt: (b, 0, 0)),
                  pl.BlockSpec((1, nch, SLOT, gw), lambda b, j, pt: (b, 0, 0, 0)),
                  pl.BlockSpec((1, kvw, PAGE_SIZE), lambda b, j, pt: (b, 0, 0)),
                  pl.BlockSpec((PAGE_SIZE, PAGE_SIZE), lambda b, j, pt: (0, 0))]
                 + _page_specs(kvw, layer, group, n_pages),
        out_specs=pl.BlockSpec((1, SLOT, A_HEADS * HEAD_DIM), lambda b, j, pt: (b, 0, 0)),
        scratch_shapes=[pltpu.VMEM((SLOT, LANES), F32), pltpu.VMEM((SLOT, LANES), F32), pltpu.VMEM((SLOT, LANES), F32),
                        pltpu.VMEM((A_KV_HEADS, hpg * SLOT, LANES), F32),
                        pltpu.VMEM((A_KV_HEADS, hpg * SLOT, LANES), F32),
                        pltpu.VMEM((A_KV_HEADS, hpg * SLOT, HEAD_DIM), F32)])
    return pl.pallas_call(
        functools.partial(_dsa_sample_attend_body, group=group, topk=topk), grid_spec=grid_spec,
        out_shape=jax.ShapeDtypeStruct((db, SLOT, A_HEADS * HEAD_DIM), F32),
        compiler_params=_cp("arbitrary", "arbitrary"), name="dsa_sample_attend",
    )(page_table, qa, scores, kvnew, _upper_tri_bf16(PAGE_SIZE), *([cache_kv] * group))


def _diff_sample_body(pt_ref, q_ref, knew_ref, vnew_ref, lam_ref, sub_ref, *rest, group, lam_init):
    kpages, vpages, o_ref = rest[:group], rest[group:2 * group], rest[2 * group]
    m_sc, l_sc, acc_sc = rest[2 * group + 1:]
    j = pl.program_id(1)
    last = pl.num_programs(1) - 1
    nmap = 2 * D_HEADS
    w = nmap * HEAD_DIM

    @pl.when(j == 0)
    def _():
        m_sc[...] = jnp.full(m_sc.shape, NEG, F32)
        l_sc[...] = jnp.zeros(l_sc.shape, F32)
        acc_sc[...] = jnp.zeros(acc_sc.shape, F32)

    q = q_ref[0] * HEAD_DIM ** -0.5
    lane_map = _iota((SLOT, w), 1) // HEAD_DIM
    qb = jnp.concatenate([jnp.where(lane_map == m * D_HEADS + h, q, 0.0)
                          for h in range(D_HEADS) for m in range(2)], axis=0).astype(BF16)
    rows_h = 2 * SLOT

    def step(k_t, v_refs, mask):
        s = _dot(qb, k_t.astype(BF16))
        if mask is not None:
            s = jnp.where(mask, s, NEG)
        p, alpha, m_new, l_new = _softmax_probs(_lane_blocks(s), m_sc[...], l_sc[...])
        m_sc[...] = m_new
        l_sc[...] = l_new
        for h in range(D_HEADS):
            vh = jnp.concatenate([r[pl.ds(h, PAGE_SIZE, stride=D_HEADS), :] for r in v_refs], axis=0).astype(BF16)
            rs = slice(h * rows_h, (h + 1) * rows_h)
            acc_sc[h] = alpha[rs] * acc_sc[h] + _dot(p[rs], vh)

    @pl.when(j < last)
    def _():
        step(_cat_lanes(kpages), vpages, None)

    @pl.when(j == last)
    def _():
        step(knew_ref[0], [vnew_ref.at[0]], _new_key_valid(nmap * SLOT, PAGE_SIZE))
        lam = _diff_lambda(lam_ref, lam_init)
        l = l_sc[...]
        outs = []
        for h in range(D_HEADS):
            acc = acc_sc[h]
            lh = l[h * rows_h:(h + 1) * rows_h]
            outs.append(_diff_finish(acc[:SLOT], lh[:SLOT], acc[SLOT:], lh[SLOT:], lam, sub_ref[...], lam_init))
        o_ref[0] = jnp.concatenate(outs, axis=1)


def _diff_sample(page_table, q, knew_t, vnew, lam_par, subln, cache_k_t, cache_v, layer, group, lam_init):
    db, n_pages = page_table.shape
    nch = n_pages // group + 1
    nmap = 2 * D_HEADS
    w = q.shape[2]
    vw = D_HEADS * D_VDIM
    vrows = PAGE_SIZE * D_HEADS
    grid_spec = pltpu.PrefetchScalarGridSpec(
        num_scalar_prefetch=1, grid=(db, nch),
        in_specs=[pl.BlockSpec((1, SLOT, w), lambda b, j, pt: (b, 0, 0)),
                  pl.BlockSpec((1, w, PAGE_SIZE), lambda b, j, pt: (b, 0, 0)),
                  pl.BlockSpec((1, vrows, D_VDIM), lambda b, j, pt: (b, 0, 0)),
                  pl.BlockSpec(lam_par.shape, lambda b, j, pt: (0, 0)),
                  pl.BlockSpec((1, D_VDIM), lambda b, j, pt: (0, 0))]
                 + _page_specs(w, layer, group, n_pages) + _page_specs(vrows, layer, group, n_pages),
        out_specs=pl.BlockSpec((1, SLOT, vw), lambda b, j, pt: (b, 0, 0)),
        scratch_shapes=[pltpu.VMEM((nmap * SLOT, LANES), F32), pltpu.VMEM((nmap * SLOT, LANES), F32),
                        pltpu.VMEM((D_HEADS, 2 * SLOT, D_VDIM), F32)])
    return pl.pallas_call(
        functools.partial(_diff_sample_body, group=group, lam_init=lam_init), grid_spec=grid_spec,
        out_shape=jax.ShapeDtypeStruct((db, SLOT, vw), F32),
        compiler_params=_cp("arbitrary", "arbitrary"), name="diff_sample",
    )(page_table, q, knew_t, vnew, lam_par, subln.reshape(1, D_VDIM), *([cache_k_t] * group), *([cache_v] * group))


def _split_cols(w, sizes):
    out, o = [], 0
    for s in sizes:
        out.append(w[:, o:o + s])
        o += s
    return out


def _pad_cols(w, n):
    return jnp.pad(w, ((0, 0), (0, n - w.shape[1])))


def _pad_rows(a, n):
    return jnp.pad(a, ((0, 0), (0, n - a.shape[1]), (0, 0)))


def _keys_t(a):
    return jnp.pad(jnp.swapaxes(a, 1, 2), ((0, 0), (0, 0), (0, PAGE_SIZE - a.shape[1])))


def _pos_minor(cache):
    nd = cache.ndim
    t = jnp.transpose(cache, (0, 1) + tuple(range(3, nd)) + (2,))
    return t.reshape(cache.shape[0], cache.shape[1], -1, cache.shape[2])


def _ab_weights(w_in):
    hd = HEAD_DIM
    aq, ak, av, iq, ik, iw, bq, bk, bv, bg = _split_cols(
        w_in, (A_HEADS * hd, A_KV_HEADS * hd, A_KV_HEADS * hd, IDX_HEADS * IDX_DIM, IDX_DIM, IDX_HEADS,
               B_HEADS * hd, B_HEADS * hd, B_HEADS * hd, B_HEADS * hd))
    segs = [aq, jnp.concatenate([ak, av], axis=1), iq, _pad_cols(jnp.concatenate([ik, iw], axis=1), LANES), bq, bk, bv, bg]
    return [s.astype(BF16) for s in segs]


def _cd_weights(w_in):
    hd = HEAD_DIM
    cq, ck, cv, ci, cf, co, dq, dk, dv = _split_cols(
        w_in, (C_HEADS * hd, C_HEADS * hd, C_HEADS * hd, C_HEADS, C_HEADS, C_HEADS * hd,
               2 * D_HEADS * hd, 2 * D_HEADS * hd, D_HEADS * D_VDIM))
    segs = [cq, ck, cv, co, _pad_cols(jnp.concatenate([ci, cf], axis=1), LANES), dq, dk, dv]
    return [s.astype(BF16) for s in segs]


def kernel(x_prompt, x_sample, c_prompt, c_sample, page_table, cache_a_kv, cache_a_idx, cache_d_k, cache_d_v, state_ret, state_mlstm_c, state_mlstm_n, state_mlstm_m, state_ffn_conv, norm_pre_mix, norm_post_mix, norm_pre_ffn, norm_post_ffn, w_ada, b_ada, w_ab_in, w_ab_out, ret_gain, w_cd_in, w_cd_out, c_gate_bias, c_gain, d_lambda, d_subln, ffn_w_up, ffn_conv_w, ffn_conv_b, ffn_w_down):
    b, s, d = x_prompt.shape
    db, t, _ = x_sample.shape
    depth = w_ada.shape[0]
    n_pages = page_table.shape[1]
    past = n_pages * PAGE_SIZE
    d_ff = ffn_w_down.shape[1]
    assert t == SLOT - 2 * SLOT_LO and past >= TOPK_MAX and s % 1024 == 0
    hi = SLOT_LO + t
    ms = db * SLOT
    tm_in, tm_ffn, tf = 512, 1024, 256
    group = 16
    topk_s = min(TOPK_MAX, (past + t) // 4)

    n_c = b + db
    c_all = jnp.pad(jnp.concatenate([c_prompt, c_sample], axis=0), ((0, -n_c % 8), (0, 0)))
    mods = _modulation(c_all, w_ada, b_ada)

    xp = x_prompt.reshape(b * s, d)
    xs = jnp.pad(x_sample, ((0, 0), (SLOT_LO, SLOT - hi), (0, 0))).reshape(ms, d)

    pos_s = past + (np.arange(ms) % SLOT) - SLOT_LO
    tp_p, tf_p = _rope_tables(np.arange(s), HEAD_DIM // 4, ROPE_THETA), _rope_tables(np.arange(s), HEAD_DIM, RET_THETA)
    tp_s, tf_s = _rope_tables(pos_s, HEAD_DIM // 4, ROPE_THETA), _rope_tables(pos_s, HEAD_DIM, RET_THETA)

    cache_a_kv = _pos_minor(cache_a_kv)
    cache_a_idx = _pos_minor(cache_a_idx)
    cache_d_k = _pos_minor(cache_d_k)
    cache_d_v = cache_d_v.reshape(*cache_d_v.shape[:2], -1, D_VDIM)

    hd = HEAD_DIM
    w512 = 8 * hd
    ab_defs = lambda dt: [(w512, dt), (4 * hd, F32), (4 * hd, dt), (LANES, F32), (w512, dt), (w512, F32), (w512, dt), (w512, F32)]
    cd_defs = lambda dt: [(w512, dt), (w512, F32), (w512, dt), (w512, F32), (LANES, F32), (w512, dt), (w512, F32), (w512, F32)]
    zeros = lambda *shape: jnp.zeros(shape, F32)

    outs = {k: [] for k in ("pa_kv", "pa_idx", "pd_k", "pd_v", "p_ret", "p_mc", "p_mn", "p_mm", "p_cv",
                            "sa_kv", "sa_idx", "sd_k", "sd_v", "s_ret", "s_mc", "s_mn", "s_mm", "s_cv")}
    for l in range(depth):
        p = l // 2
        m6 = mods[l, :n_c].reshape(n_c, 6, d)
        mp = [m6[:b, i][:, None, :] for i in range(6)]
        msm = [jnp.repeat(m6[b:, i], SLOT, axis=0)[None] for i in range(6)]
        if l % 2 == 0:
            wts = _ab_weights(w_ab_in[p])
            wo = w_ab_out[p].astype(BF16)
            qa, kv, qi, kiw, bq, bk, bv, bg = _in_proj(_ab_in_body, "ab_in", xp, norm_pre_mix[l], mp[0], mp[1],
                                                       [tp_p, tf_p], [], wts, ab_defs(BF16), tm_in, s // tm_in)
            r3 = lambda a: a.reshape(b, s, a.shape[-1])
            a_out = _dsa_prompt(r3(qa), r3(qi), r3(kiw), r3(kv))
            b_out, st = _retention(r3(bq), r3(bk), r3(bv), r3(bg), zeros(b, B_HEADS, hd, hd), ret_gain[p],
                                   CHUNK, 0, CHUNK, BF16)
            outs["pa_kv"].append(kv.reshape(b, s, 2, A_KV_HEADS, hd))
            outs["pa_idx"].append(r3(kiw)[:, :, :IDX_DIM])
            outs["p_ret"].append(st)
            xp = _out_proj(xp, a_out.reshape(b * s, -1), b_out.reshape(b * s, -1), wo[:w512], wo[w512:],
                           norm_post_mix[l], mp[2], tm_in, s // tm_in)
            qa, kv, qi, kiw, bq, bk, bv, bg = _in_proj(_ab_in_body, "ab_in", xs, norm_pre_mix[l], msm[0], msm[1],
                                                       [tp_s, tf_s], [], wts, ab_defs(F32), ms, 1)
            r3 = lambda a: a.reshape(db, SLOT, a.shape[-1])
            scores = _dsa_sample_scores(page_table, r3(qi), r3(kiw), _keys_t(r3(kiw)[:, :, :IDX_DIM]), cache_a_idx, p, group)
            a_out = _dsa_sample_attend(page_table, r3(qa), scores, _keys_t(r3(kv)), cache_a_kv, p, group, topk_s)
            b_out, st = _retention(r3(bq), r3(bk), r3(bv), r3(bg), state_ret[p], ret_gain[p], SLOT, SLOT_LO, hi, F32)
            outs["sa_kv"].append(r3(kv)[:, SLOT_LO:hi].reshape(db, t, 2, A_KV_HEADS, hd))
            outs["sa_idx"].append(r3(kiw)[:, SLOT_LO:hi, :IDX_DIM])
            outs["s_ret"].append(st)
            xs = _out_proj(xs, a_out.reshape(ms, -1), b_out.reshape(ms, -1), wo[:w512], wo[w512:],
                           norm_post_mix[l], msm[2], ms, 1)
        else:
            lam_init = 0.8 - 0.6 * math.exp(-0.3 * l)
            wts = _cd_weights(w_cd_in[p])
            wo = w_cd_out[p].astype(BF16)
            gbias = jnp.pad(c_gate_bias[p], (0, LANES - 2 * C_HEADS)).reshape(1, LANES)
            cq, ck, cv, co, gt, dq, dk, dv = _in_proj(_cd_in_body, "cd_in", xp, norm_pre_mix[l], mp[0], mp[1],
                                                      [tp_p], [gbias], wts, cd_defs(BF16), tm_in, s // tm_in)
            r3 = lambda a: a.reshape(b, s, a.shape[-1])
            c_out, mc, mn, mm = _mlstm(r3(cq), r3(ck), r3(cv), r3(co), r3(gt), zeros(b, C_HEADS, hd, hd),
                                       zeros(b, C_HEADS, hd), zeros(b, C_HEADS), c_gain[p], CHUNK, 0, CHUNK, BF16)
            d_out = _diff_prompt(r3(dq), r3(dk), r3(dv), d_lambda[p], d_subln[p], lam_init)
            outs["pd_k"].append(dk.reshape(b, s, 2, D_HEADS, hd))
            outs["pd_v"].append(dv.reshape(b, s, D_HEADS, D_VDIM))
            outs["p_mc"].append(mc); outs["p_mn"].append(mn); outs["p_mm"].append(mm)
            xp = _out_proj(xp, c_out.reshape(b * s, -1), d_out.reshape(b * s, -1), wo[:w512], wo[w512:],
                           norm_post_mix[l], mp[2], tm_in, s // tm_in)
            cq, ck, cv, co, gt, dq, dk, dv = _in_proj(_cd_in_body, "cd_in", xs, norm_pre_mix[l], msm[0], msm[1],
                                                      [tp_s], [gbias], wts, cd_defs(F32), ms, 1)
            r3 = lambda a: a.reshape(db, SLOT, a.shape[-1])
            c_out, mc, mn, mm = _mlstm(r3(cq), r3(ck), r3(cv), r3(co), r3(gt), state_mlstm_c[p], state_mlstm_n[p],
                                       state_mlstm_m[p], c_gain[p], SLOT, SLOT_LO, hi, F32)
            vnew = _pad_rows(dv.reshape(db, SLOT * D_HEADS, D_VDIM), PAGE_SIZE * D_HEADS)
            d_out = _diff_sample(page_table, r3(dq), _keys_t(r3(dk)), vnew,
                                 d_lambda[p], d_subln[p], cache_d_k, cache_d_v, p, group, lam_init)
            outs["sd_k"].append(r3(dk)[:, SLOT_LO:hi].reshape(db, t, 2, D_HEADS, hd))
            outs["sd_v"].append(r3(dv)[:, SLOT_LO:hi].reshape(db, t, D_HEADS, D_VDIM))
            outs["s_mc"].append(mc); outs["s_mn"].append(mn); outs["s_mm"].append(mm)
            xs = _out_proj(xs, c_out.reshape(ms, -1), d_out.reshape(ms, -1), wo[:w512], wo[w512:],
                           norm_post_mix[l], msm[2], ms, 1)

        wup = ffn_w_up[l].astype(BF16)
        wa, wb, wd = wup[:, :d_ff], wup[:, d_ff:], ffn_w_down[l].astype(BF16)
        xp, tails = _ffn(xp, norm_pre_ffn[l], mp[3], mp[4], wa, wb, wd, ffn_conv_w[l], ffn_conv_b[l], None,
                         norm_post_ffn[l], mp[5], tm_ffn, s // tm_ffn, tf)
        outs["p_cv"].append(tails.reshape(b, s // tm_ffn, 2, d_ff)[:, -1])
        inj = jnp.pad(state_ffn_conv[l], ((0, 0), (0, SLOT - SLOT_LO), (0, 0))).reshape(ms, d_ff)
        xs, a_all = _ffn(xs, norm_pre_ffn[l], msm[3], msm[4], wa, wb, wd, ffn_conv_w[l], ffn_conv_b[l], inj,
                         norm_post_ffn[l], msm[5], ms, 1, tf)
        outs["s_cv"].append(a_all.reshape(db, SLOT, d_ff)[:, hi - (CONV_W - 1):hi])

    st = lambda k: jnp.stack(outs[k])
    y_sample = xs.reshape(db, SLOT, d)[:, SLOT_LO:hi]
    return (xp.reshape(b, s, d), y_sample,
            st("pa_kv"), st("pa_idx"), st("pd_k"), st("pd_v"), st("p_ret"), st("p_mc"), st("p_mn"), st("p_mm"), st("p_cv"),
            st("sa_kv"), st("sa_idx"), st("sd_k"), st("sd_v"), st("s_ret"), st("s_mc"), st("s_mn"), st("s_mm"), st("s_cv"))
```

```python
import functools
import math

import numpy as np
import jax
import jax.numpy as jnp
from jax import lax
from jax.experimental import pallas as pl
from jax.experimental.pallas import tpu as pltpu

F32 = jnp.float32
BF16 = jnp.bfloat16

PAGE_SIZE = 128
HEAD_DIM = 64
ROPE_THETA = 500000.0
RET_THETA = 10000.0
A_HEADS = 8
A_KV_HEADS = 2
IDX_HEADS = 4
IDX_DIM = 64
TOPK_MAX = 256
B_HEADS = 8
C_HEADS = 8
D_HEADS = 4
D_VDIM = 2 * HEAD_DIM
CONV_W = 3
CHUNK = 128
EPS = 1e-6
NEG = -1e30
LOG2E = 1.4426950408889634
SLOT = 8
SLOT_LO = CONV_W - 1
LANES = 128
VMEM_LIMIT = 56 * 1024 * 1024

_NT = (((1,), (1,)), ((), ()))


def _cp(*sem):
    return pltpu.CompilerParams(dimension_semantics=sem, vmem_limit_bytes=VMEM_LIMIT)


def _dot(a, b):
    return jnp.dot(a, b, preferred_element_type=F32)


def _dot_nt(a, b):
    return lax.dot_general(a, b, _NT, preferred_element_type=F32)


def _iota(shape, dim):
    return lax.broadcasted_iota(jnp.int32, shape, dim)


def _eye_bf16(n):
    return jnp.where(_iota((n, n), 0) == _iota((n, n), 1), 1.0, 0.0).astype(BF16)


def _dot_tn(a_bf16, b_bf16):
    at = _dot_nt(_eye_bf16(a_bf16.shape[1]), a_bf16).astype(BF16)
    return _dot(at, b_bf16)


def _split3(x):
    hi = x.astype(BF16)
    r1 = x - hi.astype(F32)
    mid = r1.astype(BF16)
    lo = (r1 - mid.astype(F32)).astype(BF16)
    return hi, mid, lo


def _rope_tables(pos, rot, theta):
    half = rot // 2
    inv = theta ** (-np.arange(half, dtype=np.float64) / half)
    ang = np.asarray(pos, np.float64)[:, None] * inv[None]
    cos, sin = np.cos(ang), np.sin(ang)
    n = len(pos)
    t0 = np.ones((n, HEAD_DIM)); t1 = np.zeros((n, HEAD_DIM)); t2 = np.zeros((n, HEAD_DIM))
    t0[:, :half] = cos; t0[:, half:rot] = cos
    t1[:, :half] = -sin
    t2[:, half:rot] = sin
    tab = np.stack([np.tile(t, (1, LANES // HEAD_DIM)) for t in (t0, t1, t2)])
    return jnp.asarray(tab, F32)


def _retention_tables(chunk, lo, hi):
    n = hi - lo
    log_g = np.log1p(-np.exp2(-5.0 - np.arange(B_HEADS, dtype=np.float64)))
    r = np.arange(chunk)
    ok = (r >= lo) & (r < hi)
    pos = (r - lo).astype(np.float64)
    diff = pos[:, None] - pos[None, :]
    dm = np.where((diff >= 0) & ok[:, None] & ok[None, :], np.exp(log_g[:, None, None] * np.maximum(diff, 0.0)), 0.0)
    qd = np.exp(log_g[:, None] * (pos + 1.0))[:, :, None] * np.ones((1, 1, HEAD_DIM))
    kd = np.where(ok, np.exp(log_g[:, None] * (n - 1.0 - pos)), 0.0)[:, :, None] * np.ones((1, 1, HEAD_DIM))
    cd = np.exp(log_g * n)[:, None, None] * np.ones((1, 1, HEAD_DIM))
    return tuple(jnp.asarray(a, F32) for a in (dm, qd, kd, cd))


def _mod_body(c_ref, w_ref, b_ref, o_ref):
    c = c_ref[...]
    s = (c * jax.nn.sigmoid(c)).astype(BF16)
    o_ref[0] = _dot(s, w_ref[0].astype(BF16)) + b_ref[0]


def _modulation(c_all, w_ada, b_ada):
    depth, d, n = w_ada.shape
    r = c_all.shape[0]
    tn = 1536
    return pl.pallas_call(
        _mod_body, grid=(depth, n // tn),
        in_specs=[pl.BlockSpec((r, d), lambda l, j: (0, 0)),
                  pl.BlockSpec((1, d, tn), lambda l, j: (l, 0, j)),
                  pl.BlockSpec((1, 1, tn), lambda l, j: (l, 0, j))],
        out_specs=pl.BlockSpec((1, r, tn), lambda l, j: (l, 0, j)),
        out_shape=jax.ShapeDtypeStruct((depth, r, n), F32),
        compiler_params=_cp("arbitrary", "arbitrary"), name="modulation",
    )(c_all, w_ada, b_ada.reshape(depth, 1, n))


def _ada_norm_bf16(x_ref, g_ref, sh_ref, sc_ref):
    x = x_ref[...]
    xn = x * lax.rsqrt(jnp.mean(x * x, axis=-1, keepdims=True) + EPS) * g_ref[...]
    return (xn * (1.0 + sc_ref[0]) + sh_ref[0]).astype(BF16)


def _rope128(x, tab, shift):
    xl = pltpu.roll(x, LANES - shift, 1)
    xr = pltpu.roll(x, shift, 1)
    return x * tab[0] + xl * tab[1] + xr * tab[2]


def _rope(u, tab, shift):
    return jnp.concatenate(
        [_rope128(u[:, c * LANES:(c + 1) * LANES], tab, shift) for c in range(u.shape[1] // LANES)], axis=1)


def _ab_in_body(x_ref, g_ref, sh_ref, sc_ref, tp_ref, tf_ref,
                wqa, wkv, wqi, wki, wbq, wbk, wbv, wbg,
                qa_o, kv_o, qi_o, kiw_o, bq_o, bk_o, bv_o, bg_o):
    h = _ada_norm_bf16(x_ref, g_ref, sh_ref, sc_ref)
    tp = tp_ref[...]
    tf = tf_ref[...]
    pshift, fshift = HEAD_DIM // 8, HEAD_DIM // 2
    qa_o[...] = _rope(_dot(h, wqa[...]), tp, pshift).astype(qa_o.dtype)
    u = _dot(h, wkv[...])
    kv_o[...] = jnp.concatenate([_rope128(u[:, :LANES], tp, pshift), u[:, LANES:]], axis=1)
    qi_o[...] = _rope(_dot(h, wqi[...]), tp, pshift).astype(qi_o.dtype)
    u = _dot(h, wki[...])
    kiw_o[...] = jnp.where(_iota(u.shape, 1) < IDX_DIM, _rope128(u, tp, pshift), u)
    bq_o[...] = _rope(_dot(h, wbq[...]), tf, fshift).astype(bq_o.dtype)
    bk_o[...] = _rope(_dot(h, wbk[...]), tf, fshift) * HEAD_DIM ** -0.5
    bv_o[...] = _dot(h, wbv[...]).astype(bv_o.dtype)
    bg_o[...] = _dot(h, wbg[...])


def _cd_in_body(x_ref, g_ref, sh_ref, sc_ref, tp_ref, gb_ref,
                wcq, wck, wcv, wco, wgt, wdq, wdk, wdv,
                cq_o, ck_o, cv_o, co_o, gt_o, dq_o, dk_o, dv_o):
    h = _ada_norm_bf16(x_ref, g_ref, sh_ref, sc_ref)
    tp = tp_ref[...]
    pshift = HEAD_DIM // 8
    cq_o[...] = _dot(h, wcq[...]).astype(cq_o.dtype)
    ck_o[...] = _dot(h, wck[...]) * HEAD_DIM ** -0.5
    cv_o[...] = _dot(h, wcv[...]).astype(cv_o.dtype)
    co_o[...] = _dot(h, wco[...])
    gt_o[...] = _dot(h, wgt[...]) + gb_ref[...]
    dq_o[...] = _rope(_dot(h, wdq[...]), tp, pshift).astype(dq_o.dtype)
    dk_o[...] = _rope(_dot(h, wdk[...]), tp, pshift)
    dv_o[...] = _dot(h, wdv[...])


def _in_proj(body, name, x, g, shift, scale, tables, extra, weights, out_defs, tm, tps):
    m, d = x.shape
    r = shift.shape[1]
    mod_spec = pl.BlockSpec((1, r, d), lambda i: (i // tps, 0, 0))
    in_specs = [pl.BlockSpec((tm, d), lambda i: (i, 0)), pl.BlockSpec((1, d), lambda i: (0, 0)), mod_spec, mod_spec]
    in_specs += [pl.BlockSpec((3, tm, LANES), lambda i: (0, i % tps, 0)) for _ in tables]
    in_specs += [pl.BlockSpec(e.shape, lambda i: (0, 0)) for e in extra]
    in_specs += [pl.BlockSpec(w.shape, lambda i: (0, 0)) for w in weights]
    return pl.pallas_call(
        body, grid=(m // tm,), in_specs=in_specs,
        out_specs=[pl.BlockSpec((tm, n), lambda i: (i, 0)) for n, _ in out_defs],
        out_shape=[jax.ShapeDtypeStruct((m, n), dt) for n, dt in out_defs],
        compiler_params=_cp("arbitrary"), name=name,
    )(x, g.reshape(1, d), shift, scale, *tables, *extra, *weights)


def _out_proj_body(x_ref, a_ref, b_ref, wa_ref, wb_ref, g_ref, gate_ref, o_ref):
    y = _dot(a_ref[...].astype(BF16), wa_ref[...]) + _dot(b_ref[...].astype(BF16), wb_ref[...])
    yn = y * lax.rsqrt(jnp.mean(y * y, axis=-1, keepdims=True) + EPS) * g_ref[...]
    o_ref[...] = x_ref[...] + gate_ref[0] * yn


def _out_proj(x, a, b, wa, wb, g, gate, tm, tps):
    m, d = x.shape
    r = gate.shape[1]
    return pl.pallas_call(
        _out_proj_body, grid=(m // tm,),
        in_specs=[pl.BlockSpec((tm, d), lambda i: (i, 0)),
                  pl.BlockSpec((tm, a.shape[1]), lambda i: (i, 0)),
                  pl.BlockSpec((tm, b.shape[1]), lambda i: (i, 0)),
                  pl.BlockSpec(wa.shape, lambda i: (0, 0)), pl.BlockSpec(wb.shape, lambda i: (0, 0)),
                  pl.BlockSpec((1, d), lambda i: (0, 0)),
                  pl.BlockSpec((1, r, d), lambda i: (i // tps, 0, 0))],
        out_specs=pl.BlockSpec((tm, d), lambda i: (i, 0)),
        out_shape=jax.ShapeDtypeStruct((m, d), F32),
        compiler_params=_cp("arbitrary"), name="out_proj",
    )(x, a, b, wa, wb, g.reshape(1, d), gate)


def _ffn_body(x_ref, g1_ref, sh_ref, sc_ref, wa_ref, wb_ref, wd_ref, cw_ref, cb_ref, inj_ref, g2_ref, gate_ref,
              o_ref, aux_ref, h_sc, acc_sc, carry_sc, *, tps, tf, inject):
    i = pl.program_id(0)
    f = pl.program_id(1)
    tm = x_ref.shape[0]

    @pl.when(f == 0)
    def _():
        h_sc[...] = _ada_norm_bf16(x_ref, g1_ref, sh_ref, sc_ref)
        acc_sc[...] = jnp.zeros_like(acc_sc)

    h = h_sc[...]
    a = _dot(h, wa_ref[...])
    b = _dot(h, wb_ref[...])
    row = _iota((tm, tf), 0)
    if inject:
        a = jnp.where(row % SLOT < SLOT_LO, inj_ref[...], a)
        aux_ref[...] = a
        p1 = jnp.zeros((1, tf), F32)
        p2 = jnp.zeros((2, tf), F32)
    else:
        @pl.when(i % tps == 0)
        def _():
            carry_sc[f] = jnp.zeros((2, tf), F32)

        prev = carry_sc[f]
        carry_sc[f] = a[tm - 2:, :]
        aux_ref[0] = a[tm - 2:, :]
        p1 = prev[1:2, :]
        p2 = prev
    a1 = jnp.where(row == 0, p1, pltpu.roll(a, 1, 0))
    a2 = pltpu.roll(a, 2, 0)
    a2 = jnp.where(row == 0, p2[0:1, :], jnp.where(row == 1, p2[1:2, :], a2))
    cw = cw_ref[...]
    conv = cb_ref[...] + a2 * cw[0:1, :]
    conv = conv + a1 * cw[1:2, :]
    conv = conv + a * cw[2:3, :]
    y = jax.nn.gelu(conv, approximate=True) * b
    acc_sc[...] += _dot(y.astype(BF16), wd_ref[...])

    @pl.when(f == pl.num_programs(1) - 1)
    def _():
        yv = acc_sc[...]
        yn = yv * lax.rsqrt(jnp.mean(yv * yv, axis=-1, keepdims=True) + EPS) * g2_ref[...]
        o_ref[...] = x_ref[...] + gate_ref[0] * yn


def _ffn(x, g1, shift, scale, w_up, w_down, layer, cw, cb, inj, g2, gate, tm, tps, tf):
    m, d = x.shape
    ff = w_down.shape[1]
    nf = ff // tf
    r = shift.shape[1]
    inject = inj is not None
    if inject:
        aux_shape, aux_spec = (m, ff), pl.BlockSpec((tm, tf), lambda i, f: (i, f))
        inj_spec = pl.BlockSpec((tm, tf), lambda i, f: (i, f))
    else:
        aux_shape, aux_spec = (m // tm, 2, ff), pl.BlockSpec((1, 2, tf), lambda i, f: (i, 0, f))
        inj = jnp.zeros((8, LANES), F32)
        inj_spec = pl.BlockSpec((8, LANES), lambda i, f: (0, 0))
    mod_spec = pl.BlockSpec((1, r, d), lambda i, f: (i // tps, 0, 0))
    return pl.pallas_call(
        functools.partial(_ffn_body, tps=tps, tf=tf, inject=inject),
        grid=(m // tm, nf),
        in_specs=[pl.BlockSpec((tm, d), lambda i, f: (i, 0)), pl.BlockSpec((1, d), lambda i, f: (0, 0)),
                  mod_spec, mod_spec,
                  pl.BlockSpec((None, d, tf), lambda i, f: (layer, 0, f)),
                  pl.BlockSpec((None, d, tf), lambda i, f: (layer, 0, f + nf)),
                  pl.BlockSpec((None, tf, d), lambda i, f: (layer, f, 0)),
                  pl.BlockSpec((CONV_W, tf), lambda i, f: (0, f)), pl.BlockSpec((1, tf), lambda i, f: (0, f)),
                  inj_spec, pl.BlockSpec((1, d), lambda i, f: (0, 0)), mod_spec],
        out_specs=[pl.BlockSpec((tm, d), lambda i, f: (i, 0)), aux_spec],
        out_shape=[jax.ShapeDtypeStruct((m, d), F32), jax.ShapeDtypeStruct(aux_shape, F32)],
        scratch_shapes=[pltpu.VMEM((tm, d), BF16), pltpu.VMEM((tm, d), F32), pltpu.VMEM((nf, 2, tf), F32)],
        compiler_params=_cp("arbitrary", "arbitrary"), name="conv_ffn",
    )(x, g1.reshape(1, d), shift, scale, w_up, w_up, w_down, cw, cb.reshape(1, ff), inj, g2.reshape(1, d), gate)


def _kth_largest(stats_fn, cmin0, cmax0, k):
    def flag(lo, hi):
        return jnp.max(jnp.where(lo < hi, 1.0, 0.0))

    def body(st):
        cmin, cmax, _ = st
        mid = cmin + 0.5 * (cmax - cmin)
        p = jnp.where(mid > cmin, jnp.where(mid <= cmax, mid, cmax), cmax)
        cnt, mn_ge, mx_lt = stats_fn(p)
        is_open = cmin < cmax
        take_lo = cnt >= k
        ncmin = jnp.where(is_open, jnp.where(take_lo, mn_ge, cmin), cmin)
        ncmax = jnp.where(is_open, jnp.where(take_lo, cmax, mx_lt), cmax)
        return ncmin, ncmax, flag(ncmin, ncmax)

    return lax.while_loop(lambda st: st[2] > 0.0, body, (cmin0, cmax0, flag(cmin0, cmax0)))[0]


def _lane_blocks(s):
    return [s[:, c * LANES:(c + 1) * LANES] for c in range(s.shape[1] // LANES)]


def _softmax_probs(blocks, m_prev, l_prev, exp=jnp.exp):
    mloc = functools.reduce(jnp.maximum, blocks)
    m_new = jnp.maximum(m_prev, jnp.max(mloc, axis=1, keepdims=True))
    alpha = exp(m_prev - m_new)
    ps = [exp(b - m_new) for b in blocks]
    l_new = alpha * l_prev + jnp.sum(functools.reduce(jnp.add, ps), axis=1, keepdims=True)
    p = (jnp.concatenate(ps, axis=1) if len(ps) > 1 else ps[0]).astype(BF16)
    return p, alpha, m_new, l_new


def _scale_acc(alpha, acc):
    dv = acc.shape[-1]
    if dv < LANES:
        return alpha[:, :dv] * acc
    return (alpha if dv == LANES else jnp.concatenate([alpha] * (dv // LANES), axis=1)) * acc


def _softmax_step(s, v_bf16, m_ref, l_ref, acc_ref, idx, values_transposed=False, exp=jnp.exp):
    p, alpha, m_new, l_new = _softmax_probs(_lane_blocks(s), m_ref[idx], l_ref[idx], exp)
    pv = _dot_nt(p, v_bf16) if values_transposed else _dot(p, v_bf16)
    acc_ref[idx] = _scale_acc(alpha, acc_ref[idx]) + pv
    m_ref[idx] = m_new
    l_ref[idx] = l_new


def _tie_bias(x, t, need, carry, tri):
    eq = x == t
    pref = _dot(jnp.where(eq, 1.0, 0.0).astype(BF16), tri) + carry
    bias = jnp.where(x > t, 0.0, jnp.where(eq, jnp.where(pref <= need, 0.0, NEG), NEG))
    return bias, pref[:, x.shape[1] - 1:]


def _fold8(x, op):
    parts = [x[i * 8:(i + 1) * 8] for i in range(x.shape[0] // 8)]
    while len(parts) > 1:
        parts = [op(parts[i], parts[i + 1]) for i in range(0, len(parts) - 1, 2)] + (parts[-1:] if len(parts) % 2 else [])
    return parts[0]


def _dsa_prompt_body(qa_ref, qi_ref, kiwq_ref, kiwk_ref, kv_ref, tril_ref, o_ref,
                     sct, sc, m_sc, l_sc, acc_sc, *, tq, ch, topk, light_passes):
    j = pl.program_id(1)
    q0 = j * tq
    nc = (q0 + tq + ch - 1) // ch
    qpos = q0 + _iota((1, tq), 1)
    kf = float(topk)
    inf = jnp.inf
    qi = qi_ref[0] * IDX_DIM ** -0.5
    qi4 = jnp.concatenate([qi[:, h * IDX_DIM:(h + 1) * IDX_DIM] for h in range(IDX_HEADS)], axis=0).astype(BF16)
    pick = jnp.where(_iota((8, LANES), 1) == _iota((8, LANES), 0) + IDX_DIM, 1.0, 0.0).astype(BF16)
    wt = sum(_dot_nt(pick, part) for part in _split3(kiwq_ref[0])) * IDX_HEADS ** -0.5

    def score_chunk(c, st):
        mx, mn = st
        c0 = pl.multiple_of(c * ch, ch)
        kc = kiwk_ref[0, pl.ds(c0, ch), :][:, :IDX_DIM].astype(BF16)
        lg = _dot_nt(kc, qi4)
        s = jnp.zeros((ch, tq), F32)
        for h in range(IDX_HEADS):
            s = s + jnp.maximum(lg[:, h * tq:(h + 1) * tq], 0.0) * wt[h:h + 1, :]
        s = jnp.where(s == 0.0, 0.0, s)
        valid = (c0 + _iota((ch, tq), 0)) <= qpos
        sct[c] = jnp.where(valid, s, -inf)
        mx = jnp.maximum(mx, _fold8(jnp.where(valid, s, -inf), jnp.maximum))
        mn = jnp.minimum(mn, _fold8(jnp.where(valid, s, inf), jnp.minimum))
        return mx, mn

    mx, mn = lax.fori_loop(0, nc, score_chunk, (jnp.full((8, tq), -inf, F32), jnp.full((8, tq), inf, F32)))
    active = qpos + 1 > topk
    row_max = jnp.where(active, jnp.max(mx, axis=0, keepdims=True), 0.0)
    row_min = jnp.where(active, jnp.min(mn, axis=0, keepdims=True), 0.0)

    def count_ge(p):
        def body(c, cnt):
            return cnt + _fold8(jnp.where(sct[c] >= p, 1.0, 0.0), jnp.add)
        return jnp.sum(lax.fori_loop(0, nc, body, jnp.zeros((8, tq), F32)), axis=0, keepdims=True)

    def stats(p):
        def body(c, st):
            cnt, mnge, mxlt = st
            x = sct[c]
            ge = x >= p
            return (cnt + _fold8(jnp.where(ge, 1.0, 0.0), jnp.add),
                    jnp.minimum(mnge, _fold8(jnp.where(ge, x, inf), jnp.minimum)),
                    jnp.maximum(mxlt, _fold8(jnp.where(ge, -inf, x), jnp.maximum)))
        init = (jnp.zeros((8, tq), F32), jnp.full((8, tq), inf, F32), jnp.full((8, tq), -inf, F32))
        cnt, mnge, mxlt = lax.fori_loop(0, nc, body, init)
        return (jnp.sum(cnt, axis=0, keepdims=True), jnp.min(mnge, axis=0, keepdims=True),
                jnp.max(mxlt, axis=0, keepdims=True))

    def light(i, st):
        lo, hi = st
        mid = lo + 0.5 * (hi - lo)
        take = count_ge(mid) >= kf
        return jnp.where(take, mid, lo), jnp.where(take, hi, mid)

    lo, hi = lax.fori_loop(0, light_passes, light, (row_min, row_max))
    _, cmin, _ = stats(lo)
    cnt_hi, mn_hi, mx_hi = stats(hi)
    at_top = cnt_hi >= kf
    cmin = jnp.where(at_top, mn_hi, cmin)
    cmax = jnp.where(at_top, row_max, mx_hi)
    t = jnp.where(active, _kth_largest(stats, cmin, cmax, kf), -inf)

    def count_sel(c, st):
        x = sct[c]
        return (st[0] + _fold8(jnp.where(x > t, 1.0, 0.0), jnp.add),
                st[1] + _fold8(jnp.where(x >= t, 1.0, 0.0), jnp.add))

    gt, ge = lax.fori_loop(0, nc, count_sel, (jnp.zeros((8, tq), F32), jnp.zeros((8, tq), F32)))
    need = kf - jnp.sum(gt, axis=0, keepdims=True)
    over = jnp.sum(ge, axis=0, keepdims=True) > kf
    has_ties = jnp.max(jnp.where(active, jnp.where(over, 1.0, 0.0), 0.0)) > 0.0
    eye = _eye_bf16(tq)

    def store_bias(c, sel_t):
        sel = _dot_nt(eye, sel_t.astype(BF16))
        sc[c] = jnp.where(sel > 0.5, 0.0, NEG)

    @pl.when(has_ties)
    def _():
        tril = tril_ref[...]

        def tie_chunk(c, carry):
            x = sct[c]
            eq = x == t
            pref = _dot(tril, jnp.where(eq, 1.0, 0.0).astype(BF16)) + carry
            store_bias(c, jnp.where(x > t, 1.0, jnp.where(eq, jnp.where(pref <= need, 1.0, 0.0), 0.0)))
            return pref[ch - 1:, :]

        lax.fori_loop(0, nc, tie_chunk, jnp.zeros((1, tq), F32))

    @pl.when(jnp.logical_not(has_ties))
    def _():
        def plain_chunk(c, carry):
            store_bias(c, jnp.where(sct[c] >= t, 1.0, 0.0))
            return carry

        lax.fori_loop(0, nc, plain_chunk, 0)

    cl = nc - 1
    sc[cl] = jnp.where((cl * ch + _iota((tq, ch), 1)) <= q0 + _iota((tq, ch), 0), sc[cl], NEG)

    qa = qa_ref[0].astype(F32) * (HEAD_DIM ** -0.5 * LOG2E)
    hpg = A_HEADS // A_KV_HEADS
    qh = [qa[:, h * HEAD_DIM:(h + 1) * HEAD_DIM].astype(BF16) for h in range(A_HEADS)]
    m_sc[...] = jnp.full(m_sc.shape, NEG, F32)
    l_sc[...] = jnp.zeros(l_sc.shape, F32)
    acc_sc[...] = jnp.zeros(acc_sc.shape, F32)

    def attend(c, carry):
        c0 = pl.multiple_of(c * ch, ch)
        bias = _lane_blocks(sc[c])
        kvc = kv_ref[0, pl.ds(c0, ch), :]
        kcs = [kvc[:, g * HEAD_DIM:(g + 1) * HEAD_DIM].astype(BF16) for g in range(A_KV_HEADS)]
        vcs = [kvc[:, (A_KV_HEADS + g) * HEAD_DIM:(A_KV_HEADS + g + 1) * HEAD_DIM].astype(BF16) for g in range(A_KV_HEADS)]
        ss = [_dot_nt(qh[hh], kcs[hh // hpg]) for hh in range(A_HEADS)]
        sm = []
        for hh in range(A_HEADS):
            g, rs = hh // hpg, pl.ds((hh % hpg) * tq, tq)
            blocks = [sb + bb for sb, bb in zip(_lane_blocks(ss[hh]), bias)]
            sm.append(_softmax_probs(blocks, m_sc[g, rs, :], l_sc[g, rs, :], exp=jnp.exp2))
        pv = [_dot(jnp.concatenate([sm[g * hpg + h][0] for h in range(hpg)], axis=0), vcs[g]) for g in range(A_KV_HEADS)]
        for hh in range(A_HEADS):
            g, rs = hh // hpg, pl.ds((hh % hpg) * tq, tq)
            m_sc[g, rs, :] = sm[hh][2]
            l_sc[g, rs, :] = sm[hh][3]
        for g in range(A_KV_HEADS):
            alpha = jnp.concatenate([sm[g * hpg + h][1] for h in range(hpg)], axis=0)
            acc_sc[g] = _scale_acc(alpha, acc_sc[g]) + pv[g]
        return carry

    lax.fori_loop(0, nc, attend, 0)
    outs = []
    for g in range(A_KV_HEADS):
        og = acc_sc[g] / l_sc[g][:, :HEAD_DIM]
        outs += [og[h * tq:(h + 1) * tq] for h in range(hpg)]
    o_ref[0] = jnp.concatenate(outs, axis=1).astype(o_ref.dtype)


def _upper_tri_bf16(n):
    return jnp.asarray(np.triu(np.ones((n, n), np.float32)), BF16)


def _dsa_prompt(qa, qi, kiw, kv, tq=128, ch=512):
    b, s, _ = qa.shape
    ch = min(ch, s)
    topk = min(TOPK_MAX, s // 4)
    hpg = A_HEADS // A_KV_HEADS
    return pl.pallas_call(
        functools.partial(_dsa_prompt_body, tq=tq, ch=ch, topk=topk, light_passes=14),
        grid=(b, s // tq),
        in_specs=[pl.BlockSpec((1, tq, qa.shape[2]), lambda i, j: (i, j, 0)),
                  pl.BlockSpec((1, tq, qi.shape[2]), lambda i, j: (i, j, 0)),
                  pl.BlockSpec((1, tq, LANES), lambda i, j: (i, j, 0)),
                  pl.BlockSpec((1, s, LANES), lambda i, j: (i, 0, 0)),
                  pl.BlockSpec((1, s, kv.shape[2]), lambda i, j: (i, 0, 0)),
                  pl.BlockSpec((ch, ch), lambda i, j: (0, 0))],
        out_specs=pl.BlockSpec((1, tq, A_HEADS * HEAD_DIM), lambda i, j: (i, j, 0)),
        out_shape=jax.ShapeDtypeStruct((b, s, A_HEADS * HEAD_DIM), BF16),
        scratch_shapes=[pltpu.VMEM((s // ch, ch, tq), F32),
                        pltpu.VMEM((s // ch, tq, ch), F32),
                        pltpu.VMEM((A_KV_HEADS, hpg * tq, LANES), F32),
                        pltpu.VMEM((A_KV_HEADS, hpg * tq, LANES), F32),
                        pltpu.VMEM((A_KV_HEADS, hpg * tq, HEAD_DIM), F32)],
        compiler_params=_cp("arbitrary", "arbitrary"), name="dsa_prompt",
    )(qa, qi, kiw, kiw, kv, jnp.asarray(np.tril(np.ones((ch, ch), np.float32)), BF16))


def _head_avg_matrix(width):
    seg = np.arange(width) // HEAD_DIM
    return jnp.asarray((seg[:, None] == seg[None, :]) / HEAD_DIM, BF16)


def _head_mean(x, avg):
    hi = x.astype(BF16)
    lo = (x - hi.astype(F32)).astype(BF16)
    return _dot(hi, avg) + _dot(lo, avg)


def _head_layernorm(o, gain, avg):
    d = o - _head_mean(o, avg)
    return d * lax.rsqrt(_head_mean(d * d, avg) + EPS) * gain


def _retention_body(q_ref, k_ref, v_ref, g_ref, s0_ref, dm_ref, qd_ref, kd_ref, cd_ref, gain_ref, avg_ref,
                    o_ref, sf_ref, s_sc):
    c = pl.program_id(1)

    @pl.when(c == 0)
    def _():
        s_sc[...] = s0_ref[0]

    q = q_ref[0].astype(BF16)
    k = k_ref[0]
    v = v_ref[0].astype(BF16)
    gt = g_ref[0]
    heads = range(B_HEADS)
    sls = [slice(h * HEAD_DIM, (h + 1) * HEAD_DIM) for h in heads]
    eye = _eye_bf16(HEAD_DIM)
    st = [s_sc[h] for h in heads]
    att = [_dot_nt(q[:, sl], k[:, sl].astype(BF16)) for sl in sls]
    qs = [_dot(q[:, sls[h]], st[h].astype(BF16)) for h in heads]
    kt = [_dot_nt(eye, (k[:, sls[h]] * kd_ref[h]).astype(BF16)) for h in heads]
    outs = [_dot((att[h] * dm_ref[h]).astype(BF16), v[:, sls[h]]) + qs[h] * qd_ref[h] for h in heads]
    for h in heads:
        s_sc[h] = st[h] * cd_ref[h] + _dot(kt[h].astype(BF16), v[:, sls[h]])
    o = _head_layernorm(jnp.concatenate(outs, axis=1), gain_ref[...], avg_ref[...])
    o_ref[0] = (gt * jax.nn.sigmoid(gt) * o).astype(o_ref.dtype)

    @pl.when(c == pl.num_programs(1) - 1)
    def _():
        sf_ref[0] = s_sc[...]


def _retention(q, k, v, g, s0, gain, chunk, lo, hi, out_dtype):
    b, s, w = q.shape
    dm, qd, kd, cd = _retention_tables(chunk, lo, hi)
    row = lambda i, c: (i, c, 0)
    fix3 = lambda i, c: (0, 0, 0)
    st_spec = pl.BlockSpec((1, B_HEADS, HEAD_DIM, HEAD_DIM), lambda i, c: (i, 0, 0, 0))
    return pl.pallas_call(
        _retention_body, grid=(b, s // chunk),
        in_specs=[pl.BlockSpec((1, chunk, w), row)] * 4 + [st_spec] +
                 [pl.BlockSpec(t.shape, fix3) for t in (dm, qd, kd, cd)] +
                 [pl.BlockSpec((1, w), lambda i, c: (0, 0)), pl.BlockSpec((w, w), lambda i, c: (0, 0))],
        out_specs=[pl.BlockSpec((1, chunk, w), row), st_spec],
        out_shape=[jax.ShapeDtypeStruct((b, s, w), out_dtype),
                   jax.ShapeDtypeStruct((b, B_HEADS, HEAD_DIM, HEAD_DIM), F32)],
        scratch_shapes=[pltpu.VMEM((B_HEADS, HEAD_DIM, HEAD_DIM), F32)],
        compiler_params=_cp("arbitrary", "arbitrary"), name="retention",
    )(q, k, v, g, s0, dm, qd, kd, cd, gain.reshape(1, w), _head_avg_matrix(w))


def _mlstm_body(q_ref, k_ref, v_ref, og_ref, gt_ref, cn0_ref, m0_ref, gain_ref, avg_ref,
                o_ref, cnf_ref, mf_ref, cn_sc, m_sc, *, lo, hi):
    c = pl.program_id(1)
    ck = q_ref.shape[1]
    nh = C_HEADS
    hd = HEAD_DIM

    @pl.when(c == 0)
    def _():
        cn_sc[...] = cn0_ref[0]
        m_sc[...] = m0_ref[0]

    q = q_ref[0].astype(BF16)
    k = k_ref[0]
    v = v_ref[0].astype(BF16)
    gates = gt_ref[0]
    row = _iota((ck, LANES), 0)
    tok = (row >= lo) & (row < hi)
    log_sig = jnp.minimum(gates, 0.0) - jnp.log1p(jnp.exp(-jnp.abs(gates)))
    lf = jnp.where(tok, log_sig, 0.0)
    ii = jnp.where(tok, gates, NEG)
    tril = jnp.where(_iota((ck, ck), 0) >= _iota((ck, ck), 1), 1.0, 0.0).astype(BF16)
    fcum = sum(_dot(tril, part) for part in _split3(lf))
    f_all = pltpu.roll(fcum, LANES - nh, 1)
    a_all = ii - f_all
    cm = a_all
    step = 1
    while step < ck:
        cm = jnp.maximum(cm, jnp.where(row >= step, pltpu.roll(cm, step, 0), -jnp.inf))
        step *= 2
    m_prev = m_sc[...]
    m_t = f_all + jnp.maximum(m_prev, cm)
    inter_all = jnp.exp(f_all + m_prev - m_t)
    floor_all = jnp.exp(-m_t)
    g_all = f_all - m_t
    f_last = f_all[ck - 1:, :]
    m_end = m_t[ck - 1:, :]
    w_all = jnp.exp(f_last + a_all - m_end)
    dec_all = jnp.exp(f_last + m_prev - m_end)
    m_sc[...] = m_end
    pick = jnp.where(_iota((nh, LANES), 0) == _iota((nh, LANES), 1), 1.0, 0.0).astype(BF16)
    a_rows = sum(_dot_nt(pick, part) for part in _split3(a_all))
    causal = _iota((ck, ck), 0) >= _iota((ck, ck), 1)
    eye = _eye_bf16(hd)
    ones = jnp.ones((ck, hd), BF16)
    n_lanes = _iota((hd, 2 * hd), 1) >= hd
    heads = range(nh)
    sls = [slice(h * hd, (h + 1) * hd) for h in heads]
    col = lambda x, h: x[:, h:h + 1]
    cn = [cn_sc[h] for h in heads]
    v1 = [jnp.concatenate([v[:, sl], ones], axis=1) for sl in sls]
    qk = [_dot_nt(q[:, sl], k[:, sl].astype(BF16)) for sl in sls]
    qc = [_dot(q[:, sls[h]], cn[h].astype(BF16)) for h in heads]
    kw = [k[:, sls[h]] * col(w_all, h) for h in heads]
    kw_hi = [x.astype(BF16) for x in kw]
    kw_lo = [(x - y.astype(F32)).astype(BF16) for x, y in zip(kw, kw_hi)]
    kt_hi = [_dot_nt(eye, x) for x in kw_hi]
    kt_lo = [_dot_nt(eye, x) for x in kw_lo]
    dmat = [jnp.exp(jnp.where(causal, col(g_all, h) + a_rows[h:h + 1, :], -jnp.inf)) for h in heads]
    nd = [_dot((qk[h] * dmat[h]).astype(BF16), v1[h]) + col(inter_all, h) * qc[h] for h in heads]
    outs = [(nd[h] / jnp.maximum(jnp.abs(pltpu.roll(nd[h], hd, 1)), col(floor_all, h)))[:, :hd] for h in heads]
    upd = [_dot(kt_hi[h].astype(BF16), v1[h]) for h in heads]
    fix = [_dot(kt_lo[h].astype(BF16), v1[h]) for h in heads]
    for h in heads:
        cn_sc[h] = col(dec_all, h) * cn[h] + upd[h] + jnp.where(n_lanes, fix[h], 0.0)
    hc = jax.nn.sigmoid(og_ref[0]) * jnp.concatenate(outs, axis=1)
    o_ref[0] = _head_layernorm(hc, gain_ref[...], avg_ref[...]).astype(o_ref.dtype)

    @pl.when(c == pl.num_programs(1) - 1)
    def _():
        cnf_ref[0] = cn_sc[...]
        mf_ref[0] = m_sc[...]


def _mlstm(q, k, v, og, gates, c0, n0, m0, gain, chunk, lo, hi, out_dtype):
    b, s, w = q.shape
    nh = C_HEADS
    row = lambda i, c: (i, c, 0)
    hd = HEAD_DIM
    cn_spec = pl.BlockSpec((1, nh, hd, 2 * hd), lambda i, c: (i, 0, 0, 0))
    m_spec = pl.BlockSpec((1, 1, LANES), lambda i, c: (i, 0, 0))
    cn0 = jnp.concatenate([c0, jnp.broadcast_to(n0[..., None], (b, nh, hd, hd))], axis=-1)
    m0p = jnp.pad(m0, ((0, 0), (0, LANES - nh))).reshape(b, 1, LANES)
    o, cnf, mf = pl.pallas_call(
        functools.partial(_mlstm_body, lo=lo, hi=hi), grid=(b, s // chunk),
        in_specs=[pl.BlockSpec((1, chunk, w), row)] * 4 + [pl.BlockSpec((1, chunk, LANES), row), cn_spec, m_spec,
                                                          pl.BlockSpec((1, w), lambda i, c: (0, 0)),
                                                          pl.BlockSpec((w, w), lambda i, c: (0, 0))],
        out_specs=[pl.BlockSpec((1, chunk, w), row), cn_spec, m_spec],
        out_shape=[jax.ShapeDtypeStruct((b, s, w), out_dtype),
                   jax.ShapeDtypeStruct((b, nh, hd, 2 * hd), F32),
                   jax.ShapeDtypeStruct((b, 1, LANES), F32)],
        scratch_shapes=[pltpu.VMEM((nh, hd, 2 * hd), F32), pltpu.VMEM((1, LANES), F32)],
        compiler_params=_cp("arbitrary", "arbitrary"), name="mlstm",
    )(q, k, v, og, gates, cn0, m0p, gain.reshape(1, w), _head_avg_matrix(w))
    return o, cnf[..., :hd], cnf[..., hd], mf[:, 0, :nh]


def _diff_lambda(lam_ref, lam_init):
    lp = lam_ref[...]
    s01 = jnp.sum(lp[0:1] * lp[1:2], axis=1, keepdims=True)
    s23 = jnp.sum(lp[2:3] * lp[3:4], axis=1, keepdims=True)
    return jnp.exp(s01) - jnp.exp(s23) + lam_init


def _diff_finish(o0, l0, o1, l1, lam, subln, lam_init):
    od = o0 / l0 - lam * (o1 / l1)
    return od * lax.rsqrt(jnp.mean(od * od, axis=-1, keepdims=True) + EPS) * subln * (1.0 - lam_init)


def _diff_prompt_body(q_ref, k_ref, v_ref, lam_ref, sub_ref, o_ref, m_sc, l_sc, acc_sc, *, lam_init):
    i = pl.program_id(1)
    j = pl.program_id(2)
    t = q_ref.shape[1]

    @pl.when(j == 0)
    def _():
        m_sc[...] = jnp.full(m_sc.shape, NEG, F32)
        l_sc[...] = jnp.zeros(l_sc.shape, F32)
        acc_sc[...] = jnp.zeros(acc_sc.shape, F32)

    def step(diag):
        q = (q_ref[0].astype(F32) * (HEAD_DIM ** -0.5 * LOG2E)).astype(BF16)
        k = k_ref[0].astype(BF16)
        v = v_ref[0].astype(BF16)
        keep = (_iota((t, t), 0) >= _iota((t, t), 1)) if diag else None
        pairs = range(2 * D_HEADS)
        ss = [_dot_nt(q[:, idx * HEAD_DIM:(idx + 1) * HEAD_DIM], k[:, idx * HEAD_DIM:(idx + 1) * HEAD_DIM]) for idx in pairs]
        if diag:
            ss = [jnp.where(keep, s, NEG) for s in ss]
        sm = [_softmax_probs(_lane_blocks(ss[idx]), m_sc[idx], l_sc[idx], jnp.exp2) for idx in pairs]
        pv = [_dot(sm[idx][0], v[:, (idx % D_HEADS) * D_VDIM:(idx % D_HEADS + 1) * D_VDIM]) for idx in pairs]
        for idx in pairs:
            _, alpha, m_new, l_new = sm[idx]
            acc_sc[idx] = _scale_acc(alpha, acc_sc[idx]) + pv[idx]
            m_sc[idx] = m_new
            l_sc[idx] = l_new

    @pl.when(j < i)
    def _():
        step(False)

    @pl.when(j == i)
    def _():
        step(True)
        lam = _diff_lambda(lam_ref, lam_init)
        outs = [_diff_finish(acc_sc[h], l_sc[h], acc_sc[D_HEADS + h], l_sc[D_HEADS + h], lam, sub_ref[...], lam_init)
                for h in range(D_HEADS)]
        o_ref[0] = jnp.concatenate(outs, axis=1).astype(o_ref.dtype)


def _diff_prompt(q, k, v, lam_par, subln, lam_init, t=512):
    b, s, w = q.shape
    t = min(t, s)
    n = s // t
    return pl.pallas_call(
        functools.partial(_diff_prompt_body, lam_init=lam_init), grid=(b, n, n),
        in_specs=[pl.BlockSpec((1, t, w), lambda bi, i, j: (bi, i, 0)),
                  pl.BlockSpec((1, t, w), lambda bi, i, j: (bi, jnp.minimum(i, j), 0)),
                  pl.BlockSpec((1, t, w), lambda bi, i, j: (bi, jnp.minimum(i, j), 0)),
                  pl.BlockSpec(lam_par.shape, lambda bi, i, j: (0, 0)),
                  pl.BlockSpec((1, D_VDIM), lambda bi, i, j: (0, 0))],
        out_specs=pl.BlockSpec((1, t, w), lambda bi, i, j: (bi, i, 0)),
        out_shape=jax.ShapeDtypeStruct((b, s, w), BF16),
        scratch_shapes=[pltpu.VMEM((2 * D_HEADS, t, LANES), F32), pltpu.VMEM((2 * D_HEADS, t, LANES), F32),
                        pltpu.VMEM((2 * D_HEADS, t, D_VDIM), F32)],
        compiler_params=_cp("arbitrary", "arbitrary", "arbitrary"), name="diff_prompt",
    )(q, k, v, lam_par, subln.reshape(1, D_VDIM))


def _page_specs(rows, layer, group, n_pages):
    def spec(g):
        def imap(b, j, pt):
            return (layer, pt[b, jnp.minimum(j * group + g, n_pages - 1)], 0, 0)
        return pl.BlockSpec((None, None, rows, PAGE_SIZE), imap)
    return [spec(g) for g in range(group)]


def _cat_lanes(refs):
    return jnp.concatenate([r[...] for r in refs], axis=1)


def _new_key_valid(nq, nk):
    rq = _iota((nq, nk), 0) % SLOT
    rk = _iota((nq, nk), 1)
    return (rk >= SLOT_LO) & (rk < SLOT_LO + (SLOT - 2 * SLOT_LO)) & (rk <= rq)


def _idx_scores(qi4, w, keys_t_bf16):
    lg = _dot(qi4, keys_t_bf16)
    s = jnp.zeros((SLOT, keys_t_bf16.shape[1]), F32)
    for h in range(IDX_HEADS):
        s = s + jnp.maximum(lg[h * SLOT:(h + 1) * SLOT], 0.0) * w[:, h:h + 1]
    return jnp.where(s == 0.0, 0.0, s)


def _dsa_sample_scores_body(pt_ref, qi_ref, kiw_ref, knew_ref, *rest, group):
    pages, o_ref = rest[:group], rest[group]
    j = pl.program_id(1)
    last = pl.num_programs(1) - 1
    qi = qi_ref[0] * IDX_DIM ** -0.5
    qi4 = jnp.concatenate([qi[:, h * IDX_DIM:(h + 1) * IDX_DIM] for h in range(IDX_HEADS)], axis=0).astype(BF16)
    w = kiw_ref[0][:, IDX_DIM:IDX_DIM + IDX_HEADS] * IDX_HEADS ** -0.5

    @pl.when(j < last)
    def _():
        o_ref[0, 0] = _idx_scores(qi4, w, _cat_lanes(pages).astype(BF16))

    @pl.when(j == last)
    def _():
        s = _idx_scores(qi4, w, knew_ref[0].astype(BF16))
        s = jnp.where(_new_key_valid(SLOT, PAGE_SIZE), s, -jnp.inf)
        pad = jnp.full((SLOT, (group - 1) * PAGE_SIZE), -jnp.inf, F32)
        o_ref[0, 0] = jnp.concatenate([s, pad], axis=1) if group > 1 else s


def _dsa_sample_scores(page_table, qi, kiw, knew_t, cache_idx_t, layer, group):
    db, n_pages = page_table.shape
    nch = n_pages // group + 1
    gw = group * PAGE_SIZE
    grid_spec = pltpu.PrefetchScalarGridSpec(
        num_scalar_prefetch=1, grid=(db, nch),
        in_specs=[pl.BlockSpec((1, SLOT, qi.shape[2]), lambda b, j, pt: (b, 0, 0)),
                  pl.BlockSpec((1, SLOT, LANES), lambda b, j, pt: (b, 0, 0)),
                  pl.BlockSpec((1, IDX_DIM, PAGE_SIZE), lambda b, j, pt: (b, 0, 0))]
                 + _page_specs(IDX_DIM, layer, group, n_pages),
        out_specs=pl.BlockSpec((1, 1, SLOT, gw), lambda b, j, pt: (b, j, 0, 0)))
    return pl.pallas_call(
        functools.partial(_dsa_sample_scores_body, group=group), grid_spec=grid_spec,
        out_shape=jax.ShapeDtypeStruct((db, nch, SLOT, gw), F32),
        compiler_params=_cp("arbitrary", "arbitrary"), name="dsa_sample_scores",
    )(page_table, qi, kiw, knew_t, *([cache_idx_t] * group))


def _dsa_sample_attend_body(pt_ref, qa_ref, sc_ref, kvnew_ref, tri_ref, *rest, group, topk):
    pages, o_ref = rest[:group], rest[group]
    t_sc, need_sc, carry_sc, m_sc, l_sc, acc_sc, ties_sm = rest[group + 1:]
    j = pl.program_id(1)
    last = pl.num_programs(1) - 1
    hpg = A_HEADS // A_KV_HEADS

    @pl.when(j == 0)
    def _():
        x = sc_ref[0]

        def stats(p):
            ge = x >= p[None]
            cnt = jnp.sum(jnp.sum(jnp.where(ge, 1.0, 0.0), axis=0), axis=1, keepdims=True)
            mnge = jnp.min(jnp.min(jnp.where(ge, x, jnp.inf), axis=0), axis=1, keepdims=True)
            mxlt = jnp.max(jnp.max(jnp.where(ge, -jnp.inf, x), axis=0), axis=1, keepdims=True)
            return cnt, mnge, mxlt

        finite = x > -jnp.inf
        cmax0 = jnp.max(jnp.max(x, axis=0), axis=1, keepdims=True)
        cmin0 = jnp.min(jnp.min(jnp.where(finite, x, jnp.inf), axis=0), axis=1, keepdims=True)
        t = _kth_largest(stats, cmin0, cmax0, float(topk))
        n_gt = jnp.sum(jnp.sum(jnp.where(x > t[None], 1.0, 0.0), axis=0), axis=1, keepdims=True)
        n_ge = jnp.sum(jnp.sum(jnp.where(x >= t[None], 1.0, 0.0), axis=0), axis=1, keepdims=True)
        ties_sm[0] = (jnp.max(n_ge) > float(topk)).astype(jnp.int32)
        t_sc[...] = jnp.broadcast_to(t, t_sc.shape)
        need_sc[...] = jnp.broadcast_to(float(topk) - n_gt, need_sc.shape)
        carry_sc[...] = jnp.zeros(carry_sc.shape, F32)
        m_sc[...] = jnp.full(m_sc.shape, NEG, F32)
        l_sc[...] = jnp.zeros(l_sc.shape, F32)
        acc_sc[...] = jnp.zeros(acc_sc.shape, F32)

    qa = qa_ref[0] * HEAD_DIM ** -0.5
    qg = [jnp.concatenate([qa[:, (g * hpg + h) * HEAD_DIM:(g * hpg + h + 1) * HEAD_DIM] for h in range(hpg)],
                          axis=0).astype(BF16) for g in range(A_KV_HEADS)]
    t = t_sc[...][:, :1]
    need = need_sc[...][:, :1]
    tri = tri_ref[...]

    def tie_bias(x):
        carry = carry_sc[...][:, :1]
        biases = []
        for xb in _lane_blocks(x):
            bias, carry = _tie_bias(xb, t, need, carry, tri)
            biases.append(bias)
        carry_sc[...] = jnp.broadcast_to(carry, carry_sc.shape)
        return jnp.concatenate(biases, axis=1) if len(biases) > 1 else biases[0]

    def attend(x, kv_t):
        bias = lax.cond(ties_sm[0] > 0, tie_bias, lambda xs: jnp.where(xs >= t, 0.0, NEG), x)
        bias4 = jnp.concatenate([bias] * hpg, axis=0)
        for g in range(A_KV_HEADS):
            kc = kv_t[g * HEAD_DIM:(g + 1) * HEAD_DIM].astype(BF16)
            vc = kv_t[(A_KV_HEADS + g) * HEAD_DIM:(A_KV_HEADS + g + 1) * HEAD_DIM].astype(BF16)
            _softmax_step(_dot(qg[g], kc) + bias4, vc, m_sc, l_sc, acc_sc, g, values_transposed=True)

    @pl.when(j < last)
    def _():
        attend(sc_ref[0, j], _cat_lanes(pages))

    @pl.when(j == last)
    def _():
        attend(sc_ref[0, j][:, :PAGE_SIZE], kvnew_ref[0])
        outs = []
        for g in range(A_KV_HEADS):
            og = acc_sc[g] / l_sc[g][:, :HEAD_DIM]
            outs += [og[h * SLOT:(h + 1) * SLOT] for h in range(hpg)]
        o_ref[0] = jnp.concatenate(outs, axis=1)


def _dsa_sample_attend(page_table, qa, scores, kvnew, cache_kv, layer, group, topk):
    db, n_pages = page_table.shape
    nch = n_pages // group + 1
    gw = group * PAGE_SIZE
    hpg = A_HEADS // A_KV_HEADS
    kvw = 2 * A_KV_HEADS * HEAD_DIM
    grid_spec = pltpu.PrefetchScalarGridSpec(
        num_scalar_prefetch=1, grid=(db, nch),
        in_specs=[pl.BlockSpec((1, SLOT, qa.shape[2]), lambda b, j, pt: (b, 0, 0)),
                  pl.BlockSpec((1, nch, SLOT, gw), lambda b, j, pt: (b, 0, 0, 0)),
                  pl.BlockSpec((1, kvw, PAGE_SIZE), lambda b, j, pt: (b, 0, 0)),
                  pl.BlockSpec((PAGE_SIZE, PAGE_SIZE), lambda b, j, pt: (0, 0))]
                 + _page_specs(kvw, layer, group, n_pages),
        out_specs=pl.BlockSpec((1, SLOT, A_HEADS * HEAD_DIM), lambda b, j, pt: (b, 0, 0)),
        scratch_shapes=[pltpu.VMEM((SLOT, LANES), F32), pltpu.VMEM((SLOT, LANES), F32), pltpu.VMEM((SLOT, LANES), F32),
                        pltpu.VMEM((A_KV_HEADS, hpg * SLOT, LANES), F32),
                        pltpu.VMEM((A_KV_HEADS, hpg * SLOT, LANES), F32),
                        pltpu.VMEM((A_KV_HEADS, hpg * SLOT, HEAD_DIM), F32),
                        pltpu.SMEM((1,), jnp.int32)])
    return pl.pallas_call(
        functools.partial(_dsa_sample_attend_body, group=group, topk=topk), grid_spec=grid_spec,
        out_shape=jax.ShapeDtypeStruct((db, SLOT, A_HEADS * HEAD_DIM), F32),
        compiler_params=_cp("arbitrary", "arbitrary"), name="dsa_sample_attend",
    )(page_table, qa, scores, kvnew, _upper_tri_bf16(PAGE_SIZE), *([cache_kv] * group))


def _diff_sample_body(pt_ref, q_ref, knew_ref, vnew_ref, lam_ref, sub_ref, *rest, group, lam_init):
    kpages, vpages, o_ref = rest[:group], rest[group:2 * group], rest[2 * group]
    m_sc, l_sc, acc_sc = rest[2 * group + 1:]
    j = pl.program_id(1)
    last = pl.num_programs(1) - 1
    nmap = 2 * D_HEADS
    w = nmap * HEAD_DIM

    @pl.when(j == 0)
    def _():
        m_sc[...] = jnp.full(m_sc.shape, NEG, F32)
        l_sc[...] = jnp.zeros(l_sc.shape, F32)
        acc_sc[...] = jnp.zeros(acc_sc.shape, F32)

    q = q_ref[0] * HEAD_DIM ** -0.5
    lane_map = _iota((SLOT, w), 1) // HEAD_DIM
    qb = jnp.concatenate([jnp.where(lane_map == m * D_HEADS + h, q, 0.0)
                          for h in range(D_HEADS) for m in range(2)], axis=0).astype(BF16)
    rows_h = 2 * SLOT

    def step(k_t, v_refs, mask):
        s = _dot(qb, k_t.astype(BF16))
        if mask is not None:
            s = jnp.where(mask, s, NEG)
        p, alpha, m_new, l_new = _softmax_probs(_lane_blocks(s), m_sc[...], l_sc[...])
        m_sc[...] = m_new
        l_sc[...] = l_new
        for h in range(D_HEADS):
            vh = jnp.concatenate([r[pl.ds(h, PAGE_SIZE, stride=D_HEADS), :] for r in v_refs], axis=0).astype(BF16)
            rs = slice(h * rows_h, (h + 1) * rows_h)
            acc_sc[h] = alpha[rs] * acc_sc[h] + _dot(p[rs], vh)

    @pl.when(j < last)
    def _():
        step(_cat_lanes(kpages), vpages, None)

    @pl.when(j == last)
    def _():
        step(knew_ref[0], [vnew_ref.at[0]], _new_key_valid(nmap * SLOT, PAGE_SIZE))
        lam = _diff_lambda(lam_ref, lam_init)
        l = l_sc[...]
        outs = []
        for h in range(D_HEADS):
            acc = acc_sc[h]
            lh = l[h * rows_h:(h + 1) * rows_h]
            outs.append(_diff_finish(acc[:SLOT], lh[:SLOT], acc[SLOT:], lh[SLOT:], lam, sub_ref[...], lam_init))
        o_ref[0] = jnp.concatenate(outs, axis=1)


def _diff_sample(page_table, q, knew_t, vnew, lam_par, subln, cache_k_t, cache_v, layer, group, lam_init):
    db, n_pages = page_table.shape
    nch = n_pages // group + 1
    nmap = 2 * D_HEADS
    w = q.shape[2]
    vw = D_HEADS * D_VDIM
    vrows = PAGE_SIZE * D_HEADS
    grid_spec = pltpu.PrefetchScalarGridSpec(
        num_scalar_prefetch=1, grid=(db, nch),
        in_specs=[pl.BlockSpec((1, SLOT, w), lambda b, j, pt: (b, 0, 0)),
                  pl.BlockSpec((1, w, PAGE_SIZE), lambda b, j, pt: (b, 0, 0)),
                  pl.BlockSpec((1, vrows, D_VDIM), lambda b, j, pt: (b, 0, 0)),
                  pl.BlockSpec(lam_par.shape, lambda b, j, pt: (0, 0)),
                  pl.BlockSpec((1, D_VDIM), lambda b, j, pt: (0, 0))]
                 + _page_specs(w, layer, group, n_pages) + _page_specs(vrows, layer, group, n_pages),
        out_specs=pl.BlockSpec((1, SLOT, vw), lambda b, j, pt: (b, 0, 0)),
        scratch_shapes=[pltpu.VMEM((nmap * SLOT, LANES), F32), pltpu.VMEM((nmap * SLOT, LANES), F32),
                        pltpu.VMEM((D_HEADS, 2 * SLOT, D_VDIM), F32)])
    return pl.pallas_call(
        functools.partial(_diff_sample_body, group=group, lam_init=lam_init), grid_spec=grid_spec,
        out_shape=jax.ShapeDtypeStruct((db, SLOT, vw), F32),
        compiler_params=_cp("arbitrary", "arbitrary"), name="diff_sample",
    )(page_table, q, knew_t, vnew, lam_par, subln.reshape(1, D_VDIM), *([cache_k_t] * group), *([cache_v] * group))


def _split_cols(w, sizes):
    out, o = [], 0
    for s in sizes:
        out.append(w[:, o:o + s])
        o += s
    return out


def _pad_cols(w, n):
    return jnp.pad(w, ((0, 0), (0, n - w.shape[1])))


def _pad_rows(a, n):
    return jnp.pad(a, ((0, 0), (0, n - a.shape[1]), (0, 0)))


def _keys_t(a):
    return jnp.pad(jnp.swapaxes(a, 1, 2), ((0, 0), (0, 0), (0, PAGE_SIZE - a.shape[1])))


def _pos_minor(cache):
    nd = cache.ndim
    t = jnp.transpose(cache, (0, 1) + tuple(range(3, nd)) + (2,))
    return t.reshape(cache.shape[0], cache.shape[1], -1, cache.shape[2])


def _ab_weights(w_in):
    hd = HEAD_DIM
    aq, ak, av, iq, ik, iw, bq, bk, bv, bg = _split_cols(
        w_in, (A_HEADS * hd, A_KV_HEADS * hd, A_KV_HEADS * hd, IDX_HEADS * IDX_DIM, IDX_DIM, IDX_HEADS,
               B_HEADS * hd, B_HEADS * hd, B_HEADS * hd, B_HEADS * hd))
    segs = [aq, jnp.concatenate([ak, av], axis=1), iq, _pad_cols(jnp.concatenate([ik, iw], axis=1), LANES), bq, bk, bv, bg]
    return [s.astype(BF16) for s in segs]


def _cd_weights(w_in):
    hd = HEAD_DIM
    cq, ck, cv, ci, cf, co, dq, dk, dv = _split_cols(
        w_in, (C_HEADS * hd, C_HEADS * hd, C_HEADS * hd, C_HEADS, C_HEADS, C_HEADS * hd,
               2 * D_HEADS * hd, 2 * D_HEADS * hd, D_HEADS * D_VDIM))
    segs = [cq, ck, cv, co, _pad_cols(jnp.concatenate([ci, cf], axis=1), LANES), dq, dk, dv]
    return [s.astype(BF16) for s in segs]


def kernel(x_prompt, x_sample, c_prompt, c_sample, page_table, cache_a_kv, cache_a_idx, cache_d_k, cache_d_v, state_ret, state_mlstm_c, state_mlstm_n, state_mlstm_m, state_ffn_conv, norm_pre_mix, norm_post_mix, norm_pre_ffn, norm_post_ffn, w_ada, b_ada, w_ab_in, w_ab_out, ret_gain, w_cd_in, w_cd_out, c_gate_bias, c_gain, d_lambda, d_subln, ffn_w_up, ffn_conv_w, ffn_conv_b, ffn_w_down):
    b, s, d = x_prompt.shape
    db, t, _ = x_sample.shape
    depth = w_ada.shape[0]
    n_pages = page_table.shape[1]
    past = n_pages * PAGE_SIZE
    d_ff = ffn_w_down.shape[1]
    assert t == SLOT - 2 * SLOT_LO and past >= TOPK_MAX and s % 1024 == 0
    hi = SLOT_LO + t
    ms = db * SLOT
    tm_in, tm_ffn, tf = 512, 1024, 256
    group = 16
    topk_s = min(TOPK_MAX, (past + t) // 4)

    n_c = b + db
    c_all = jnp.pad(jnp.concatenate([c_prompt, c_sample], axis=0), ((0, -n_c % 8), (0, 0)))
    mods = _modulation(c_all, w_ada, b_ada)
    w_up, w_down = ffn_w_up.astype(BF16), ffn_w_down.astype(BF16)
    w_ab_in, w_ab_out = w_ab_in.astype(BF16), w_ab_out.astype(BF16)
    w_cd_in, w_cd_out = w_cd_in.astype(BF16), w_cd_out.astype(BF16)

    xp = x_prompt.reshape(b * s, d)
    xs = jnp.pad(x_sample, ((0, 0), (SLOT_LO, SLOT - hi), (0, 0))).reshape(ms, d)

    pos_s = past + (np.arange(ms) % SLOT) - SLOT_LO
    tp_p, tf_p = _rope_tables(np.arange(s), HEAD_DIM // 4, ROPE_THETA), _rope_tables(np.arange(s), HEAD_DIM, RET_THETA)
    tp_s, tf_s = _rope_tables(pos_s, HEAD_DIM // 4, ROPE_THETA), _rope_tables(pos_s, HEAD_DIM, RET_THETA)

    cache_a_kv = _pos_minor(cache_a_kv)
    cache_a_idx = _pos_minor(cache_a_idx)
    cache_d_k = _pos_minor(cache_d_k)
    cache_d_v = cache_d_v.reshape(*cache_d_v.shape[:2], -1, D_VDIM)

    hd = HEAD_DIM
    w512 = 8 * hd
    ab_defs = lambda dt: [(w512, dt), (4 * hd, F32), (4 * hd, dt), (LANES, F32), (w512, dt), (w512, F32), (w512, dt), (w512, F32)]
    cd_defs = lambda dt: [(w512, dt), (w512, F32), (w512, dt), (w512, F32), (LANES, F32), (w512, dt), (w512, F32), (w512, F32)]
    zeros = lambda *shape: jnp.zeros(shape, F32)

    outs = {k: [] for k in ("pa_kv", "pa_idx", "pd_k", "pd_v", "p_ret", "p_mc", "p_mn", "p_mm", "p_cv",
                            "sa_kv", "sa_idx", "sd_k", "sd_v", "s_ret", "s_mc", "s_mn", "s_mm", "s_cv")}
    for l in range(depth):
        p = l // 2
        m6 = mods[l, :n_c].reshape(n_c, 6, d)
        mp = [m6[:b, i][:, None, :] for i in range(6)]
        msm = [jnp.repeat(m6[b:, i], SLOT, axis=0)[None] for i in range(6)]
        if l % 2 == 0:
            wts = _ab_weights(w_ab_in[p])
            wo = w_ab_out[p].astype(BF16)
            qa, kv, qi, kiw, bq, bk, bv, bg = _in_proj(_ab_in_body, "ab_in", xp, norm_pre_mix[l], mp[0], mp[1],
                                                       [tp_p, tf_p], [], wts, ab_defs(BF16), tm_in, s // tm_in)
            r3 = lambda a: a.reshape(b, s, a.shape[-1])
            a_out = _dsa_prompt(r3(qa), r3(qi), r3(kiw), r3(kv))
            b_out, st = _retention(r3(bq), r3(bk), r3(bv), r3(bg), zeros(b, B_HEADS, hd, hd), ret_gain[p],
                                   CHUNK, 0, CHUNK, BF16)
            outs["pa_kv"].append(kv.reshape(b, s, 2, A_KV_HEADS, hd))
            outs["pa_idx"].append(r3(kiw)[:, :, :IDX_DIM])
            outs["p_ret"].append(st)
            xp = _out_proj(xp, a_out.reshape(b * s, -1), b_out.reshape(b * s, -1), wo[:w512], wo[w512:],
                           norm_post_mix[l], mp[2], tm_in, s // tm_in)
            qa, kv, qi, kiw, bq, bk, bv, bg = _in_proj(_ab_in_body, "ab_in", xs, norm_pre_mix[l], msm[0], msm[1],
                                                       [tp_s, tf_s], [], wts, ab_defs(F32), ms, 1)
            r3 = lambda a: a.reshape(db, SLOT, a.shape[-1])
            scores = _dsa_sample_scores(page_table, r3(qi), r3(kiw), _keys_t(r3(kiw)[:, :, :IDX_DIM]), cache_a_idx, p, group)
            a_out = _dsa_sample_attend(page_table, r3(qa), scores, _keys_t(r3(kv)), cache_a_kv, p, group, topk_s)
            b_out, st = _retention(r3(bq), r3(bk), r3(bv), r3(bg), state_ret[p], ret_gain[p], SLOT, SLOT_LO, hi, F32)
            outs["sa_kv"].append(r3(kv)[:, SLOT_LO:hi].reshape(db, t, 2, A_KV_HEADS, hd))
            outs["sa_idx"].append(r3(kiw)[:, SLOT_LO:hi, :IDX_DIM])
            outs["s_ret"].append(st)
            xs = _out_proj(xs, a_out.reshape(ms, -1), b_out.reshape(ms, -1), wo[:w512], wo[w512:],
                           norm_post_mix[l], msm[2], ms, 1)
        else:
            lam_init = 0.8 - 0.6 * math.exp(-0.3 * l)
            wts = _cd_weights(w_cd_in[p])
            wo = w_cd_out[p].astype(BF16)
            gbias = jnp.pad(c_gate_bias[p], (0, LANES - 2 * C_HEADS)).reshape(1, LANES)
            cq, ck, cv, co, gt, dq, dk, dv = _in_proj(_cd_in_body, "cd_in", xp, norm_pre_mix[l], mp[0], mp[1],
                                                      [tp_p], [gbias], wts, cd_defs(BF16), tm_in, s // tm_in)
            r3 = lambda a: a.reshape(b, s, a.shape[-1])
            c_out, mc, mn, mm = _mlstm(r3(cq), r3(ck), r3(cv), r3(co), r3(gt), zeros(b, C_HEADS, hd, hd),
                                       zeros(b, C_HEADS, hd), zeros(b, C_HEADS), c_gain[p], CHUNK, 0, CHUNK, BF16)
            d_out = _diff_prompt(r3(dq), r3(dk), r3(dv), d_lambda[p], d_subln[p], lam_init)
            outs["pd_k"].append(dk.reshape(b, s, 2, D_HEADS, hd))
            outs["pd_v"].append(dv.reshape(b, s, D_HEADS, D_VDIM))
            outs["p_mc"].append(mc); outs["p_mn"].append(mn); outs["p_mm"].append(mm)
            xp = _out_proj(xp, c_out.reshape(b * s, -1), d_out.reshape(b * s, -1), wo[:w512], wo[w512:],
                           norm_post_mix[l], mp[2], tm_in, s // tm_in)
            cq, ck, cv, co, gt, dq, dk, dv = _in_proj(_cd_in_body, "cd_in", xs, norm_pre_mix[l], msm[0], msm[1],
                                                      [tp_s], [gbias], wts, cd_defs(F32), ms, 1)
            r3 = lambda a: a.reshape(db, SLOT, a.shape[-1])
            c_out, mc, mn, mm = _mlstm(r3(cq), r3(ck), r3(cv), r3(co), r3(gt), state_mlstm_c[p], state_mlstm_n[p],
                                       state_mlstm_m[p], c_gain[p], SLOT, SLOT_LO, hi, F32)
            vnew = _pad_rows(dv.reshape(db, SLOT * D_HEADS, D_VDIM), PAGE_SIZE * D_HEADS)
            d_out = _diff_sample(page_table, r3(dq), _keys_t(r3(dk)), vnew,
                                 d_lambda[p], d_subln[p], cache_d_k, cache_d_v, p, group, lam_init)
            outs["sd_k"].append(r3(dk)[:, SLOT_LO:hi].reshape(db, t, 2, D_HEADS, hd))
            outs["sd_v"].append(r3(dv)[:, SLOT_LO:hi].reshape(db, t, D_HEADS, D_VDIM))
            outs["s_mc"].append(mc); outs["s_mn"].append(mn); outs["s_mm"].append(mm)
            xs = _out_proj(xs, c_out.reshape(ms, -1), d_out.reshape(ms, -1), wo[:w512], wo[w512:],
                           norm_post_mix[l], msm[2], ms, 1)

        xp, tails = _ffn(xp, norm_pre_ffn[l], mp[3], mp[4], w_up, w_down, l, ffn_conv_w[l], ffn_conv_b[l], None,
                         norm_post_ffn[l], mp[5], tm_ffn, s // tm_ffn, tf)
        outs["p_cv"].append(tails.reshape(b, s // tm_ffn, 2, d_ff)[:, -1])
        inj = jnp.pad(state_ffn_conv[l], ((0, 0), (0, SLOT - SLOT_LO), (0, 0))).reshape(ms, d_ff)
        xs, a_all = _ffn(xs, norm_pre_ffn[l], msm[3], msm[4], w_up, w_down, l, ffn_conv_w[l], ffn_conv_b[l], inj,
                         norm_post_ffn[l], msm[5], ms, 1, tf)
        outs["s_cv"].append(a_all.reshape(db, SLOT, d_ff)[:, hi - (CONV_W - 1):hi])

    st = lambda k: jnp.stack(outs[k])
    y_sample = xs.reshape(db, SLOT, d)[:, SLOT_LO:hi]
    return (xp.reshape(b, s, d), y_sample,
            st("pa_kv"), st("pa_idx"), st("pd_k"), st("pd_v"), st("p_ret"), st("p_mc"), st("p_mn"), st("p_mm"), st("p_cv"),
            st("sa_kv"), st("sa_idx"), st("sd_k"), st("sd_v"), st("s_ret"), st("s_mc"), st("s_mn"), st("s_mm"), st("s_cv"))
```

```python
import functools
import math

import numpy as np
import jax
import jax.numpy as jnp
from jax import lax
from jax.experimental import pallas as pl
from jax.experimental.pallas import tpu as pltpu

F32 = jnp.float32
BF16 = jnp.bfloat16

PAGE_SIZE = 128
HEAD_DIM = 64
ROPE_THETA = 500000.0
RET_THETA = 10000.0
A_HEADS = 8
A_KV_HEADS = 2
IDX_HEADS = 4
IDX_DIM = 64
TOPK_MAX = 256
B_HEADS = 8
C_HEADS = 8
D_HEADS = 4
D_VDIM = 2 * HEAD_DIM
CONV_W = 3
CHUNK = 128
EPS = 1e-6
NEG = -1e30
LOG2E = 1.4426950408889634
SLOT = 8
SLOT_LO = CONV_W - 1
LANES = 128
VMEM_LIMIT = 56 * 1024 * 1024

_NT = (((1,), (1,)), ((), ()))


def _cp(*sem):
    return pltpu.CompilerParams(dimension_semantics=sem, vmem_limit_bytes=VMEM_LIMIT)


def _dot(a, b):
    return jnp.dot(a, b, preferred_element_type=F32)


def _dot_nt(a, b):
    return lax.dot_general(a, b, _NT, preferred_element_type=F32)


def _iota(shape, dim):
    return lax.broadcasted_iota(jnp.int32, shape, dim)


def _eye_bf16(n):
    return jnp.where(_iota((n, n), 0) == _iota((n, n), 1), 1.0, 0.0).astype(BF16)


def _dot_tn(a_bf16, b_bf16):
    at = _dot_nt(_eye_bf16(a_bf16.shape[1]), a_bf16).astype(BF16)
    return _dot(at, b_bf16)


def _split3(x):
    hi = x.astype(BF16)
    r1 = x - hi.astype(F32)
    mid = r1.astype(BF16)
    lo = (r1 - mid.astype(F32)).astype(BF16)
    return hi, mid, lo


def _rope_tables(pos, rot, theta):
    half = rot // 2
    inv = theta ** (-np.arange(half, dtype=np.float64) / half)
    ang = np.asarray(pos, np.float64)[:, None] * inv[None]
    cos, sin = np.cos(ang), np.sin(ang)
    n = len(pos)
    t0 = np.ones((n, HEAD_DIM)); t1 = np.zeros((n, HEAD_DIM)); t2 = np.zeros((n, HEAD_DIM))
    t0[:, :half] = cos; t0[:, half:rot] = cos
    t1[:, :half] = -sin
    t2[:, half:rot] = sin
    tab = np.stack([np.tile(t, (1, LANES // HEAD_DIM)) for t in (t0, t1, t2)])
    return jnp.asarray(tab, F32)


def _retention_tables(chunk, lo, hi):
    n = hi - lo
    log_g = np.log1p(-np.exp2(-5.0 - np.arange(B_HEADS, dtype=np.float64)))
    r = np.arange(chunk)
    ok = (r >= lo) & (r < hi)
    pos = (r - lo).astype(np.float64)
    diff = pos[:, None] - pos[None, :]
    dm = np.where((diff >= 0) & ok[:, None] & ok[None, :], np.exp(log_g[:, None, None] * np.maximum(diff, 0.0)), 0.0)
    qd = np.exp(log_g[:, None] * (pos + 1.0))[:, :, None] * np.ones((1, 1, HEAD_DIM))
    kd = np.where(ok, np.exp(log_g[:, None] * (n - 1.0 - pos)), 0.0)[:, :, None] * np.ones((1, 1, HEAD_DIM))
    cd = np.exp(log_g * n)[:, None, None] * np.ones((1, 1, HEAD_DIM))
    return tuple(jnp.asarray(a, F32) for a in (dm, qd, kd, cd))


def _mod_body(c_ref, w_ref, b_ref, o_ref):
    c = c_ref[...]
    s = (c * jax.nn.sigmoid(c)).astype(BF16)
    o_ref[0] = _dot(s, w_ref[0].astype(BF16)) + b_ref[0]


def _modulation(c_all, w_ada, b_ada):
    depth, d, n = w_ada.shape
    r = c_all.shape[0]
    tn = 1536
    return pl.pallas_call(
        _mod_body, grid=(depth, n // tn),
        in_specs=[pl.BlockSpec((r, d), lambda l, j: (0, 0)),
                  pl.BlockSpec((1, d, tn), lambda l, j: (l, 0, j)),
                  pl.BlockSpec((1, 1, tn), lambda l, j: (l, 0, j))],
        out_specs=pl.BlockSpec((1, r, tn), lambda l, j: (l, 0, j)),
        out_shape=jax.ShapeDtypeStruct((depth, r, n), F32),
        compiler_params=_cp("arbitrary", "arbitrary"), name="modulation",
    )(c_all, w_ada, b_ada.reshape(depth, 1, n))


def _ada_norm_bf16(x_ref, g_ref, sh_ref, sc_ref):
    x = x_ref[...]
    xn = x * lax.rsqrt(jnp.mean(x * x, axis=-1, keepdims=True) + EPS) * g_ref[...]
    return (xn * (1.0 + sc_ref[0]) + sh_ref[0]).astype(BF16)


def _rope128(x, tab, shift):
    xl = pltpu.roll(x, LANES - shift, 1)
    xr = pltpu.roll(x, shift, 1)
    return x * tab[0] + xl * tab[1] + xr * tab[2]


def _rope(u, tab, shift):
    return jnp.concatenate(
        [_rope128(u[:, c * LANES:(c + 1) * LANES], tab, shift) for c in range(u.shape[1] // LANES)], axis=1)


def _ab_in_body(x_ref, g_ref, sh_ref, sc_ref, tp_ref, tf_ref,
                wqa, wkv, wqi, wki, wbq, wbk, wbv, wbg,
                qa_o, kv_o, qi_o, kiw_o, bq_o, bk_o, bv_o, bg_o):
    h = _ada_norm_bf16(x_ref, g_ref, sh_ref, sc_ref)
    tp = tp_ref[...]
    tf = tf_ref[...]
    pshift, fshift = HEAD_DIM // 8, HEAD_DIM // 2
    qa_o[...] = _rope(_dot(h, wqa[...]), tp, pshift).astype(qa_o.dtype)
    u = _dot(h, wkv[...])
    kv_o[...] = jnp.concatenate([_rope128(u[:, :LANES], tp, pshift), u[:, LANES:]], axis=1)
    qi_o[...] = _rope(_dot(h, wqi[...]), tp, pshift).astype(qi_o.dtype)
    u = _dot(h, wki[...])
    kiw_o[...] = jnp.where(_iota(u.shape, 1) < IDX_DIM, _rope128(u, tp, pshift), u)
    bq_o[...] = _rope(_dot(h, wbq[...]), tf, fshift).astype(bq_o.dtype)
    bk_o[...] = _rope(_dot(h, wbk[...]), tf, fshift) * HEAD_DIM ** -0.5
    bv_o[...] = _dot(h, wbv[...]).astype(bv_o.dtype)
    bg_o[...] = _dot(h, wbg[...])


def _cd_in_body(x_ref, g_ref, sh_ref, sc_ref, tp_ref, gb_ref,
                wcq, wck, wcv, wco, wgt, wdq, wdk, wdv,
                cq_o, ck_o, cv_o, co_o, gt_o, dq_o, dk_o, dv_o):
    h = _ada_norm_bf16(x_ref, g_ref, sh_ref, sc_ref)
    tp = tp_ref[...]
    pshift = HEAD_DIM // 8
    cq_o[...] = _dot(h, wcq[...]).astype(cq_o.dtype)
    ck_o[...] = _dot(h, wck[...]) * HEAD_DIM ** -0.5
    cv_o[...] = _dot(h, wcv[...]).astype(cv_o.dtype)
    co_o[...] = _dot(h, wco[...])
    gt_o[...] = _dot(h, wgt[...]) + gb_ref[...]
    dq_o[...] = _rope(_dot(h, wdq[...]), tp, pshift).astype(dq_o.dtype)
    dk_o[...] = _rope(_dot(h, wdk[...]), tp, pshift)
    dv_o[...] = _dot(h, wdv[...])


def _in_proj(body, name, x, g, shift, scale, tables, extra, weights, out_defs, tm, tps):
    m, d = x.shape
    r = shift.shape[1]
    mod_spec = pl.BlockSpec((1, r, d), lambda i: (i // tps, 0, 0))
    in_specs = [pl.BlockSpec((tm, d), lambda i: (i, 0)), pl.BlockSpec((1, d), lambda i: (0, 0)), mod_spec, mod_spec]
    in_specs += [pl.BlockSpec((3, tm, LANES), lambda i: (0, i % tps, 0)) for _ in tables]
    in_specs += [pl.BlockSpec(e.shape, lambda i: (0, 0)) for e in extra]
    in_specs += [pl.BlockSpec(w.shape, lambda i: (0, 0)) for w in weights]
    return pl.pallas_call(
        body, grid=(m // tm,), in_specs=in_specs,
        out_specs=[pl.BlockSpec((tm, n), lambda i: (i, 0)) for n, _ in out_defs],
        out_shape=[jax.ShapeDtypeStruct((m, n), dt) for n, dt in out_defs],
        compiler_params=_cp("arbitrary"), name=name,
    )(x, g.reshape(1, d), shift, scale, *tables, *extra, *weights)


def _out_proj_body(x_ref, a_ref, b_ref, wa_ref, wb_ref, g_ref, gate_ref, o_ref):
    y = _dot(a_ref[...].astype(BF16), wa_ref[...]) + _dot(b_ref[...].astype(BF16), wb_ref[...])
    yn = y * lax.rsqrt(jnp.mean(y * y, axis=-1, keepdims=True) + EPS) * g_ref[...]
    o_ref[...] = x_ref[...] + gate_ref[0] * yn


def _out_proj(x, a, b, wa, wb, g, gate, tm, tps):
    m, d = x.shape
    r = gate.shape[1]
    return pl.pallas_call(
        _out_proj_body, grid=(m // tm,),
        in_specs=[pl.BlockSpec((tm, d), lambda i: (i, 0)),
                  pl.BlockSpec((tm, a.shape[1]), lambda i: (i, 0)),
                  pl.BlockSpec((tm, b.shape[1]), lambda i: (i, 0)),
                  pl.BlockSpec(wa.shape, lambda i: (0, 0)), pl.BlockSpec(wb.shape, lambda i: (0, 0)),
                  pl.BlockSpec((1, d), lambda i: (0, 0)),
                  pl.BlockSpec((1, r, d), lambda i: (i // tps, 0, 0))],
        out_specs=pl.BlockSpec((tm, d), lambda i: (i, 0)),
        out_shape=jax.ShapeDtypeStruct((m, d), F32),
        compiler_params=_cp("arbitrary"), name="out_proj",
    )(x, a, b, wa, wb, g.reshape(1, d), gate)


def _ffn_body(x_ref, g1_ref, sh_ref, sc_ref, wu_ref, wd_ref, cw_ref, cb_ref, inj_ref, g2_ref, gate_ref,
              o_ref, aux_ref, carry_sc, *, tps, tf, inject):
    i = pl.program_id(0)
    tm = x_ref.shape[0]
    ff = wd_ref.shape[0]
    h = _ada_norm_bf16(x_ref, g1_ref, sh_ref, sc_ref)
    row = _iota((tm, tf), 0)
    cw = cw_ref[...]
    cb = cb_ref[...]
    if not inject:
        @pl.when(i % tps == 0)
        def _():
            carry_sc[...] = jnp.zeros(carry_sc.shape, F32)

    yv = None
    for f in range(ff // tf):
        fs = slice(f * tf, (f + 1) * tf)
        a = _dot(h, wu_ref[:, fs])
        b = _dot(h, wu_ref[:, ff + f * tf:ff + (f + 1) * tf])
        if inject:
            a = jnp.where(row % SLOT < SLOT_LO, inj_ref[:, fs], a)
            aux_ref[:, fs] = a
            p1 = jnp.zeros((1, tf), F32)
            p2 = jnp.zeros((2, tf), F32)
        else:
            p2 = carry_sc[:, fs]
            carry_sc[:, fs] = a[tm - 2:, :]
            aux_ref[0, :, fs] = a[tm - 2:, :]
            p1 = p2[1:2, :]
        a1 = jnp.where(row == 0, p1, pltpu.roll(a, 1, 0))
        a2 = pltpu.roll(a, 2, 0)
        a2 = jnp.where(row == 0, p2[0:1, :], jnp.where(row == 1, p2[1:2, :], a2))
        conv = cb[:, fs] + a2 * cw[0:1, fs]
        conv = conv + a1 * cw[1:2, fs]
        conv = conv + a * cw[2:3, fs]
        y = jax.nn.gelu(conv, approximate=True) * b
        part = _dot(y.astype(BF16), wd_ref[fs, :])
        yv = part if yv is None else yv + part
    yn = yv * lax.rsqrt(jnp.mean(yv * yv, axis=-1, keepdims=True) + EPS) * g2_ref[...]
    o_ref[...] = x_ref[...] + gate_ref[0] * yn


def _ffn(x, g1, shift, scale, w_up, w_down, layer, cw, cb, inj, g2, gate, tm, tps, tf):
    m, d = x.shape
    ff = w_down.shape[1]
    r = shift.shape[1]
    inject = inj is not None
    if inject:
        aux_shape, aux_spec = (m, ff), pl.BlockSpec((tm, ff), lambda i: (i, 0))
        inj_spec = pl.BlockSpec((tm, ff), lambda i: (i, 0))
    else:
        aux_shape, aux_spec = (m // tm, 2, ff), pl.BlockSpec((1, 2, ff), lambda i: (i, 0, 0))
        inj = jnp.zeros((8, LANES), F32)
        inj_spec = pl.BlockSpec((8, LANES), lambda i: (0, 0))
    mod_spec = pl.BlockSpec((1, r, d), lambda i: (i // tps, 0, 0))
    fix = lambda i: (0, 0)
    return pl.pallas_call(
        functools.partial(_ffn_body, tps=tps, tf=tf, inject=inject),
        grid=(m // tm,),
        in_specs=[pl.BlockSpec((tm, d), lambda i: (i, 0)), pl.BlockSpec((1, d), fix), mod_spec, mod_spec,
                  pl.BlockSpec((None, d, 2 * ff), lambda i: (layer, 0, 0)),
                  pl.BlockSpec((None, ff, d), lambda i: (layer, 0, 0)),
                  pl.BlockSpec((CONV_W, ff), fix), pl.BlockSpec((1, ff), fix),
                  inj_spec, pl.BlockSpec((1, d), fix), mod_spec],
        out_specs=[pl.BlockSpec((tm, d), lambda i: (i, 0)), aux_spec],
        out_shape=[jax.ShapeDtypeStruct((m, d), F32), jax.ShapeDtypeStruct(aux_shape, F32)],
        scratch_shapes=[pltpu.VMEM((2, ff), F32)],
        compiler_params=_cp("arbitrary"), name="conv_ffn",
    )(x, g1.reshape(1, d), shift, scale, w_up, w_down, cw, cb.reshape(1, ff), inj, g2.reshape(1, d), gate)


def _kth_largest(stats_fn, cmin0, cmax0, k):
    def flag(lo, hi):
        return jnp.max(jnp.where(lo < hi, 1.0, 0.0))

    def body(st):
        cmin, cmax, _ = st
        mid = cmin + 0.5 * (cmax - cmin)
        p = jnp.where(mid > cmin, jnp.where(mid <= cmax, mid, cmax), cmax)
        cnt, mn_ge, mx_lt = stats_fn(p)
        is_open = cmin < cmax
        take_lo = cnt >= k
        ncmin = jnp.where(is_open, jnp.where(take_lo, mn_ge, cmin), cmin)
        ncmax = jnp.where(is_open, jnp.where(take_lo, cmax, mx_lt), cmax)
        return ncmin, ncmax, flag(ncmin, ncmax)

    return lax.while_loop(lambda st: st[2] > 0.0, body, (cmin0, cmax0, flag(cmin0, cmax0)))[0]


def _kth_largest_bracketed(count_fn, stats_fn, row_min, row_max, k, light_passes):
    def light(i, st):
        lo, hi = st
        mid = lo + 0.5 * (hi - lo)
        take = count_fn(mid) >= k
        return jnp.where(take, mid, lo), jnp.where(take, hi, mid)

    lo, hi = lax.fori_loop(0, light_passes, light, (row_min, row_max))
    _, cmin, _ = stats_fn(lo)
    cnt_hi, mn_hi, mx_hi = stats_fn(hi)
    at_top = cnt_hi >= k
    return _kth_largest(stats_fn, jnp.where(at_top, mn_hi, cmin), jnp.where(at_top, row_max, mx_hi), k)


def _lane_blocks(s):
    return [s[:, c * LANES:(c + 1) * LANES] for c in range(s.shape[1] // LANES)]


def _softmax_probs(blocks, m_prev, l_prev, exp=jnp.exp):
    mloc = functools.reduce(jnp.maximum, blocks)
    m_new = jnp.maximum(m_prev, jnp.max(mloc, axis=1, keepdims=True))
    alpha = exp(m_prev - m_new)
    ps = [exp(b - m_new) for b in blocks]
    l_new = alpha * l_prev + jnp.sum(functools.reduce(jnp.add, ps), axis=1, keepdims=True)
    p = (jnp.concatenate(ps, axis=1) if len(ps) > 1 else ps[0]).astype(BF16)
    return p, alpha, m_new, l_new


def _scale_acc(alpha, acc):
    dv = acc.shape[-1]
    if dv < LANES:
        return alpha[:, :dv] * acc
    return (alpha if dv == LANES else jnp.concatenate([alpha] * (dv // LANES), axis=1)) * acc


def _softmax_step(s, v_bf16, m_ref, l_ref, acc_ref, idx, values_transposed=False, exp=jnp.exp):
    p, alpha, m_new, l_new = _softmax_probs(_lane_blocks(s), m_ref[idx], l_ref[idx], exp)
    pv = _dot_nt(p, v_bf16) if values_transposed else _dot(p, v_bf16)
    acc_ref[idx] = _scale_acc(alpha, acc_ref[idx]) + pv
    m_ref[idx] = m_new
    l_ref[idx] = l_new


def _tie_bias(x, t, need, carry, tri):
    eq = x == t
    pref = _dot(jnp.where(eq, 1.0, 0.0).astype(BF16), tri) + carry
    bias = jnp.where(x > t, 0.0, jnp.where(eq, jnp.where(pref <= need, 0.0, NEG), NEG))
    return bias, pref[:, x.shape[1] - 1:]


def _fold8(x, op):
    parts = [x[i * 8:(i + 1) * 8] for i in range(x.shape[0] // 8)]
    while len(parts) > 1:
        parts = [op(parts[i], parts[i + 1]) for i in range(0, len(parts) - 1, 2)] + (parts[-1:] if len(parts) % 2 else [])
    return parts[0]


def _dsa_prompt_body(qa_ref, qi_ref, kiwq_ref, kiwk_ref, kv_ref, tril_ref, o_ref,
                     sct, sc, m_sc, l_sc, acc_sc, *, tq, ch, topk, light_passes):
    j = pl.program_id(1)
    q0 = j * tq
    nc = (q0 + tq + ch - 1) // ch
    qpos = q0 + _iota((1, tq), 1)
    kf = float(topk)
    inf = jnp.inf
    qi = qi_ref[0] * IDX_DIM ** -0.5
    qi4 = jnp.concatenate([qi[:, h * IDX_DIM:(h + 1) * IDX_DIM] for h in range(IDX_HEADS)], axis=0).astype(BF16)
    pick = jnp.where(_iota((8, LANES), 1) == _iota((8, LANES), 0) + IDX_DIM, 1.0, 0.0).astype(BF16)
    wt = sum(_dot_nt(pick, part) for part in _split3(kiwq_ref[0])) * IDX_HEADS ** -0.5

    def score_chunk(c, st):
        mx, mn = st
        c0 = pl.multiple_of(c * ch, ch)
        kc = kiwk_ref[0, pl.ds(c0, ch), :][:, :IDX_DIM].astype(BF16)
        lg = _dot_nt(kc, qi4)
        s = jnp.zeros((ch, tq), F32)
        for h in range(IDX_HEADS):
            s = s + jnp.maximum(lg[:, h * tq:(h + 1) * tq], 0.0) * wt[h:h + 1, :]
        s = jnp.where(s == 0.0, 0.0, s)
        valid = (c0 + _iota((ch, tq), 0)) <= qpos
        sct[c] = jnp.where(valid, s, -inf)
        mx = jnp.maximum(mx, _fold8(jnp.where(valid, s, -inf), jnp.maximum))
        mn = jnp.minimum(mn, _fold8(jnp.where(valid, s, inf), jnp.minimum))
        return mx, mn

    mx, mn = lax.fori_loop(0, nc, score_chunk, (jnp.full((8, tq), -inf, F32), jnp.full((8, tq), inf, F32)))
    active = qpos + 1 > topk
    row_max = jnp.where(active, jnp.max(mx, axis=0, keepdims=True), 0.0)
    row_min = jnp.where(active, jnp.min(mn, axis=0, keepdims=True), 0.0)

    def count_ge(p):
        def body(c, cnt):
            return cnt + _fold8(jnp.where(sct[c] >= p, 1.0, 0.0), jnp.add)
        return jnp.sum(lax.fori_loop(0, nc, body, jnp.zeros((8, tq), F32)), axis=0, keepdims=True)

    def stats(p):
        def body(c, st):
            cnt, mnge, mxlt = st
            x = sct[c]
            ge = x >= p
            return (cnt + _fold8(jnp.where(ge, 1.0, 0.0), jnp.add),
                    jnp.minimum(mnge, _fold8(jnp.where(ge, x, inf), jnp.minimum)),
                    jnp.maximum(mxlt, _fold8(jnp.where(ge, -inf, x), jnp.maximum)))
        init = (jnp.zeros((8, tq), F32), jnp.full((8, tq), inf, F32), jnp.full((8, tq), -inf, F32))
        cnt, mnge, mxlt = lax.fori_loop(0, nc, body, init)
        return (jnp.sum(cnt, axis=0, keepdims=True), jnp.min(mnge, axis=0, keepdims=True),
                jnp.max(mxlt, axis=0, keepdims=True))

    t = jnp.where(active, _kth_largest_bracketed(count_ge, stats, row_min, row_max, kf, light_passes), -inf)

    def count_sel(c, st):
        x = sct[c]
        return (st[0] + _fold8(jnp.where(x > t, 1.0, 0.0), jnp.add),
                st[1] + _fold8(jnp.where(x >= t, 1.0, 0.0), jnp.add))

    gt, ge = lax.fori_loop(0, nc, count_sel, (jnp.zeros((8, tq), F32), jnp.zeros((8, tq), F32)))
    need = kf - jnp.sum(gt, axis=0, keepdims=True)
    over = jnp.sum(ge, axis=0, keepdims=True) > kf
    has_ties = jnp.max(jnp.where(active, jnp.where(over, 1.0, 0.0), 0.0)) > 0.0
    eye = _eye_bf16(tq)

    def store_bias(c, sel_t):
        sel = _dot_nt(eye, sel_t.astype(BF16))
        sc[c] = jnp.where(sel > 0.5, 0.0, NEG)

    @pl.when(has_ties)
    def _():
        tril = tril_ref[...]

        def tie_chunk(c, carry):
            x = sct[c]
            eq = x == t
            pref = _dot(tril, jnp.where(eq, 1.0, 0.0).astype(BF16)) + carry
            store_bias(c, jnp.where(x > t, 1.0, jnp.where(eq, jnp.where(pref <= need, 1.0, 0.0), 0.0)))
            return pref[ch - 1:, :]

        lax.fori_loop(0, nc, tie_chunk, jnp.zeros((1, tq), F32))

    @pl.when(jnp.logical_not(has_ties))
    def _():
        def plain_chunk(c, carry):
            store_bias(c, jnp.where(sct[c] >= t, 1.0, 0.0))
            return carry

        lax.fori_loop(0, nc, plain_chunk, 0)

    cl = nc - 1
    sc[cl] = jnp.where((cl * ch + _iota((tq, ch), 1)) <= q0 + _iota((tq, ch), 0), sc[cl], NEG)

    qa = qa_ref[0].astype(F32) * (HEAD_DIM ** -0.5 * LOG2E)
    hpg = A_HEADS // A_KV_HEADS
    qh = [qa[:, h * HEAD_DIM:(h + 1) * HEAD_DIM].astype(BF16) for h in range(A_HEADS)]
    m_sc[...] = jnp.full(m_sc.shape, NEG, F32)
    l_sc[...] = jnp.zeros(l_sc.shape, F32)
    acc_sc[...] = jnp.zeros(acc_sc.shape, F32)

    def attend(c, carry):
        c0 = pl.multiple_of(c * ch, ch)
        bias = _lane_blocks(sc[c])
        kvc = kv_ref[0, pl.ds(c0, ch), :]
        kcs = [kvc[:, g * HEAD_DIM:(g + 1) * HEAD_DIM].astype(BF16) for g in range(A_KV_HEADS)]
        vcs = [kvc[:, (A_KV_HEADS + g) * HEAD_DIM:(A_KV_HEADS + g + 1) * HEAD_DIM].astype(BF16) for g in range(A_KV_HEADS)]
        ss = [_dot_nt(qh[hh], kcs[hh // hpg]) for hh in range(A_HEADS)]
        sm = []
        for hh in range(A_HEADS):
            g, rs = hh // hpg, pl.ds((hh % hpg) * tq, tq)
            blocks = [sb + bb for sb, bb in zip(_lane_blocks(ss[hh]), bias)]
            sm.append(_softmax_probs(blocks, m_sc[g, rs, :], l_sc[g, rs, :], exp=jnp.exp2))
        pv = [_dot(jnp.concatenate([sm[g * hpg + h][0] for h in range(hpg)], axis=0), vcs[g]) for g in range(A_KV_HEADS)]
        for hh in range(A_HEADS):
            g, rs = hh // hpg, pl.ds((hh % hpg) * tq, tq)
            m_sc[g, rs, :] = sm[hh][2]
            l_sc[g, rs, :] = sm[hh][3]
        for g in range(A_KV_HEADS):
            alpha = jnp.concatenate([sm[g * hpg + h][1] for h in range(hpg)], axis=0)
            acc_sc[g] = _scale_acc(alpha, acc_sc[g]) + pv[g]
        return carry

    lax.fori_loop(0, nc, attend, 0)
    outs = []
    for g in range(A_KV_HEADS):
        og = acc_sc[g] / l_sc[g][:, :HEAD_DIM]
        outs += [og[h * tq:(h + 1) * tq] for h in range(hpg)]
    o_ref[0] = jnp.concatenate(outs, axis=1).astype(o_ref.dtype)


def _upper_tri_bf16(n):
    return jnp.asarray(np.triu(np.ones((n, n), np.float32)), BF16)


def _dsa_prompt(qa, qi, kiw, kv, tq=128, ch=512):
    b, s, _ = qa.shape
    ch = min(ch, s)
    topk = min(TOPK_MAX, s // 4)
    hpg = A_HEADS // A_KV_HEADS
    return pl.pallas_call(
        functools.partial(_dsa_prompt_body, tq=tq, ch=ch, topk=topk, light_passes=14),
        grid=(b, s // tq),
        in_specs=[pl.BlockSpec((1, tq, qa.shape[2]), lambda i, j: (i, j, 0)),
                  pl.BlockSpec((1, tq, qi.shape[2]), lambda i, j: (i, j, 0)),
                  pl.BlockSpec((1, tq, LANES), lambda i, j: (i, j, 0)),
                  pl.BlockSpec((1, s, LANES), lambda i, j: (i, 0, 0)),
                  pl.BlockSpec((1, s, kv.shape[2]), lambda i, j: (i, 0, 0)),
                  pl.BlockSpec((ch, ch), lambda i, j: (0, 0))],
        out_specs=pl.BlockSpec((1, tq, A_HEADS * HEAD_DIM), lambda i, j: (i, j, 0)),
        out_shape=jax.ShapeDtypeStruct((b, s, A_HEADS * HEAD_DIM), BF16),
        scratch_shapes=[pltpu.VMEM((s // ch, ch, tq), F32),
                        pltpu.VMEM((s // ch, tq, ch), F32),
                        pltpu.VMEM((A_KV_HEADS, hpg * tq, LANES), F32),
                        pltpu.VMEM((A_KV_HEADS, hpg * tq, LANES), F32),
                        pltpu.VMEM((A_KV_HEADS, hpg * tq, HEAD_DIM), F32)],
        compiler_params=_cp("arbitrary", "arbitrary"), name="dsa_prompt",
    )(qa, qi, kiw, kiw, kv, jnp.asarray(np.tril(np.ones((ch, ch), np.float32)), BF16))


def _head_avg_matrix(width):
    seg = np.arange(width) // HEAD_DIM
    return jnp.asarray((seg[:, None] == seg[None, :]) / HEAD_DIM, BF16)


def _head_mean(x, avg):
    hi = x.astype(BF16)
    lo = (x - hi.astype(F32)).astype(BF16)
    return _dot(hi, avg) + _dot(lo, avg)


def _head_layernorm(o, gain, avg):
    d = o - _head_mean(o, avg)
    return d * lax.rsqrt(_head_mean(d * d, avg) + EPS) * gain


def _retention_body(q_ref, k_ref, v_ref, g_ref, s0_ref, dm_ref, qd_ref, kd_ref, cd_ref, gain_ref, avg_ref,
                    o_ref, sf_ref, s_sc):
    c = pl.program_id(1)

    @pl.when(c == 0)
    def _():
        s_sc[...] = s0_ref[0]

    q = q_ref[0].astype(BF16)
    k = k_ref[0]
    v = v_ref[0].astype(BF16)
    gt = g_ref[0]
    heads = range(B_HEADS)
    sls = [slice(h * HEAD_DIM, (h + 1) * HEAD_DIM) for h in heads]
    eye = _eye_bf16(HEAD_DIM)
    st = [s_sc[h] for h in heads]
    att = [_dot_nt(q[:, sl], k[:, sl].astype(BF16)) for sl in sls]
    qs = [_dot(q[:, sls[h]], st[h].astype(BF16)) for h in heads]
    kt = [_dot_nt(eye, (k[:, sls[h]] * kd_ref[h]).astype(BF16)) for h in heads]
    outs = [_dot((att[h] * dm_ref[h]).astype(BF16), v[:, sls[h]]) + qs[h] * qd_ref[h] for h in heads]
    for h in heads:
        s_sc[h] = st[h] * cd_ref[h] + _dot(kt[h].astype(BF16), v[:, sls[h]])
    o = _head_layernorm(jnp.concatenate(outs, axis=1), gain_ref[...], avg_ref[...])
    o_ref[0] = (gt * jax.nn.sigmoid(gt) * o).astype(o_ref.dtype)

    @pl.when(c == pl.num_programs(1) - 1)
    def _():
        sf_ref[0] = s_sc[...]


def _retention(q, k, v, g, s0, gain, chunk, lo, hi, out_dtype):
    b, s, w = q.shape
    dm, qd, kd, cd = _retention_tables(chunk, lo, hi)
    row = lambda i, c: (i, c, 0)
    fix3 = lambda i, c: (0, 0, 0)
    st_spec = pl.BlockSpec((1, B_HEADS, HEAD_DIM, HEAD_DIM), lambda i, c: (i, 0, 0, 0))
    return pl.pallas_call(
        _retention_body, grid=(b, s // chunk),
        in_specs=[pl.BlockSpec((1, chunk, w), row)] * 4 + [st_spec] +
                 [pl.BlockSpec(t.shape, fix3) for t in (dm, qd, kd, cd)] +
                 [pl.BlockSpec((1, w), lambda i, c: (0, 0)), pl.BlockSpec((w, w), lambda i, c: (0, 0))],
        out_specs=[pl.BlockSpec((1, chunk, w), row), st_spec],
        out_shape=[jax.ShapeDtypeStruct((b, s, w), out_dtype),
                   jax.ShapeDtypeStruct((b, B_HEADS, HEAD_DIM, HEAD_DIM), F32)],
        scratch_shapes=[pltpu.VMEM((B_HEADS, HEAD_DIM, HEAD_DIM), F32)],
        compiler_params=_cp("arbitrary", "arbitrary"), name="retention",
    )(q, k, v, g, s0, dm, qd, kd, cd, gain.reshape(1, w), _head_avg_matrix(w))


def _mlstm_body(q_ref, k_ref, v_ref, og_ref, gt_ref, cn0_ref, m0_ref, gain_ref, avg_ref,
                o_ref, cnf_ref, mf_ref, cn_sc, m_sc, *, lo, hi):
    c = pl.program_id(1)
    ck = q_ref.shape[1]
    nh = C_HEADS
    hd = HEAD_DIM

    @pl.when(c == 0)
    def _():
        cn_sc[...] = cn0_ref[0]
        m_sc[...] = m0_ref[0]

    q = q_ref[0].astype(BF16)
    k = k_ref[0]
    v = v_ref[0].astype(BF16)
    gates = gt_ref[0]
    row = _iota((ck, LANES), 0)
    tok = (row >= lo) & (row < hi)
    log_sig = jnp.minimum(gates, 0.0) - jnp.log1p(jnp.exp(-jnp.abs(gates)))
    lf = jnp.where(tok, log_sig, 0.0)
    ii = jnp.where(tok, gates, NEG)
    tril = jnp.where(_iota((ck, ck), 0) >= _iota((ck, ck), 1), 1.0, 0.0).astype(BF16)
    fcum = sum(_dot(tril, part) for part in _split3(lf))
    f_all = pltpu.roll(fcum, LANES - nh, 1)
    a_all = ii - f_all
    cm = a_all
    step = 1
    while step < ck:
        cm = jnp.maximum(cm, jnp.where(row >= step, pltpu.roll(cm, step, 0), -jnp.inf))
        step *= 2
    m_prev = m_sc[...]
    m_t = f_all + jnp.maximum(m_prev, cm)
    inter_all = jnp.exp(f_all + m_prev - m_t)
    floor_all = jnp.exp(-m_t)
    g_all = f_all - m_t
    f_last = f_all[ck - 1:, :]
    m_end = m_t[ck - 1:, :]
    w_all = jnp.exp(f_last + a_all - m_end)
    dec_all = jnp.exp(f_last + m_prev - m_end)
    m_sc[...] = m_end
    pick = jnp.where(_iota((nh, LANES), 0) == _iota((nh, LANES), 1), 1.0, 0.0).astype(BF16)
    a_rows = sum(_dot_nt(pick, part) for part in _split3(a_all))
    causal = _iota((ck, ck), 0) >= _iota((ck, ck), 1)
    eye = _eye_bf16(hd)
    ones = jnp.ones((ck, hd), BF16)
    n_lanes = _iota((hd, 2 * hd), 1) >= hd
    heads = range(nh)
    sls = [slice(h * hd, (h + 1) * hd) for h in heads]
    col = lambda x, h: x[:, h:h + 1]
    cn = [cn_sc[h] for h in heads]
    v1 = [jnp.concatenate([v[:, sl], ones], axis=1) for sl in sls]
    qk = [_dot_nt(q[:, sl], k[:, sl].astype(BF16)) for sl in sls]
    qc = [_dot(q[:, sls[h]], cn[h].astype(BF16)) for h in heads]
    kw = [k[:, sls[h]] * col(w_all, h) for h in heads]
    kw_hi = [x.astype(BF16) for x in kw]
    kw_lo = [(x - y.astype(F32)).astype(BF16) for x, y in zip(kw, kw_hi)]
    kt_hi = [_dot_nt(eye, x) for x in kw_hi]
    kt_lo = [_dot_nt(eye, x) for x in kw_lo]
    dmat = [jnp.exp(jnp.where(causal, col(g_all, h) + a_rows[h:h + 1, :], -jnp.inf)) for h in heads]
    nd = [_dot((qk[h] * dmat[h]).astype(BF16), v1[h]) + col(inter_all, h) * qc[h] for h in heads]
    outs = [(nd[h] / jnp.maximum(jnp.abs(pltpu.roll(nd[h], hd, 1)), col(floor_all, h)))[:, :hd] for h in heads]
    upd = [_dot(kt_hi[h].astype(BF16), v1[h]) for h in heads]
    fix = [_dot(kt_lo[h].astype(BF16), v1[h]) for h in heads]
    for h in heads:
        cn_sc[h] = col(dec_all, h) * cn[h] + upd[h] + jnp.where(n_lanes, fix[h], 0.0)
    hc = jax.nn.sigmoid(og_ref[0]) * jnp.concatenate(outs, axis=1)
    o_ref[0] = _head_layernorm(hc, gain_ref[...], avg_ref[...]).astype(o_ref.dtype)

    @pl.when(c == pl.num_programs(1) - 1)
    def _():
        cnf_ref[0] = cn_sc[...]
        mf_ref[0] = m_sc[...]


def _mlstm(q, k, v, og, gates, c0, n0, m0, gain, chunk, lo, hi, out_dtype):
    b, s, w = q.shape
    nh = C_HEADS
    row = lambda i, c: (i, c, 0)
    hd = HEAD_DIM
    cn_spec = pl.BlockSpec((1, nh, hd, 2 * hd), lambda i, c: (i, 0, 0, 0))
    m_spec = pl.BlockSpec((1, 1, LANES), lambda i, c: (i, 0, 0))
    cn0 = jnp.concatenate([c0, jnp.broadcast_to(n0[..., None], (b, nh, hd, hd))], axis=-1)
    m0p = jnp.pad(m0, ((0, 0), (0, LANES - nh))).reshape(b, 1, LANES)
    o, cnf, mf = pl.pallas_call(
        functools.partial(_mlstm_body, lo=lo, hi=hi), grid=(b, s // chunk),
        in_specs=[pl.BlockSpec((1, chunk, w), row)] * 4 + [pl.BlockSpec((1, chunk, LANES), row), cn_spec, m_spec,
                                                          pl.BlockSpec((1, w), lambda i, c: (0, 0)),
                                                          pl.BlockSpec((w, w), lambda i, c: (0, 0))],
        out_specs=[pl.BlockSpec((1, chunk, w), row), cn_spec, m_spec],
        out_shape=[jax.ShapeDtypeStruct((b, s, w), out_dtype),
                   jax.ShapeDtypeStruct((b, nh, hd, 2 * hd), F32),
                   jax.ShapeDtypeStruct((b, 1, LANES), F32)],
        scratch_shapes=[pltpu.VMEM((nh, hd, 2 * hd), F32), pltpu.VMEM((1, LANES), F32)],
        compiler_params=_cp("arbitrary", "arbitrary"), name="mlstm",
    )(q, k, v, og, gates, cn0, m0p, gain.reshape(1, w), _head_avg_matrix(w))
    return o, cnf[..., :hd], cnf[..., hd], mf[:, 0, :nh]


def _diff_lambda(lam_ref, lam_init):
    lp = lam_ref[...]
    s01 = jnp.sum(lp[0:1] * lp[1:2], axis=1, keepdims=True)
    s23 = jnp.sum(lp[2:3] * lp[3:4], axis=1, keepdims=True)
    return jnp.exp(s01) - jnp.exp(s23) + lam_init


def _diff_finish(o0, l0, o1, l1, lam, subln, lam_init):
    od = o0 / l0 - lam * (o1 / l1)
    return od * lax.rsqrt(jnp.mean(od * od, axis=-1, keepdims=True) + EPS) * subln * (1.0 - lam_init)


def _diff_prompt_body(q_ref, k_ref, v_ref, lam_ref, sub_ref, o_ref, m_sc, l_sc, acc_sc, *, lam_init):
    i = pl.program_id(1)
    j = pl.program_id(2)
    t = q_ref.shape[1]

    @pl.when(j == 0)
    def _():
        m_sc[...] = jnp.full(m_sc.shape, NEG, F32)
        l_sc[...] = jnp.zeros(l_sc.shape, F32)
        acc_sc[...] = jnp.zeros(acc_sc.shape, F32)

    def step(diag):
        q = (q_ref[0].astype(F32) * (HEAD_DIM ** -0.5 * LOG2E)).astype(BF16)
        k = k_ref[0].astype(BF16)
        v = v_ref[0].astype(BF16)
        keep = (_iota((t, t), 0) >= _iota((t, t), 1)) if diag else None
        pairs = range(2 * D_HEADS)
        ss = [_dot_nt(q[:, idx * HEAD_DIM:(idx + 1) * HEAD_DIM], k[:, idx * HEAD_DIM:(idx + 1) * HEAD_DIM]) for idx in pairs]
        if diag:
            ss = [jnp.where(keep, s, NEG) for s in ss]
        sm = [_softmax_probs(_lane_blocks(ss[idx]), m_sc[idx], l_sc[idx], jnp.exp2) for idx in pairs]
        pv = [_dot(sm[idx][0], v[:, (idx % D_HEADS) * D_VDIM:(idx % D_HEADS + 1) * D_VDIM]) for idx in pairs]
        for idx in pairs:
            _, alpha, m_new, l_new = sm[idx]
            acc_sc[idx] = _scale_acc(alpha, acc_sc[idx]) + pv[idx]
            m_sc[idx] = m_new
            l_sc[idx] = l_new

    @pl.when(j < i)
    def _():
        step(False)

    @pl.when(j == i)
    def _():
        step(True)
        lam = _diff_lambda(lam_ref, lam_init)
        outs = [_diff_finish(acc_sc[h], l_sc[h], acc_sc[D_HEADS + h], l_sc[D_HEADS + h], lam, sub_ref[...], lam_init)
                for h in range(D_HEADS)]
        o_ref[0] = jnp.concatenate(outs, axis=1).astype(o_ref.dtype)


def _diff_prompt(q, k, v, lam_par, subln, lam_init, t=512):
    b, s, w = q.shape
    t = min(t, s)
    n = s // t
    return pl.pallas_call(
        functools.partial(_diff_prompt_body, lam_init=lam_init), grid=(b, n, n),
        in_specs=[pl.BlockSpec((1, t, w), lambda bi, i, j: (bi, i, 0)),
                  pl.BlockSpec((1, t, w), lambda bi, i, j: (bi, jnp.minimum(i, j), 0)),
                  pl.BlockSpec((1, t, w), lambda bi, i, j: (bi, jnp.minimum(i, j), 0)),
                  pl.BlockSpec(lam_par.shape, lambda bi, i, j: (0, 0)),
                  pl.BlockSpec((1, D_VDIM), lambda bi, i, j: (0, 0))],
        out_specs=pl.BlockSpec((1, t, w), lambda bi, i, j: (bi, i, 0)),
        out_shape=jax.ShapeDtypeStruct((b, s, w), BF16),
        scratch_shapes=[pltpu.VMEM((2 * D_HEADS, t, LANES), F32), pltpu.VMEM((2 * D_HEADS, t, LANES), F32),
                        pltpu.VMEM((2 * D_HEADS, t, D_VDIM), F32)],
        compiler_params=_cp("arbitrary", "arbitrary", "arbitrary"), name="diff_prompt",
    )(q, k, v, lam_par, subln.reshape(1, D_VDIM))


def _page_specs(rows, layer, group, n_pages):
    def spec(g):
        def imap(b, j, pt):
            return (layer, pt[b, jnp.minimum(j * group + g, n_pages - 1)], 0, 0)
        return pl.BlockSpec((None, None, rows, PAGE_SIZE), imap)
    return [spec(g) for g in range(group)]


def _cat_lanes(refs):
    return jnp.concatenate([r[...] for r in refs], axis=1)


def _new_key_valid(nq, nk):
    rq = _iota((nq, nk), 0) % SLOT
    rk = _iota((nq, nk), 1)
    return (rk >= SLOT_LO) & (rk < SLOT_LO + (SLOT - 2 * SLOT_LO)) & (rk <= rq)


def _idx_scores(qi4, w, keys_t_bf16):
    lg = _dot(qi4, keys_t_bf16)
    s = jnp.zeros((SLOT, keys_t_bf16.shape[1]), F32)
    for h in range(IDX_HEADS):
        s = s + jnp.maximum(lg[h * SLOT:(h + 1) * SLOT], 0.0) * w[:, h:h + 1]
    return jnp.where(s == 0.0, 0.0, s)


def _dsa_sample_scores_body(pt_ref, qi_ref, kiw_ref, knew_ref, *rest, group):
    pages, o_ref = rest[:group], rest[group]
    j = pl.program_id(1)
    last = pl.num_programs(1) - 1
    qi = qi_ref[0] * IDX_DIM ** -0.5
    qi4 = jnp.concatenate([qi[:, h * IDX_DIM:(h + 1) * IDX_DIM] for h in range(IDX_HEADS)], axis=0).astype(BF16)
    w = kiw_ref[0][:, IDX_DIM:IDX_DIM + IDX_HEADS] * IDX_HEADS ** -0.5

    @pl.when(j < last)
    def _():
        o_ref[0, 0] = _idx_scores(qi4, w, _cat_lanes(pages).astype(BF16))

    @pl.when(j == last)
    def _():
        s = _idx_scores(qi4, w, knew_ref[0].astype(BF16))
        s = jnp.where(_new_key_valid(SLOT, PAGE_SIZE), s, -jnp.inf)
        pad = jnp.full((SLOT, (group - 1) * PAGE_SIZE), -jnp.inf, F32)
        o_ref[0, 0] = jnp.concatenate([s, pad], axis=1) if group > 1 else s


def _dsa_sample_scores(page_table, qi, kiw, knew_t, cache_idx_t, layer, group):
    db, n_pages = page_table.shape
    nch = n_pages // group + 1
    gw = group * PAGE_SIZE
    grid_spec = pltpu.PrefetchScalarGridSpec(
        num_scalar_prefetch=1, grid=(db, nch),
        in_specs=[pl.BlockSpec((1, SLOT, qi.shape[2]), lambda b, j, pt: (b, 0, 0)),
                  pl.BlockSpec((1, SLOT, LANES), lambda b, j, pt: (b, 0, 0)),
                  pl.BlockSpec((1, IDX_DIM, PAGE_SIZE), lambda b, j, pt: (b, 0, 0))]
                 + _page_specs(IDX_DIM, layer, group, n_pages),
        out_specs=pl.BlockSpec((1, 1, SLOT, gw), lambda b, j, pt: (b, j, 0, 0)))
    return pl.pallas_call(
        functools.partial(_dsa_sample_scores_body, group=group), grid_spec=grid_spec,
        out_shape=jax.ShapeDtypeStruct((db, nch, SLOT, gw), F32),
        compiler_params=_cp("arbitrary", "arbitrary"), name="dsa_sample_scores",
    )(page_table, qi, kiw, knew_t, *([cache_idx_t] * group))


def _dsa_sample_attend_body(pt_ref, qa_ref, sc_ref, kvnew_ref, tri_ref, *rest, group, topk):
    pages, o_ref = rest[:group], rest[group]
    t_sc, need_sc, carry_sc, m_sc, l_sc, acc_sc, ties_sm = rest[group + 1:]
    j = pl.program_id(1)
    last = pl.num_programs(1) - 1
    hpg = A_HEADS // A_KV_HEADS

    @pl.when(j == 0)
    def _():
        x = sc_ref[0]

        def stats(p):
            ge = x >= p[None]
            cnt = jnp.sum(jnp.sum(jnp.where(ge, 1.0, 0.0), axis=0), axis=1, keepdims=True)
            mnge = jnp.min(jnp.min(jnp.where(ge, x, jnp.inf), axis=0), axis=1, keepdims=True)
            mxlt = jnp.max(jnp.max(jnp.where(ge, -jnp.inf, x), axis=0), axis=1, keepdims=True)
            return cnt, mnge, mxlt

        finite = x > -jnp.inf
        cmax0 = jnp.max(jnp.max(x, axis=0), axis=1, keepdims=True)
        cmin0 = jnp.min(jnp.min(jnp.where(finite, x, jnp.inf), axis=0), axis=1, keepdims=True)
        def count_ge(p):
            return jnp.sum(jnp.sum(jnp.where(x >= p[None], 1.0, 0.0), axis=0), axis=1, keepdims=True)

        t = _kth_largest_bracketed(count_ge, stats, cmin0, cmax0, float(topk), 14)
        n_gt =jnp.sum(jnp.sum(jnp.where(x > t[None], 1.0, 0.0), axis=0), axis=1, keepdims=True)
        n_ge = jnp.sum(jnp.sum(jnp.where(x >= t[None], 1.0, 0.0), axis=0), axis=1, keepdims=True)
        ties_sm[0] = (jnp.max(n_ge) > float(topk)).astype(jnp.int32)
        t_sc[...] = jnp.broadcast_to(t, t_sc.shape)
        need_sc[...] = jnp.broadcast_to(float(topk) - n_gt, need_sc.shape)
        carry_sc[...] = jnp.zeros(carry_sc.shape, F32)
        m_sc[...] = jnp.full(m_sc.shape, NEG, F32)
        l_sc[...] = jnp.zeros(l_sc.shape, F32)
        acc_sc[...] = jnp.zeros(acc_sc.shape, F32)

    qa = qa_ref[0] * HEAD_DIM ** -0.5
    qg = [jnp.concatenate([qa[:, (g * hpg + h) * HEAD_DIM:(g * hpg + h + 1) * HEAD_DIM] for h in range(hpg)],
                          axis=0).astype(BF16) for g in range(A_KV_HEADS)]
    t = t_sc[...][:, :1]
    need = need_sc[...][:, :1]
    tri = tri_ref[...]

    def tie_bias(x):
        carry = carry_sc[...][:, :1]
        biases = []
        for xb in _lane_blocks(x):
            bias, carry = _tie_bias(xb, t, need, carry, tri)
            biases.append(bias)
        carry_sc[...] = jnp.broadcast_to(carry, carry_sc.shape)
        return jnp.concatenate(biases, axis=1) if len(biases) > 1 else biases[0]

    def attend(x, kv_t):
        bias = lax.cond(ties_sm[0] > 0, tie_bias, lambda xs: jnp.where(xs >= t, 0.0, NEG), x)
        bias4 = jnp.concatenate([bias] * hpg, axis=0)
        for g in range(A_KV_HEADS):
            kc = kv_t[g * HEAD_DIM:(g + 1) * HEAD_DIM].astype(BF16)
            vc = kv_t[(A_KV_HEADS + g) * HEAD_DIM:(A_KV_HEADS + g + 1) * HEAD_DIM].astype(BF16)
            _softmax_step(_dot(qg[g], kc) + bias4, vc, m_sc, l_sc, acc_sc, g, values_transposed=True)

    @pl.when(j < last)
    def _():
        attend(sc_ref[0, j], _cat_lanes(pages))

    @pl.when(j == last)
    def _():
        attend(sc_ref[0, j][:, :PAGE_SIZE], kvnew_ref[0])
        outs = []
        for g in range(A_KV_HEADS):
            og = acc_sc[g] / l_sc[g][:, :HEAD_DIM]
            outs += [og[h * SLOT:(h + 1) * SLOT] for h in range(hpg)]
        o_ref[0] = jnp.concatenate(outs, axis=1)


def _dsa_sample_attend(page_table, qa, scores, kvnew, cache_kv, layer, group, topk):
    db, n_pages = page_table.shape
    nch = n_pages // group + 1
    gw = group * PAGE_SIZE
    hpg = A_HEADS // A_KV_HEADS
    kvw = 2 * A_KV_HEADS * HEAD_DIM
    grid_spec = pltpu.PrefetchScalarGridSpec(
        num_scalar_prefetch=1, grid=(db, nch),
        in_specs=[pl.BlockSpec((1, SLOT, qa.shape[2]), lambda b, j, pt: (b, 0, 0)),
                  pl.BlockSpec((1, nch, SLOT, gw), lambda b, j, pt: (b, 0, 0, 0)),
                  pl.BlockSpec((1, kvw, PAGE_SIZE), lambda b, j, pt: (b, 0, 0)),
                  pl.BlockSpec((PAGE_SIZE, PAGE_SIZE), lambda b, j, pt: (0, 0))]
                 + _page_specs(kvw, layer, group, n_pages),
        out_specs=pl.BlockSpec((1, SLOT, A_HEADS * HEAD_DIM), lambda b, j, pt: (b, 0, 0)),
        scratch_shapes=[pltpu.VMEM((SLOT, LANES), F32), pltpu.VMEM((SLOT, LANES), F32), pltpu.VMEM((SLOT, LANES), F32),
                        pltpu.VMEM((A_KV_HEADS, hpg * SLOT, LANES), F32),
                        pltpu.VMEM((A_KV_HEADS, hpg * SLOT, LANES), F32),
                        pltpu.VMEM((A_KV_HEADS, hpg * SLOT, HEAD_DIM), F32),
                        pltpu.SMEM((1,), jnp.int32)])
    return pl.pallas_call(
        functools.partial(_dsa_sample_attend_body, group=group, topk=topk), grid_spec=grid_spec,
        out_shape=jax.ShapeDtypeStruct((db, SLOT, A_HEADS * HEAD_DIM), F32),
        compiler_params=_cp("arbitrary", "arbitrary"), name="dsa_sample_attend",
    )(page_table, qa, scores, kvnew, _upper_tri_bf16(PAGE_SIZE), *([cache_kv] * group))


def _diff_sample_body(pt_ref, q_ref, knew_ref, vnew_ref, lam_ref, sub_ref, *rest, group, lam_init):
    kpages, vpages, o_ref = rest[:group], rest[group:2 * group], rest[2 * group]
    m_sc, l_sc, acc_sc = rest[2 * group + 1:]
    j = pl.program_id(1)
    last = pl.num_programs(1) - 1
    nmap = 2 * D_HEADS
    w = nmap * HEAD_DIM

    @pl.when(j == 0)
    def _():
        m_sc[...] = jnp.full(m_sc.shape, NEG, F32)
        l_sc[...] = jnp.zeros(l_sc.shape, F32)
        acc_sc[...] = jnp.zeros(acc_sc.shape, F32)

    q = q_ref[0] * HEAD_DIM ** -0.5
    lane_map = _iota((SLOT, w), 1) // HEAD_DIM
    qb = jnp.concatenate([jnp.where(lane_map == m * D_HEADS + h, q, 0.0)
                          for h in range(D_HEADS) for m in range(2)], axis=0).astype(BF16)
    rows_h = 2 * SLOT

    def step(k_t, v_refs, mask):
        s = _dot(qb, k_t.astype(BF16))
        if mask is not None:
            s = jnp.where(mask, s, NEG)
        p, alpha, m_new, l_new = _softmax_probs(_lane_blocks(s), m_sc[...], l_sc[...])
        m_sc[...] = m_new
        l_sc[...] = l_new
        for h in range(D_HEADS):
            vh = jnp.concatenate([r[pl.ds(h, PAGE_SIZE, stride=D_HEADS), :] for r in v_refs], axis=0).astype(BF16)
            rs = slice(h * rows_h, (h + 1) * rows_h)
            acc_sc[h] = alpha[rs] * acc_sc[h] + _dot(p[rs], vh)

    @pl.when(j < last)
    def _():
        step(_cat_lanes(kpages), vpages, None)

    @pl.when(j == last)
    def _():
        step(knew_ref[0], [vnew_ref.at[0]], _new_key_valid(nmap * SLOT, PAGE_SIZE))
        lam = _diff_lambda(lam_ref, lam_init)
        l = l_sc[...]
        outs = []
        for h in range(D_HEADS):
            acc = acc_sc[h]
            lh = l[h * rows_h:(h + 1) * rows_h]
            outs.append(_diff_finish(acc[:SLOT], lh[:SLOT], acc[SLOT:], lh[SLOT:], lam, sub_ref[...], lam_init))
        o_ref[0] = jnp.concatenate(outs, axis=1)


def _diff_sample(page_table, q, knew_t, vnew, lam_par, subln, cache_k_t, cache_v, layer, group, lam_init):
    db, n_pages = page_table.shape
    nch = n_pages // group + 1
    nmap = 2 * D_HEADS
    w = q.shape[2]
    vw = D_HEADS * D_VDIM
    vrows = PAGE_SIZE * D_HEADS
    grid_spec = pltpu.PrefetchScalarGridSpec(
        num_scalar_prefetch=1, grid=(db, nch),
        in_specs=[pl.BlockSpec((1, SLOT, w), lambda b, j, pt: (b, 0, 0)),
                  pl.BlockSpec((1, w, PAGE_SIZE), lambda b, j, pt: (b, 0, 0)),
                  pl.BlockSpec((1, vrows, D_VDIM), lambda b, j, pt: (b, 0, 0)),
                  pl.BlockSpec(lam_par.shape, lambda b, j, pt: (0, 0)),
                  pl.BlockSpec((1, D_VDIM), lambda b, j, pt: (0, 0))]
                 + _page_specs(w, layer, group, n_pages) + _page_specs(vrows, layer, group, n_pages),
        out_specs=pl.BlockSpec((1, SLOT, vw), lambda b, j, pt: (b, 0, 0)),
        scratch_shapes=[pltpu.VMEM((nmap * SLOT, LANES), F32), pltpu.VMEM((nmap * SLOT, LANES), F32),
                        pltpu.VMEM((D_HEADS, 2 * SLOT, D_VDIM), F32)])
    return pl.pallas_call(
        functools.partial(_diff_sample_body, group=group, lam_init=lam_init), grid_spec=grid_spec,
        out_shape=jax.ShapeDtypeStruct((db, SLOT, vw), F32),
        compiler_params=_cp("arbitrary", "arbitrary"), name="diff_sample",
    )(page_table, q, knew_t, vnew, lam_par, subln.reshape(1, D_VDIM), *([cache_k_t] * group), *([cache_v] * group))


def _split_cols(w, sizes):
    out, o = [], 0
    for s in sizes:
        out.append(w[:, o:o + s])
        o += s
    return out


def _pad_cols(w, n):
    return jnp.pad(w, ((0, 0), (0, n - w.shape[1])))


def _pad_rows(a, n):
    return jnp.pad(a, ((0, 0), (0, n - a.shape[1]), (0, 0)))


def _keys_t(a):
    return jnp.pad(jnp.swapaxes(a, 1, 2), ((0, 0), (0, 0), (0, PAGE_SIZE - a.shape[1])))


def _pos_minor(cache):
    nd = cache.ndim
    t = jnp.transpose(cache, (0, 1) + tuple(range(3, nd)) + (2,))
    return t.reshape(cache.shape[0], cache.shape[1], -1, cache.shape[2])


def _ab_weights(w_in):
    hd = HEAD_DIM
    aq, ak, av, iq, ik, iw, bq, bk, bv, bg = _split_cols(
        w_in, (A_HEADS * hd, A_KV_HEADS * hd, A_KV_HEADS * hd, IDX_HEADS * IDX_DIM, IDX_DIM, IDX_HEADS,
               B_HEADS * hd, B_HEADS * hd, B_HEADS * hd, B_HEADS * hd))
    segs = [aq, jnp.concatenate([ak, av], axis=1), iq, _pad_cols(jnp.concatenate([ik, iw], axis=1), LANES), bq, bk, bv, bg]
    return [s.astype(BF16) for s in segs]


def _cd_weights(w_in):
    hd = HEAD_DIM
    cq, ck, cv, ci, cf, co, dq, dk, dv = _split_cols(
        w_in, (C_HEADS * hd, C_HEADS * hd, C_HEADS * hd, C_HEADS, C_HEADS, C_HEADS * hd,
               2 * D_HEADS * hd, 2 * D_HEADS * hd, D_HEADS * D_VDIM))
    segs = [cq, ck, cv, co, _pad_cols(jnp.concatenate([ci, cf], axis=1), LANES), dq, dk, dv]
    return [s.astype(BF16) for s in segs]


def kernel(x_prompt, x_sample, c_prompt, c_sample, page_table, cache_a_kv, cache_a_idx, cache_d_k, cache_d_v, state_ret, state_mlstm_c, state_mlstm_n, state_mlstm_m, state_ffn_conv, norm_pre_mix, norm_post_mix, norm_pre_ffn, norm_post_ffn, w_ada, b_ada, w_ab_in, w_ab_out, ret_gain, w_cd_in, w_cd_out, c_gate_bias, c_gain, d_lambda, d_subln, ffn_w_up, ffn_conv_w, ffn_conv_b, ffn_w_down):
    b, s, d = x_prompt.shape
    db, t, _ = x_sample.shape
    depth = w_ada.shape[0]
    n_pages = page_table.shape[1]
    past = n_pages * PAGE_SIZE
    d_ff = ffn_w_down.shape[1]
    assert t == SLOT - 2 * SLOT_LO and past >= TOPK_MAX and s % 1024 == 0
    hi = SLOT_LO + t
    ms = db * SLOT
    tm_in, tm_ffn, tf = 512, 512, 256
    group = math.gcd(n_pages, 16)
    group_a = math.gcd(n_pages, 32)
    topk_s = min(TOPK_MAX, (past + t) // 4)

    n_c = b + db
    c_all = jnp.pad(jnp.concatenate([c_prompt, c_sample], axis=0), ((0, -n_c % 8), (0, 0)))
    mods = _modulation(c_all, w_ada, b_ada)
    w_up, w_down = ffn_w_up.astype(BF16), ffn_w_down.astype(BF16)
    w_ab_in, w_ab_out = w_ab_in.astype(BF16), w_ab_out.astype(BF16)
    w_cd_in, w_cd_out = w_cd_in.astype(BF16), w_cd_out.astype(BF16)

    xp = x_prompt.reshape(b * s, d)
    xs = jnp.pad(x_sample, ((0, 0), (SLOT_LO, SLOT - hi), (0, 0))).reshape(ms, d)

    pos_s = past + (np.arange(ms) % SLOT) - SLOT_LO
    tp_p, tf_p = _rope_tables(np.arange(s), HEAD_DIM // 4, ROPE_THETA), _rope_tables(np.arange(s), HEAD_DIM, RET_THETA)
    tp_s, tf_s = _rope_tables(pos_s, HEAD_DIM // 4, ROPE_THETA), _rope_tables(pos_s, HEAD_DIM, RET_THETA)

    cache_a_kv = _pos_minor(cache_a_kv)
    cache_a_idx = _pos_minor(cache_a_idx)
    cache_d_k = _pos_minor(cache_d_k)
    cache_d_v = cache_d_v.reshape(*cache_d_v.shape[:2], -1, D_VDIM)

    hd = HEAD_DIM
    w512 = 8 * hd
    ab_defs = lambda dt: [(w512, dt), (4 * hd, F32), (4 * hd, dt), (LANES, F32), (w512, dt), (w512, F32), (w512, dt), (w512, F32)]
    cd_defs = lambda dt: [(w512, dt), (w512, F32), (w512, dt), (w512, F32), (LANES, F32), (w512, dt), (w512, F32), (w512, F32)]
    zeros = lambda *shape: jnp.zeros(shape, F32)

    outs = {k: [] for k in ("pa_kv", "pa_idx", "pd_k", "pd_v", "p_ret", "p_mc", "p_mn", "p_mm", "p_cv",
                            "sa_kv", "sa_idx", "sd_k", "sd_v", "s_ret", "s_mc", "s_mn", "s_mm", "s_cv")}
    for l in range(depth):
        p = l // 2
        m6 = mods[l, :n_c].reshape(n_c, 6, d)
        mp = [m6[:b, i][:, None, :] for i in range(6)]
        msm = [jnp.repeat(m6[b:, i], SLOT, axis=0)[None] for i in range(6)]
        if l % 2 == 0:
            wts = _ab_weights(w_ab_in[p])
            wo = w_ab_out[p].astype(BF16)
            qa, kv, qi, kiw, bq, bk, bv, bg = _in_proj(_ab_in_body, "ab_in", xp, norm_pre_mix[l], mp[0], mp[1],
                                                       [tp_p, tf_p], [], wts, ab_defs(BF16), tm_in, s // tm_in)
            r3 = lambda a: a.reshape(b, s, a.shape[-1])
            a_out = _dsa_prompt(r3(qa), r3(qi), r3(kiw), r3(kv))
            b_out, st = _retention(r3(bq), r3(bk), r3(bv), r3(bg), zeros(b, B_HEADS, hd, hd), ret_gain[p],
                                   CHUNK, 0, CHUNK, BF16)
            outs["pa_kv"].append(kv.reshape(b, s, 2, A_KV_HEADS, hd))
            outs["pa_idx"].append(r3(kiw)[:, :, :IDX_DIM])
            outs["p_ret"].append(st)
            xp = _out_proj(xp, a_out.reshape(b * s, -1), b_out.reshape(b * s, -1), wo[:w512], wo[w512:],
                           norm_post_mix[l], mp[2], tm_in, s // tm_in)
            qa, kv, qi, kiw, bq, bk, bv, bg = _in_proj(_ab_in_body, "ab_in", xs, norm_pre_mix[l], msm[0], msm[1],
                                                       [tp_s, tf_s], [], wts, ab_defs(F32), ms, 1)
            r3 = lambda a: a.reshape(db, SLOT, a.shape[-1])
            scores = _dsa_sample_scores(page_table, r3(qi), r3(kiw), _keys_t(r3(kiw)[:, :, :IDX_DIM]), cache_a_idx, p, group_a)
            a_out = _dsa_sample_attend(page_table, r3(qa), scores, _keys_t(r3(kv)), cache_a_kv, p, group_a, topk_s)
            b_out, st = _retention(r3(bq), r3(bk), r3(bv), r3(bg), state_ret[p], ret_gain[p], SLOT, SLOT_LO, hi, F32)
            outs["sa_kv"].append(r3(kv)[:, SLOT_LO:hi].reshape(db, t, 2, A_KV_HEADS, hd))
            outs["sa_idx"].append(r3(kiw)[:, SLOT_LO:hi, :IDX_DIM])
            outs["s_ret"].append(st)
            xs = _out_proj(xs, a_out.reshape(ms, -1), b_out.reshape(ms, -1), wo[:w512], wo[w512:],
                           norm_post_mix[l], msm[2], ms, 1)
        else:
            lam_init = 0.8 - 0.6 * math.exp(-0.3 * l)
            wts = _cd_weights(w_cd_in[p])
            wo = w_cd_out[p].astype(BF16)
            gbias = jnp.pad(c_gate_bias[p], (0, LANES - 2 * C_HEADS)).reshape(1, LANES)
            cq, ck, cv, co, gt, dq, dk, dv = _in_proj(_cd_in_body, "cd_in", xp, norm_pre_mix[l], mp[0], mp[1],
                                                      [tp_p], [gbias], wts, cd_defs(BF16), tm_in, s // tm_in)
            r3 = lambda a: a.reshape(b, s, a.shape[-1])
            c_out, mc, mn, mm = _mlstm(r3(cq), r3(ck), r3(cv), r3(co), r3(gt), zeros(b, C_HEADS, hd, hd),
                                       zeros(b, C_HEADS, hd), zeros(b, C_HEADS), c_gain[p], CHUNK, 0, CHUNK, BF16)
            d_out = _diff_prompt(r3(dq), r3(dk), r3(dv), d_lambda[p], d_subln[p], lam_init)
            outs["pd_k"].append(dk.reshape(b, s, 2, D_HEADS, hd))
            outs["pd_v"].append(dv.reshape(b, s, D_HEADS, D_VDIM))
            outs["p_mc"].append(mc); outs["p_mn"].append(mn); outs["p_mm"].append(mm)
            xp = _out_proj(xp, c_out.reshape(b * s, -1), d_out.reshape(b * s, -1), wo[:w512], wo[w512:],
                           norm_post_mix[l], mp[2], tm_in, s // tm_in)
            cq, ck, cv, co, gt, dq, dk, dv = _in_proj(_cd_in_body, "cd_in", xs, norm_pre_mix[l], msm[0], msm[1],
                                                      [tp_s], [gbias], wts, cd_defs(F32), ms, 1)
            r3 = lambda a: a.reshape(db, SLOT, a.shape[-1])
            c_out, mc, mn, mm = _mlstm(r3(cq), r3(ck), r3(cv), r3(co), r3(gt), state_mlstm_c[p], state_mlstm_n[p],
                                       state_mlstm_m[p], c_gain[p], SLOT, SLOT_LO, hi, F32)
            vnew = _pad_rows(dv.reshape(db, SLOT * D_HEADS, D_VDIM), PAGE_SIZE * D_HEADS)
            d_out = _diff_sample(page_table, r3(dq), _keys_t(r3(dk)), vnew,
                                 d_lambda[p], d_subln[p], cache_d_k, cache_d_v, p, group, lam_init)
            outs["sd_k"].append(r3(dk)[:, SLOT_LO:hi].reshape(db, t, 2, D_HEADS, hd))
            outs["sd_v"].append(r3(dv)[:, SLOT_LO:hi].reshape(db, t, D_HEADS, D_VDIM))
            outs["s_mc"].append(mc); outs["s_mn"].append(mn); outs["s_mm"].append(mm)
            xs = _out_proj(xs, c_out.reshape(ms, -1), d_out.reshape(ms, -1), wo[:w512], wo[w512:],
                           norm_post_mix[l], msm[2], ms, 1)

        xp, tails = _ffn(xp, norm_pre_ffn[l], mp[3], mp[4], w_up, w_down, l, ffn_conv_w[l], ffn_conv_b[l], None,
                         norm_post_ffn[l], mp[5], tm_ffn, s // tm_ffn, tf)
        outs["p_cv"].append(tails.reshape(b, s // tm_ffn, 2, d_ff)[:, -1])
        inj = jnp.pad(state_ffn_conv[l], ((0, 0), (0, SLOT - SLOT_LO), (0, 0))).reshape(ms, d_ff)
        xs, a_all = _ffn(xs, norm_pre_ffn[l], msm[3], msm[4], w_up, w_down, l, ffn_conv_w[l], ffn_conv_b[l], inj,
                         norm_post_ffn[l], msm[5], ms, 1, tf)
        outs["s_cv"].append(a_all.reshape(db, SLOT, d_ff)[:, hi - (CONV_W - 1):hi])

    st = lambda k: jnp.stack(outs[k])
    y_sample = xs.reshape(db, SLOT, d)[:, SLOT_LO:hi]
    return (xp.reshape(b, s, d), y_sample,
            st("pa_kv"), st("pa_idx"), st("pd_k"), st("pd_v"), st("p_ret"), st("p_mc"), st("p_mn"), st("p_mm"), st("p_cv"),
            st("sa_kv"), st("sa_idx"), st("sd_k"), st("sd_v"), st("s_ret"), st("s_mc"), st("s_mn"), st("s_mm"), st("s_cv"))
```

```python
import functools
import math

import numpy as np
import jax
import jax.numpy as jnp
from jax import lax
from jax.experimental import pallas as pl
from jax.experimental.pallas import tpu as pltpu

F32 = jnp.float32
BF16 = jnp.bfloat16

PAGE_SIZE = 128
HEAD_DIM = 64
ROPE_THETA = 500000.0
RET_THETA = 10000.0
A_HEADS = 8
A_KV_HEADS = 2
IDX_HEADS = 4
IDX_DIM = 64
TOPK_MAX = 256
B_HEADS = 8
C_HEADS = 8
D_HEADS = 4
D_VDIM = 2 * HEAD_DIM
CONV_W = 3
CHUNK = 128
EPS = 1e-6
NEG = -1e30
LOG2E = 1.4426950408889634
SLOT = 8
SLOT_LO = CONV_W - 1
LANES = 128
VMEM_LIMIT = 56 * 1024 * 1024

_NT = (((1,), (1,)), ((), ()))


def _cp(*sem):
    return pltpu.CompilerParams(dimension_semantics=sem, vmem_limit_bytes=VMEM_LIMIT)


def _dot(a, b):
    return jnp.dot(a, b, preferred_element_type=F32)


def _dot_nt(a, b):
    return lax.dot_general(a, b, _NT, preferred_element_type=F32)


def _iota(shape, dim):
    return lax.broadcasted_iota(jnp.int32, shape, dim)


def _eye_bf16(n):
    return jnp.where(_iota((n, n), 0) == _iota((n, n), 1), 1.0, 0.0).astype(BF16)


def _dot_tn(a_bf16, b_bf16):
    at = _dot_nt(_eye_bf16(a_bf16.shape[1]), a_bf16).astype(BF16)
    return _dot(at, b_bf16)


def _split3(x):
    hi = x.astype(BF16)
    r1 = x - hi.astype(F32)
    mid = r1.astype(BF16)
    lo = (r1 - mid.astype(F32)).astype(BF16)
    return hi, mid, lo


def _rope_tables(pos, rot, theta):
    half = rot // 2
    inv = theta ** (-np.arange(half, dtype=np.float64) / half)
    ang = np.asarray(pos, np.float64)[:, None] * inv[None]
    cos, sin = np.cos(ang), np.sin(ang)
    n = len(pos)
    t0 = np.ones((n, HEAD_DIM)); t1 = np.zeros((n, HEAD_DIM)); t2 = np.zeros((n, HEAD_DIM))
    t0[:, :half] = cos; t0[:, half:rot] = cos
    t1[:, :half] = -sin
    t2[:, half:rot] = sin
    tab = np.stack([np.tile(t, (1, LANES // HEAD_DIM)) for t in (t0, t1, t2)])
    return jnp.asarray(tab, F32)


def _retention_tables(chunk, lo, hi):
    n = hi - lo
    log_g = np.log1p(-np.exp2(-5.0 - np.arange(B_HEADS, dtype=np.float64)))
    r = np.arange(chunk)
    ok = (r >= lo) & (r < hi)
    pos = (r - lo).astype(np.float64)
    diff = pos[:, None] - pos[None, :]
    dm = np.where((diff >= 0) & ok[:, None] & ok[None, :], np.exp(log_g[:, None, None] * np.maximum(diff, 0.0)), 0.0)
    qd = np.exp(log_g[:, None] * (pos + 1.0))[:, :, None] * np.ones((1, 1, HEAD_DIM))
    kd = np.where(ok, np.exp(log_g[:, None] * (n - 1.0 - pos)), 0.0)[:, :, None] * np.ones((1, 1, HEAD_DIM))
    cd = np.exp(log_g * n)[:, None, None] * np.ones((1, 1, HEAD_DIM))
    return tuple(jnp.asarray(a, F32) for a in (dm, qd, kd, cd))


def _mod_body(c_ref, w_ref, b_ref, o_ref):
    c = c_ref[...]
    s = (c * jax.nn.sigmoid(c)).astype(BF16)
    o_ref[0] = _dot(s, w_ref[0].astype(BF16)) + b_ref[0]


def _modulation(c_all, w_ada, b_ada):
    depth, d, n = w_ada.shape
    r = c_all.shape[0]
    tn = 1536
    return pl.pallas_call(
        _mod_body, grid=(depth, n // tn),
        in_specs=[pl.BlockSpec((r, d), lambda l, j: (0, 0)),
                  pl.BlockSpec((1, d, tn), lambda l, j: (l, 0, j)),
                  pl.BlockSpec((1, 1, tn), lambda l, j: (l, 0, j))],
        out_specs=pl.BlockSpec((1, r, tn), lambda l, j: (l, 0, j)),
        out_shape=jax.ShapeDtypeStruct((depth, r, n), F32),
        compiler_params=_cp("arbitrary", "arbitrary"), name="modulation",
    )(c_all, w_ada, b_ada.reshape(depth, 1, n))


def _rmsnorm(x, g):
    return x * lax.rsqrt(jnp.mean(x * x, axis=-1, keepdims=True) + EPS) * g


def _ada_norm_bf16(x_ref, g_ref, sh_ref, sc_ref):
    return (_rmsnorm(x_ref[...], g_ref[...]) * (1.0 + sc_ref[0]) + sh_ref[0]).astype(BF16)


def _rope128(x, tab, shift):
    xl = pltpu.roll(x, LANES - shift, 1)
    xr = pltpu.roll(x, shift, 1)
    return x * tab[0] + xl * tab[1] + xr * tab[2]


def _rope(u, tab, shift):
    return jnp.concatenate(
        [_rope128(u[:, c * LANES:(c + 1) * LANES], tab, shift) for c in range(u.shape[1] // LANES)], axis=1)


def _ab_in_body(x_ref, g_ref, sh_ref, sc_ref, tp_ref, tf_ref,
                wqa, wkv, wqi, wki, wbq, wbk, wbv, wbg,
                qa_o, kv_o, qi_o, kiw_o, bq_o, bk_o, bv_o, bg_o):
    h = _ada_norm_bf16(x_ref, g_ref, sh_ref, sc_ref)
    tp = tp_ref[...]
    tf = tf_ref[...]
    pshift, fshift = HEAD_DIM // 8, HEAD_DIM // 2
    qa_o[...] = _rope(_dot(h, wqa[...]), tp, pshift).astype(qa_o.dtype)
    u = _dot(h, wkv[...])
    kv_o[...] = jnp.concatenate([_rope128(u[:, :LANES], tp, pshift), u[:, LANES:]], axis=1)
    qi_o[...] = _rope(_dot(h, wqi[...]), tp, pshift).astype(qi_o.dtype)
    u = _dot(h, wki[...])
    kiw_o[...] = jnp.where(_iota(u.shape, 1) < IDX_DIM, _rope128(u, tp, pshift), u)
    bq_o[...] = _rope(_dot(h, wbq[...]), tf, fshift).astype(bq_o.dtype)
    bk_o[...] = _rope(_dot(h, wbk[...]), tf, fshift) * HEAD_DIM ** -0.5
    bv_o[...] = _dot(h, wbv[...]).astype(bv_o.dtype)
    bg_o[...] = _dot(h, wbg[...])


def _cd_in_body(x_ref, g_ref, sh_ref, sc_ref, tp_ref, gb_ref,
                wcq, wck, wcv, wco, wgt, wdq, wdk, wdv,
                cq_o, ck_o, cv_o, co_o, gt_o, dq_o, dk_o, dv_o):
    h = _ada_norm_bf16(x_ref, g_ref, sh_ref, sc_ref)
    tp = tp_ref[...]
    pshift = HEAD_DIM // 8
    cq_o[...] = _dot(h, wcq[...]).astype(cq_o.dtype)
    ck_o[...] = _dot(h, wck[...]) * HEAD_DIM ** -0.5
    cv_o[...] = _dot(h, wcv[...]).astype(cv_o.dtype)
    co_o[...] = _dot(h, wco[...])
    gt_o[...] = _dot(h, wgt[...]) + gb_ref[...]
    dq_o[...] = _rope(_dot(h, wdq[...]), tp, pshift).astype(dq_o.dtype)
    dk_o[...] = _rope(_dot(h, wdk[...]), tp, pshift)
    dv_o[...] = _dot(h, wdv[...])


def _in_proj(body, name, x, g, shift, scale, tables, extra, weights, out_defs, tm, tps):
    m, d = x.shape
    r = shift.shape[1]
    mod_spec = pl.BlockSpec((1, r, d), lambda i: (i // tps, 0, 0))
    in_specs = [pl.BlockSpec((tm, d), lambda i: (i, 0)), pl.BlockSpec((1, d), lambda i: (0, 0)), mod_spec, mod_spec]
    in_specs += [pl.BlockSpec((3, tm, LANES), lambda i: (0, i % tps, 0)) for _ in tables]
    in_specs += [pl.BlockSpec(e.shape, lambda i: (0, 0)) for e in extra]
    in_specs += [pl.BlockSpec(w.shape, lambda i: (0, 0)) for w in weights]
    return pl.pallas_call(
        body, grid=(m // tm,), in_specs=in_specs,
        out_specs=[pl.BlockSpec((tm, n), lambda i: (i, 0)) for n, _ in out_defs],
        out_shape=[jax.ShapeDtypeStruct((m, n), dt) for n, dt in out_defs],
        compiler_params=_cp("arbitrary"), name=name,
    )(x, g.reshape(1, d), shift, scale, *tables, *extra, *weights)


def _ffn_body(x_ref, ma_ref, mb_ref, wo_ref, g0_ref, gate0_ref, g1_ref, sh_ref, sc_ref, wu_ref, wd_ref, cw_ref, cb_ref,
              inj_ref, g2_ref, gate_ref, o_ref, aux_ref, carry_sc, *, tps, tf, inject):
    i = pl.program_id(0)
    tm = x_ref.shape[0]
    ff = wd_ref.shape[0]
    mix = jnp.concatenate([ma_ref[...].astype(BF16), mb_ref[...].astype(BF16)], axis=1)
    x = x_ref[...] + gate0_ref[0] * _rmsnorm(_dot(mix, wo_ref[...]), g0_ref[...])
    h = (_rmsnorm(x, g1_ref[...]) * (1.0 + sc_ref[0]) + sh_ref[0]).astype(BF16)
    row = _iota((tm, tf), 0)
    cw = cw_ref[...]
    cb = cb_ref[...]
    if not inject:
        @pl.when(i % tps == 0)
        def _():
            carry_sc[...] = jnp.zeros(carry_sc.shape, F32)

    def up(f):
        return _dot(h, wu_ref[:, f * tf:(f + 1) * tf]), _dot(h, wu_ref[:, ff + f * tf:ff + (f + 1) * tf])

    yv = None
    nf = ff // tf
    nxt = up(0)
    for f in range(nf):
        fs = slice(f * tf, (f + 1) * tf)
        a, b = nxt
        if f + 1 < nf:
            nxt = up(f + 1)
        if inject:
            a = jnp.where(row % SLOT < SLOT_LO, inj_ref[:, fs], a)
            aux_ref[:, fs] = a
            p1 = jnp.zeros((1, tf), F32)
            p2 = jnp.zeros((2, tf), F32)
        else:
            p2 = carry_sc[:, fs]
            carry_sc[:, fs] = a[tm - 2:, :]
            aux_ref[0, :, fs] = a[tm - 2:, :]
            p1 = p2[1:2, :]
        a1 = jnp.where(row == 0, p1, pltpu.roll(a, 1, 0))
        a2 = pltpu.roll(a, 2, 0)
        a2 = jnp.where(row == 0, p2[0:1, :], jnp.where(row == 1, p2[1:2, :], a2))
        conv = cb[:, fs] + a2 * cw[0:1, fs]
        conv = conv + a1 * cw[1:2, fs]
        conv = conv + a * cw[2:3, fs]
        y = jax.nn.gelu(conv, approximate=True) * b
        part = _dot(y.astype(BF16), wd_ref[fs, :])
        yv = part if yv is None else yv + part
    o_ref[...] = x + gate_ref[0] * _rmsnorm(yv, g2_ref[...])


def _ffn(x, mix_a, mix_b, w_out, g0, gate0, g1, shift, scale, w_up, w_down, layer, cw, cb, inj, g2, gate, tm, tps, tf):
    m, d = x.shape
    ff = w_down.shape[1]
    r = shift.shape[1]
    inject = inj is not None
    if inject:
        aux_shape, aux_spec = (m, ff), pl.BlockSpec((tm, ff), lambda i: (i, 0))
        inj_spec = pl.BlockSpec((tm, ff), lambda i: (i, 0))
    else:
        aux_shape, aux_spec = (m // tm, 2, ff), pl.BlockSpec((1, 2, ff), lambda i: (i, 0, 0))
        inj = jnp.zeros((8, LANES), F32)
        inj_spec = pl.BlockSpec((8, LANES), lambda i: (0, 0))
    mod_spec = pl.BlockSpec((1, r, d), lambda i: (i // tps, 0, 0))
    fix = lambda i: (0, 0)
    return pl.pallas_call(
        functools.partial(_ffn_body, tps=tps, tf=tf, inject=inject),
        grid=(m // tm,),
        in_specs=[pl.BlockSpec((tm, d), lambda i: (i, 0)),
                  pl.BlockSpec((tm, mix_a.shape[1]), lambda i: (i, 0)), pl.BlockSpec((tm, mix_b.shape[1]), lambda i: (i, 0)),
                  pl.BlockSpec(w_out.shape, fix), pl.BlockSpec((1, d), fix), mod_spec,
                  pl.BlockSpec((1, d), fix), mod_spec, mod_spec,
                  pl.BlockSpec((None, d, 2 * ff), lambda i: (layer, 0, 0)),
                  pl.BlockSpec((None, ff, d), lambda i: (layer, 0, 0)),
                  pl.BlockSpec((CONV_W, ff), fix), pl.BlockSpec((1, ff), fix),
                  inj_spec, pl.BlockSpec((1, d), fix), mod_spec],
        out_specs=[pl.BlockSpec((tm, d), lambda i: (i, 0)), aux_spec],
        out_shape=[jax.ShapeDtypeStruct((m, d), F32), jax.ShapeDtypeStruct(aux_shape, F32)],
        scratch_shapes=[pltpu.VMEM((2, ff), F32)],
        compiler_params=_cp("arbitrary"), name="mix_out_conv_ffn",
    )(x, mix_a, mix_b, w_out, g0.reshape(1, d), gate0, g1.reshape(1, d), shift, scale, w_up, w_down, cw,
      cb.reshape(1, ff), inj, g2.reshape(1, d), gate)


def _kth_largest(stats_fn, cmin0, cmax0, k):
    def flag(lo, hi):
        return jnp.max(jnp.where(lo < hi, 1.0, 0.0))

    def body(st):
        cmin, cmax, _ = st
        mid = cmin + 0.5 * (cmax - cmin)
        p = jnp.where(mid > cmin, jnp.where(mid <= cmax, mid, cmax), cmax)
        cnt, mn_ge, mx_lt = stats_fn(p)
        is_open = cmin < cmax
        take_lo = cnt >= k
        ncmin = jnp.where(is_open, jnp.where(take_lo, mn_ge, cmin), cmin)
        ncmax = jnp.where(is_open, jnp.where(take_lo, cmax, mx_lt), cmax)
        return ncmin, ncmax, flag(ncmin, ncmax)

    return lax.while_loop(lambda st: st[2] > 0.0, body, (cmin0, cmax0, flag(cmin0, cmax0)))[0]


def _kth_largest_bracketed(count_fn, stats_fn, row_min, row_max, k, light_passes):
    def light(i, st):
        lo, hi = st
        mid = lo + 0.5 * (hi - lo)
        take = count_fn(mid) >= k
        return jnp.where(take, mid, lo), jnp.where(take, hi, mid)

    lo, hi = lax.fori_loop(0, light_passes, light, (row_min, row_max))
    _, cmin, _ = stats_fn(lo)
    cnt_hi, mn_hi, mx_hi = stats_fn(hi)
    at_top = cnt_hi >= k
    return _kth_largest(stats_fn, jnp.where(at_top, mn_hi, cmin), jnp.where(at_top, row_max, mx_hi), k)


def _lane_blocks(s):
    return [s[:, c * LANES:(c + 1) * LANES] for c in range(s.shape[1] // LANES)]


def _softmax_probs(blocks, m_prev, l_prev, exp=jnp.exp):
    mloc = functools.reduce(jnp.maximum, blocks)
    m_new = jnp.maximum(m_prev, jnp.max(mloc, axis=1, keepdims=True))
    alpha = exp(m_prev - m_new)
    ps = [exp(b - m_new) for b in blocks]
    l_new = alpha * l_prev + jnp.sum(functools.reduce(jnp.add, ps), axis=1, keepdims=True)
    p = (jnp.concatenate(ps, axis=1) if len(ps) > 1 else ps[0]).astype(BF16)
    return p, alpha, m_new, l_new


def _scale_acc(alpha, acc):
    dv = acc.shape[-1]
    if dv < LANES:
        return alpha[:, :dv] * acc
    return (alpha if dv == LANES else jnp.concatenate([alpha] * (dv // LANES), axis=1)) * acc


def _softmax_step(s, v_bf16, m_ref, l_ref, acc_ref, idx, values_transposed=False, exp=jnp.exp):
    p, alpha, m_new, l_new = _softmax_probs(_lane_blocks(s), m_ref[idx], l_ref[idx], exp)
    pv = _dot_nt(p, v_bf16) if values_transposed else _dot(p, v_bf16)
    acc_ref[idx] = _scale_acc(alpha, acc_ref[idx]) + pv
    m_ref[idx] = m_new
    l_ref[idx] = l_new


def _tie_bias(x, t, need, carry, tri):
    eq = x == t
    pref = _dot(jnp.where(eq, 1.0, 0.0).astype(BF16), tri) + carry
    bias = jnp.where(x > t, 0.0, jnp.where(eq, jnp.where(pref <= need, 0.0, NEG), NEG))
    return bias, pref[:, x.shape[1] - 1:]


def _fold8(x, op):
    parts = [x[i * 8:(i + 1) * 8] for i in range(x.shape[0] // 8)]
    while len(parts) > 1:
        parts = [op(parts[i], parts[i + 1]) for i in range(0, len(parts) - 1, 2)] + (parts[-1:] if len(parts) % 2 else [])
    return parts[0]


def _dsa_prompt_body(qa_ref, qi_ref, kiwq_ref, kiwk_ref, kv_ref, tril_ref, o_ref,
                     sct, sc, m_sc, l_sc, acc_sc, *, tq, ch, topk, light_passes):
    j = pl.program_id(1)
    q0 = j * tq
    nc = (q0 + tq + ch - 1) // ch
    qpos = q0 + _iota((1, tq), 1)
    kf = float(topk)
    inf = jnp.inf
    qi = qi_ref[0] * IDX_DIM ** -0.5
    qi4 = jnp.concatenate([qi[:, h * IDX_DIM:(h + 1) * IDX_DIM] for h in range(IDX_HEADS)], axis=0).astype(BF16)
    pick = jnp.where(_iota((8, LANES), 1) == _iota((8, LANES), 0) + IDX_DIM, 1.0, 0.0).astype(BF16)
    wt = sum(_dot_nt(pick, part) for part in _split3(kiwq_ref[0])) * IDX_HEADS ** -0.5

    def score_chunk(c, st):
        mx, mn = st
        c0 = pl.multiple_of(c * ch, ch)
        kc = kiwk_ref[0, pl.ds(c0, ch), :][:, :IDX_DIM].astype(BF16)
        lg = _dot_nt(kc, qi4)
        s = jnp.zeros((ch, tq), F32)
        for h in range(IDX_HEADS):
            s = s + jnp.maximum(lg[:, h * tq:(h + 1) * tq], 0.0) * wt[h:h + 1, :]
        s = jnp.where(s == 0.0, 0.0, s)
        valid = (c0 + _iota((ch, tq), 0)) <= qpos
        sct[c] = jnp.where(valid, s, -inf)
        mx = jnp.maximum(mx, _fold8(jnp.where(valid, s, -inf), jnp.maximum))
        mn = jnp.minimum(mn, _fold8(jnp.where(valid, s, inf), jnp.minimum))
        return mx, mn

    mx, mn = lax.fori_loop(0, nc, score_chunk, (jnp.full((8, tq), -inf, F32), jnp.full((8, tq), inf, F32)))
    active = qpos + 1 > topk
    row_max = jnp.where(active, jnp.max(mx, axis=0, keepdims=True), 0.0)
    row_min = jnp.where(active, jnp.min(mn, axis=0, keepdims=True), 0.0)

    def count_ge(p):
        def body(c, cnt):
            return cnt + _fold8(jnp.where(sct[c] >= p, 1.0, 0.0), jnp.add)
        return jnp.sum(lax.fori_loop(0, nc, body, jnp.zeros((8, tq), F32)), axis=0, keepdims=True)

    def stats(p):
        def body(c, st):
            cnt, mnge, mxlt = st
            x = sct[c]
            ge = x >= p
            return (cnt + _fold8(jnp.where(ge, 1.0, 0.0), jnp.add),
                    jnp.minimum(mnge, _fold8(jnp.where(ge, x, inf), jnp.minimum)),
                    jnp.maximum(mxlt, _fold8(jnp.where(ge, -inf, x), jnp.maximum)))
        init = (jnp.zeros((8, tq), F32), jnp.full((8, tq), inf, F32), jnp.full((8, tq), -inf, F32))
        cnt, mnge, mxlt = lax.fori_loop(0, nc, body, init)
        return (jnp.sum(cnt, axis=0, keepdims=True), jnp.min(mnge, axis=0, keepdims=True),
                jnp.max(mxlt, axis=0, keepdims=True))

    t = jnp.where(active, _kth_largest_bracketed(count_ge, stats, row_min, row_max, kf, light_passes), -inf)

    def count_sel(c, st):
        x = sct[c]
        return (st[0] + _fold8(jnp.where(x > t, 1.0, 0.0), jnp.add),
                st[1] + _fold8(jnp.where(x >= t, 1.0, 0.0), jnp.add))

    gt, ge = lax.fori_loop(0, nc, count_sel, (jnp.zeros((8, tq), F32), jnp.zeros((8, tq), F32)))
    need = kf - jnp.sum(gt, axis=0, keepdims=True)
    over = jnp.sum(ge, axis=0, keepdims=True) > kf
    has_ties = jnp.max(jnp.where(active, jnp.where(over, 1.0, 0.0), 0.0)) > 0.0
    eye = _eye_bf16(tq)

    def store_bias(c, sel_t):
        sel = _dot_nt(eye, sel_t.astype(BF16))
        sc[c] = jnp.where(sel > 0.5, 0.0, NEG)

    @pl.when(has_ties)
    def _():
        tril = tril_ref[...]

        def tie_chunk(c, carry):
            x = sct[c]
            eq = x == t
            pref = _dot(tril, jnp.where(eq, 1.0, 0.0).astype(BF16)) + carry
            store_bias(c, jnp.where(x > t, 1.0, jnp.where(eq, jnp.where(pref <= need, 1.0, 0.0), 0.0)))
            return pref[ch - 1:, :]

        lax.fori_loop(0, nc, tie_chunk, jnp.zeros((1, tq), F32))

    @pl.when(jnp.logical_not(has_ties))
    def _():
        def plain_chunk(c, carry):
            store_bias(c, jnp.where(sct[c] >= t, 1.0, 0.0))
            return carry

        lax.fori_loop(0, nc, plain_chunk, 0)

    cl = nc - 1
    sc[cl] = jnp.where((cl * ch + _iota((tq, ch), 1)) <= q0 + _iota((tq, ch), 0), sc[cl], NEG)

    qa = qa_ref[0].astype(F32) * (HEAD_DIM ** -0.5 * LOG2E)
    hpg = A_HEADS // A_KV_HEADS
    qh = [qa[:, h * HEAD_DIM:(h + 1) * HEAD_DIM].astype(BF16) for h in range(A_HEADS)]
    m_sc[...] = jnp.full(m_sc.shape, NEG, F32)
    l_sc[...] = jnp.zeros(l_sc.shape, F32)
    acc_sc[...] = jnp.zeros(acc_sc.shape, F32)

    def attend(c, carry):
        c0 = pl.multiple_of(c * ch, ch)
        bias = _lane_blocks(sc[c])
        kvc = kv_ref[0, pl.ds(c0, ch), :]
        kcs = [kvc[:, g * HEAD_DIM:(g + 1) * HEAD_DIM].astype(BF16) for g in range(A_KV_HEADS)]
        vcs = [kvc[:, (A_KV_HEADS + g) * HEAD_DIM:(A_KV_HEADS + g + 1) * HEAD_DIM].astype(BF16) for g in range(A_KV_HEADS)]
        ss = [_dot_nt(qh[hh], kcs[hh // hpg]) for hh in range(A_HEADS)]
        sm = []
        for hh in range(A_HEADS):
            g, rs = hh // hpg, pl.ds((hh % hpg) * tq, tq)
            blocks = [sb + bb for sb, bb in zip(_lane_blocks(ss[hh]), bias)]
            sm.append(_softmax_probs(blocks, m_sc[g, rs, :], l_sc[g, rs, :], exp=jnp.exp2))
        pv = [_dot(jnp.concatenate([sm[g * hpg + h][0] for h in range(hpg)], axis=0), vcs[g]) for g in range(A_KV_HEADS)]
        for hh in range(A_HEADS):
            g, rs = hh // hpg, pl.ds((hh % hpg) * tq, tq)
            m_sc[g, rs, :] = sm[hh][2]
            l_sc[g, rs, :] = sm[hh][3]
        for g in range(A_KV_HEADS):
            alpha = jnp.concatenate([sm[g * hpg + h][1] for h in range(hpg)], axis=0)
            acc_sc[g] = _scale_acc(alpha, acc_sc[g]) + pv[g]
        return carry

    lax.fori_loop(0, nc, attend, 0)
    outs = []
    for g in range(A_KV_HEADS):
        og = acc_sc[g] / l_sc[g][:, :HEAD_DIM]
        outs += [og[h * tq:(h + 1) * tq] for h in range(hpg)]
    o_ref[0] = jnp.concatenate(outs, axis=1).astype(o_ref.dtype)


def _upper_tri_bf16(n):
    return jnp.asarray(np.triu(np.ones((n, n), np.float32)), BF16)


def _dsa_prompt(qa, qi, kiw, kv, tq=128, ch=512):
    b, s, _ = qa.shape
    ch = min(ch, s)
    topk = min(TOPK_MAX, s // 4)
    hpg = A_HEADS // A_KV_HEADS
    return pl.pallas_call(
        functools.partial(_dsa_prompt_body, tq=tq, ch=ch, topk=topk, light_passes=14),
        grid=(b, s // tq),
        in_specs=[pl.BlockSpec((1, tq, qa.shape[2]), lambda i, j: (i, j, 0)),
                  pl.BlockSpec((1, tq, qi.shape[2]), lambda i, j: (i, j, 0)),
                  pl.BlockSpec((1, tq, LANES), lambda i, j: (i, j, 0)),
                  pl.BlockSpec((1, s, LANES), lambda i, j: (i, 0, 0)),
                  pl.BlockSpec((1, s, kv.shape[2]), lambda i, j: (i, 0, 0)),
                  pl.BlockSpec((ch, ch), lambda i, j: (0, 0))],
        out_specs=pl.BlockSpec((1, tq, A_HEADS * HEAD_DIM), lambda i, j: (i, j, 0)),
        out_shape=jax.ShapeDtypeStruct((b, s, A_HEADS * HEAD_DIM), BF16),
        scratch_shapes=[pltpu.VMEM((s // ch, ch, tq), F32),
                        pltpu.VMEM((s // ch, tq, ch), F32),
                        pltpu.VMEM((A_KV_HEADS, hpg * tq, LANES), F32),
                        pltpu.VMEM((A_KV_HEADS, hpg * tq, LANES), F32),
                        pltpu.VMEM((A_KV_HEADS, hpg * tq, HEAD_DIM), F32)],
        compiler_params=_cp("arbitrary", "arbitrary"), name="dsa_prompt",
    )(qa, qi, kiw, kiw, kv, jnp.asarray(np.tril(np.ones((ch, ch), np.float32)), BF16))


def _head_avg_matrix(width):
    seg = np.arange(width) // HEAD_DIM
    return jnp.asarray((seg[:, None] == seg[None, :]) / HEAD_DIM, BF16)


def _head_mean(x, avg):
    hi = x.astype(BF16)
    lo = (x - hi.astype(F32)).astype(BF16)
    return _dot(hi, avg) + _dot(lo, avg)


def _head_layernorm(o, gain, avg):
    d = o - _head_mean(o, avg)
    return d * lax.rsqrt(_head_mean(d * d, avg) + EPS) * gain


def _retention_body(q_ref, k_ref, v_ref, g_ref, s0_ref, dm_ref, qd_ref, kd_ref, cd_ref, gain_ref, avg_ref,
                    o_ref, sf_ref, s_sc):
    c = pl.program_id(1)

    @pl.when(c == 0)
    def _():
        s_sc[...] = s0_ref[0]

    q = q_ref[0].astype(BF16)
    k = k_ref[0]
    v = v_ref[0].astype(BF16)
    gt = g_ref[0]
    heads = range(B_HEADS)
    sls = [slice(h * HEAD_DIM, (h + 1) * HEAD_DIM) for h in heads]
    eye = _eye_bf16(HEAD_DIM)
    st = [s_sc[h] for h in heads]
    att = [_dot_nt(q[:, sl], k[:, sl].astype(BF16)) for sl in sls]
    qs = [_dot(q[:, sls[h]], st[h].astype(BF16)) for h in heads]
    kt = [_dot_nt(eye, (k[:, sls[h]] * kd_ref[h]).astype(BF16)) for h in heads]
    outs = [_dot((att[h] * dm_ref[h]).astype(BF16), v[:, sls[h]]) + qs[h] * qd_ref[h] for h in heads]
    for h in heads:
        s_sc[h] = st[h] * cd_ref[h] + _dot(kt[h].astype(BF16), v[:, sls[h]])
    o = _head_layernorm(jnp.concatenate(outs, axis=1), gain_ref[...], avg_ref[...])
    o_ref[0] = (gt * jax.nn.sigmoid(gt) * o).astype(o_ref.dtype)

    @pl.when(c == pl.num_programs(1) - 1)
    def _():
        sf_ref[0] = s_sc[...]


def _retention(q, k, v, g, s0, gain, chunk, lo, hi, out_dtype):
    b, s, w = q.shape
    dm, qd, kd, cd = _retention_tables(chunk, lo, hi)
    row = lambda i, c: (i, c, 0)
    fix3 = lambda i, c: (0, 0, 0)
    st_spec = pl.BlockSpec((1, B_HEADS, HEAD_DIM, HEAD_DIM), lambda i, c: (i, 0, 0, 0))
    return pl.pallas_call(
        _retention_body, grid=(b, s // chunk),
        in_specs=[pl.BlockSpec((1, chunk, w), row)] * 4 + [st_spec] +
                 [pl.BlockSpec(t.shape, fix3) for t in (dm, qd, kd, cd)] +
                 [pl.BlockSpec((1, w), lambda i, c: (0, 0)), pl.BlockSpec((w, w), lambda i, c: (0, 0))],
        out_specs=[pl.BlockSpec((1, chunk, w), row), st_spec],
        out_shape=[jax.ShapeDtypeStruct((b, s, w), out_dtype),
                   jax.ShapeDtypeStruct((b, B_HEADS, HEAD_DIM, HEAD_DIM), F32)],
        scratch_shapes=[pltpu.VMEM((B_HEADS, HEAD_DIM, HEAD_DIM), F32)],
        compiler_params=_cp("arbitrary", "arbitrary"), name="retention",
    )(q, k, v, g, s0, dm, qd, kd, cd, gain.reshape(1, w), _head_avg_matrix(w))


def _mlstm_body(q_ref, k_ref, v_ref, og_ref, gt_ref, cn0_ref, m0_ref, gain_ref, avg_ref,
                o_ref, cnf_ref, mf_ref, cn_sc, m_sc, *, lo, hi):
    c = pl.program_id(1)
    ck = q_ref.shape[1]
    nh = C_HEADS
    hd = HEAD_DIM

    @pl.when(c == 0)
    def _():
        cn_sc[...] = cn0_ref[0]
        m_sc[...] = m0_ref[0]

    q = q_ref[0].astype(BF16)
    k = k_ref[0]
    v = v_ref[0].astype(BF16)
    gates = gt_ref[0]
    row = _iota((ck, LANES), 0)
    tok = (row >= lo) & (row < hi)
    log_sig = jnp.minimum(gates, 0.0) - jnp.log1p(jnp.exp(-jnp.abs(gates)))
    lf = jnp.where(tok, log_sig, 0.0)
    ii = jnp.where(tok, gates, NEG)
    tril = jnp.where(_iota((ck, ck), 0) >= _iota((ck, ck), 1), 1.0, 0.0).astype(BF16)
    fcum = sum(_dot(tril, part) for part in _split3(lf))
    f_all = pltpu.roll(fcum, LANES - nh, 1)
    a_all = ii - f_all
    cm = a_all
    step = 1
    while step < ck:
        cm = jnp.maximum(cm, jnp.where(row >= step, pltpu.roll(cm, step, 0), -jnp.inf))
        step *= 2
    m_prev = m_sc[...]
    m_t = f_all + jnp.maximum(m_prev, cm)
    inter_all = jnp.exp(f_all + m_prev - m_t)
    floor_all = jnp.exp(-m_t)
    g_all = f_all - m_t
    f_last = f_all[ck - 1:, :]
    m_end = m_t[ck - 1:, :]
    w_all = jnp.exp(f_last + a_all - m_end)
    dec_all = jnp.exp(f_last + m_prev - m_end)
    m_sc[...] = m_end
    pick = jnp.where(_iota((nh, LANES), 0) == _iota((nh, LANES), 1), 1.0, 0.0).astype(BF16)
    a_rows = sum(_dot_nt(pick, part) for part in _split3(a_all))
    causal = _iota((ck, ck), 0) >= _iota((ck, ck), 1)
    eye = _eye_bf16(hd)
    ones = jnp.ones((ck, hd), BF16)
    n_lanes = _iota((hd, 2 * hd), 1) >= hd
    heads = range(nh)
    sls = [slice(h * hd, (h + 1) * hd) for h in heads]
    col = lambda x, h: x[:, h:h + 1]
    cn = [cn_sc[h] for h in heads]
    v1 = [jnp.concatenate([v[:, sl], ones], axis=1) for sl in sls]
    qk = [_dot_nt(q[:, sl], k[:, sl].astype(BF16)) for sl in sls]
    qc = [_dot(q[:, sls[h]], cn[h].astype(BF16)) for h in heads]
    kw = [k[:, sls[h]] * col(w_all, h) for h in heads]
    kw_hi = [x.astype(BF16) for x in kw]
    kw_lo = [(x - y.astype(F32)).astype(BF16) for x, y in zip(kw, kw_hi)]
    kt_hi = [_dot_nt(eye, x) for x in kw_hi]
    kt_lo = [_dot_nt(eye, x) for x in kw_lo]
    dmat = [jnp.exp(jnp.where(causal, col(g_all, h) + a_rows[h:h + 1, :], -jnp.inf)) for h in heads]
    nd = [_dot((qk[h] * dmat[h]).astype(BF16), v1[h]) + col(inter_all, h) * qc[h] for h in heads]
    outs = [(nd[h] / jnp.maximum(jnp.abs(pltpu.roll(nd[h], hd, 1)), col(floor_all, h)))[:, :hd] for h in heads]
    upd = [_dot(kt_hi[h].astype(BF16), v1[h]) for h in heads]
    fix = [_dot(kt_lo[h].astype(BF16), v1[h]) for h in heads]
    for h in heads:
        cn_sc[h] = col(dec_all, h) * cn[h] + upd[h] + jnp.where(n_lanes, fix[h], 0.0)
    hc = jax.nn.sigmoid(og_ref[0]) * jnp.concatenate(outs, axis=1)
    o_ref[0] = _head_layernorm(hc, gain_ref[...], avg_ref[...]).astype(o_ref.dtype)

    @pl.when(c == pl.num_programs(1) - 1)
    def _():
        cnf_ref[0] = cn_sc[...]
        mf_ref[0] = m_sc[...]


def _mlstm(q, k, v, og, gates, c0, n0, m0, gain, chunk, lo, hi, out_dtype):
    b, s, w = q.shape
    nh = C_HEADS
    row = lambda i, c: (i, c, 0)
    hd = HEAD_DIM
    cn_spec = pl.BlockSpec((1, nh, hd, 2 * hd), lambda i, c: (i, 0, 0, 0))
    m_spec = pl.BlockSpec((1, 1, LANES), lambda i, c: (i, 0, 0))
    cn0 = jnp.concatenate([c0, jnp.broadcast_to(n0[..., None], (b, nh, hd, hd))], axis=-1)
    m0p = jnp.pad(m0, ((0, 0), (0, LANES - nh))).reshape(b, 1, LANES)
    o, cnf, mf = pl.pallas_call(
        functools.partial(_mlstm_body, lo=lo, hi=hi), grid=(b, s // chunk),
        in_specs=[pl.BlockSpec((1, chunk, w), row)] * 4 + [pl.BlockSpec((1, chunk, LANES), row), cn_spec, m_spec,
                                                          pl.BlockSpec((1, w), lambda i, c: (0, 0)),
                                                          pl.BlockSpec((w, w), lambda i, c: (0, 0))],
        out_specs=[pl.BlockSpec((1, chunk, w), row), cn_spec, m_spec],
        out_shape=[jax.ShapeDtypeStruct((b, s, w), out_dtype),
                   jax.ShapeDtypeStruct((b, nh, hd, 2 * hd), F32),
                   jax.ShapeDtypeStruct((b, 1, LANES), F32)],
        scratch_shapes=[pltpu.VMEM((nh, hd, 2 * hd), F32), pltpu.VMEM((1, LANES), F32)],
        compiler_params=_cp("arbitrary", "arbitrary"), name="mlstm",
    )(q, k, v, og, gates, cn0, m0p, gain.reshape(1, w), _head_avg_matrix(w))
    return o, cnf[..., :hd], cnf[..., hd], mf[:, 0, :nh]


def _diff_lambda(lam_ref, lam_init):
    lp = lam_ref[...]
    s01 = jnp.sum(lp[0:1] * lp[1:2], axis=1, keepdims=True)
    s23 = jnp.sum(lp[2:3] * lp[3:4], axis=1, keepdims=True)
    return jnp.exp(s01) - jnp.exp(s23) + lam_init


def _diff_finish(o0, l0, o1, l1, lam, subln, lam_init):
    od = o0 / l0 - lam * (o1 / l1)
    return od * lax.rsqrt(jnp.mean(od * od, axis=-1, keepdims=True) + EPS) * subln * (1.0 - lam_init)


def _diff_prompt_body(q_ref, k_ref, v_ref, lam_ref, sub_ref, o_ref, m_sc, l_sc, acc_sc, *, lam_init):
    i = pl.program_id(1)
    j = pl.program_id(2)
    t = q_ref.shape[1]

    @pl.when(j == 0)
    def _():
        m_sc[...] = jnp.full(m_sc.shape, NEG, F32)
        l_sc[...] = jnp.zeros(l_sc.shape, F32)
        acc_sc[...] = jnp.zeros(acc_sc.shape, F32)

    def step(diag):
        q = (q_ref[0].astype(F32) * (HEAD_DIM ** -0.5 * LOG2E)).astype(BF16)
        k = k_ref[0].astype(BF16)
        v = v_ref[0].astype(BF16)
        keep = (_iota((t, t), 0) >= _iota((t, t), 1)) if diag else None
        pairs = range(2 * D_HEADS)
        ss = [_dot_nt(q[:, idx * HEAD_DIM:(idx + 1) * HEAD_DIM], k[:, idx * HEAD_DIM:(idx + 1) * HEAD_DIM]) for idx in pairs]
        if diag:
            ss = [jnp.where(keep, s, NEG) for s in ss]
        sm = [_softmax_probs(_lane_blocks(ss[idx]), m_sc[idx], l_sc[idx], jnp.exp2) for idx in pairs]
        pv = [_dot(sm[idx][0], v[:, (idx % D_HEADS) * D_VDIM:(idx % D_HEADS + 1) * D_VDIM]) for idx in pairs]
        for idx in pairs:
            _, alpha, m_new, l_new = sm[idx]
            acc_sc[idx] = _scale_acc(alpha, acc_sc[idx]) + pv[idx]
            m_sc[idx] = m_new
            l_sc[idx] = l_new

    @pl.when(j < i)
    def _():
        step(False)

    @pl.when(j == i)
    def _():
        step(True)
        lam = _diff_lambda(lam_ref, lam_init)
        outs = [_diff_finish(acc_sc[h], l_sc[h], acc_sc[D_HEADS + h], l_sc[D_HEADS + h], lam, sub_ref[...], lam_init)
                for h in range(D_HEADS)]
        o_ref[0] = jnp.concatenate(outs, axis=1).astype(o_ref.dtype)


def _diff_prompt(q, k, v, lam_par, subln, lam_init, t=512):
    b, s, w = q.shape
    t = min(t, s)
    n = s // t
    return pl.pallas_call(
        functools.partial(_diff_prompt_body, lam_init=lam_init), grid=(b, n, n),
        in_specs=[pl.BlockSpec((1, t, w), lambda bi, i, j: (bi, i, 0)),
                  pl.BlockSpec((1, t, w), lambda bi, i, j: (bi, jnp.minimum(i, j), 0)),
                  pl.BlockSpec((1, t, w), lambda bi, i, j: (bi, jnp.minimum(i, j), 0)),
                  pl.BlockSpec(lam_par.shape, lambda bi, i, j: (0, 0)),
                  pl.BlockSpec((1, D_VDIM), lambda bi, i, j: (0, 0))],
        out_specs=pl.BlockSpec((1, t, w), lambda bi, i, j: (bi, i, 0)),
        out_shape=jax.ShapeDtypeStruct((b, s, w), BF16),
        scratch_shapes=[pltpu.VMEM((2 * D_HEADS, t, LANES), F32), pltpu.VMEM((2 * D_HEADS, t, LANES), F32),
                        pltpu.VMEM((2 * D_HEADS, t, D_VDIM), F32)],
        compiler_params=_cp("arbitrary", "arbitrary", "arbitrary"), name="diff_prompt",
    )(q, k, v, lam_par, subln.reshape(1, D_VDIM))


def _page_specs(rows, layer, group, n_pages):
    def spec(g):
        def imap(b, j, pt):
            return (layer, pt[b, jnp.minimum(j * group + g, n_pages - 1)], 0, 0)
        return pl.BlockSpec((None, None, rows, PAGE_SIZE), imap)
    return [spec(g) for g in range(group)]


def _cat_lanes(refs):
    return jnp.concatenate([r[...] for r in refs], axis=1)


def _new_key_valid(nq, nk):
    rq = _iota((nq, nk), 0) % SLOT
    rk = _iota((nq, nk), 1)
    return (rk >= SLOT_LO) & (rk < SLOT_LO + (SLOT - 2 * SLOT_LO)) & (rk <= rq)


def _idx_scores(qi4, w, keys_t_bf16):
    lg = _dot(qi4, keys_t_bf16)
    s = jnp.zeros((SLOT, keys_t_bf16.shape[1]), F32)
    for h in range(IDX_HEADS):
        s = s + jnp.maximum(lg[h * SLOT:(h + 1) * SLOT], 0.0) * w[:, h:h + 1]
    return jnp.where(s == 0.0, 0.0, s)


def _dsa_sample_scores_body(pt_ref, qi_ref, kiw_ref, knew_ref, *rest, group):
    pages, o_ref = rest[:group], rest[group]
    j = pl.program_id(1)
    last = pl.num_programs(1) - 1
    qi = qi_ref[0] * IDX_DIM ** -0.5
    qi4 = jnp.concatenate([qi[:, h * IDX_DIM:(h + 1) * IDX_DIM] for h in range(IDX_HEADS)], axis=0).astype(BF16)
    w = kiw_ref[0][:, IDX_DIM:IDX_DIM + IDX_HEADS] * IDX_HEADS ** -0.5

    @pl.when(j < last)
    def _():
        o_ref[0, 0] = _idx_scores(qi4, w, _cat_lanes(pages).astype(BF16))

    @pl.when(j == last)
    def _():
        s = _idx_scores(qi4, w, knew_ref[0].astype(BF16))
        s = jnp.where(_new_key_valid(SLOT, PAGE_SIZE), s, -jnp.inf)
        pad = jnp.full((SLOT, (group - 1) * PAGE_SIZE), -jnp.inf, F32)
        o_ref[0, 0] = jnp.concatenate([s, pad], axis=1) if group > 1 else s


def _dsa_sample_scores(page_table, qi, kiw, knew_t, cache_idx_t, layer, group):
    db, n_pages = page_table.shape
    nch = n_pages // group + 1
    gw = group * PAGE_SIZE
    grid_spec = pltpu.PrefetchScalarGridSpec(
        num_scalar_prefetch=1, grid=(db, nch),
        in_specs=[pl.BlockSpec((1, SLOT, qi.shape[2]), lambda b, j, pt: (b, 0, 0)),
                  pl.BlockSpec((1, SLOT, LANES), lambda b, j, pt: (b, 0, 0)),
                  pl.BlockSpec((1, IDX_DIM, PAGE_SIZE), lambda b, j, pt: (b, 0, 0))]
                 + _page_specs(IDX_DIM, layer, group, n_pages),
        out_specs=pl.BlockSpec((1, 1, SLOT, gw), lambda b, j, pt: (b, j, 0, 0)))
    return pl.pallas_call(
        functools.partial(_dsa_sample_scores_body, group=group), grid_spec=grid_spec,
        out_shape=jax.ShapeDtypeStruct((db, nch, SLOT, gw), F32),
        compiler_params=_cp("arbitrary", "arbitrary"), name="dsa_sample_scores",
    )(page_table, qi, kiw, knew_t, *([cache_idx_t] * group))


def _dsa_sample_attend_body(pt_ref, qa_ref, sc_ref, kvnew_ref, tri_ref, *rest, group, topk):
    pages, o_ref = rest[:group], rest[group]
    t_sc, need_sc, carry_sc, m_sc, l_sc, acc_sc, ties_sm = rest[group + 1:]
    j = pl.program_id(1)
    last = pl.num_programs(1) - 1
    hpg = A_HEADS // A_KV_HEADS

    @pl.when(j == 0)
    def _():
        x = sc_ref[0]

        def stats(p):
            ge = x >= p[None]
            cnt = jnp.sum(jnp.sum(jnp.where(ge, 1.0, 0.0), axis=0), axis=1, keepdims=True)
            mnge = jnp.min(jnp.min(jnp.where(ge, x, jnp.inf), axis=0), axis=1, keepdims=True)
            mxlt = jnp.max(jnp.max(jnp.where(ge, -jnp.inf, x), axis=0), axis=1, keepdims=True)
            return cnt, mnge, mxlt

        finite = x > -jnp.inf
        cmax0 = jnp.max(jnp.max(x, axis=0), axis=1, keepdims=True)
        cmin0 = jnp.min(jnp.min(jnp.where(finite, x, jnp.inf), axis=0), axis=1, keepdims=True)
        def count_ge(p):
            return jnp.sum(jnp.sum(jnp.where(x >= p[None], 1.0, 0.0), axis=0), axis=1, keepdims=True)

        t = _kth_largest_bracketed(count_ge, stats, cmin0, cmax0, float(topk), 14)
        n_gt =jnp.sum(jnp.sum(jnp.where(x > t[None], 1.0, 0.0), axis=0), axis=1, keepdims=True)
        n_ge = jnp.sum(jnp.sum(jnp.where(x >= t[None], 1.0, 0.0), axis=0), axis=1, keepdims=True)
        ties_sm[0] = (jnp.max(n_ge) > float(topk)).astype(jnp.int32)
        t_sc[...] = jnp.broadcast_to(t, t_sc.shape)
        need_sc[...] = jnp.broadcast_to(float(topk) - n_gt, need_sc.shape)
        carry_sc[...] = jnp.zeros(carry_sc.shape, F32)
        m_sc[...] = jnp.full(m_sc.shape, NEG, F32)
        l_sc[...] = jnp.zeros(l_sc.shape, F32)
        acc_sc[...] = jnp.zeros(acc_sc.shape, F32)

    qa = qa_ref[0] * HEAD_DIM ** -0.5
    qg = [jnp.concatenate([qa[:, (g * hpg + h) * HEAD_DIM:(g * hpg + h + 1) * HEAD_DIM] for h in range(hpg)],
                          axis=0).astype(BF16) for g in range(A_KV_HEADS)]
    t = t_sc[...][:, :1]
    need = need_sc[...][:, :1]
    tri = tri_ref[...]

    def tie_bias(x):
        carry = carry_sc[...][:, :1]
        biases = []
        for xb in _lane_blocks(x):
            bias, carry = _tie_bias(xb, t, need, carry, tri)
            biases.append(bias)
        carry_sc[...] = jnp.broadcast_to(carry, carry_sc.shape)
        return jnp.concatenate(biases, axis=1) if len(biases) > 1 else biases[0]

    def attend(x, kv_t):
        bias = lax.cond(ties_sm[0] > 0, tie_bias, lambda xs: jnp.where(xs >= t, 0.0, NEG), x)
        bias4 = jnp.concatenate([bias] * hpg, axis=0)
        for g in range(A_KV_HEADS):
            kc = kv_t[g * HEAD_DIM:(g + 1) * HEAD_DIM].astype(BF16)
            vc = kv_t[(A_KV_HEADS + g) * HEAD_DIM:(A_KV_HEADS + g + 1) * HEAD_DIM].astype(BF16)
            _softmax_step(_dot(qg[g], kc) + bias4, vc, m_sc, l_sc, acc_sc, g, values_transposed=True)

    @pl.when(j < last)
    def _():
        attend(sc_ref[0, j], _cat_lanes(pages))

    @pl.when(j == last)
    def _():
        attend(sc_ref[0, j][:, :PAGE_SIZE], kvnew_ref[0])
        outs = []
        for g in range(A_KV_HEADS):
            og = acc_sc[g] / l_sc[g][:, :HEAD_DIM]
            outs += [og[h * SLOT:(h + 1) * SLOT] for h in range(hpg)]
        o_ref[0] = jnp.concatenate(outs, axis=1)


def _dsa_sample_attend(page_table, qa, scores, kvnew, cache_kv, layer, group, topk):
    db, n_pages = page_table.shape
    nch = n_pages // group + 1
    gw = group * PAGE_SIZE
    hpg = A_HEADS // A_KV_HEADS
    kvw = 2 * A_KV_HEADS * HEAD_DIM
    grid_spec = pltpu.PrefetchScalarGridSpec(
        num_scalar_prefetch=1, grid=(db, nch),
        in_specs=[pl.BlockSpec((1, SLOT, qa.shape[2]), lambda b, j, pt: (b, 0, 0)),
                  pl.BlockSpec((1, nch, SLOT, gw), lambda b, j, pt: (b, 0, 0, 0)),
                  pl.BlockSpec((1, kvw, PAGE_SIZE), lambda b, j, pt: (b, 0, 0)),
                  pl.BlockSpec((PAGE_SIZE, PAGE_SIZE), lambda b, j, pt: (0, 0))]
                 + _page_specs(kvw, layer, group, n_pages),
        out_specs=pl.BlockSpec((1, SLOT, A_HEADS * HEAD_DIM), lambda b, j, pt: (b, 0, 0)),
        scratch_shapes=[pltpu.VMEM((SLOT, LANES), F32), pltpu.VMEM((SLOT, LANES), F32), pltpu.VMEM((SLOT, LANES), F32),
                        pltpu.VMEM((A_KV_HEADS, hpg * SLOT, LANES), F32),
                        pltpu.VMEM((A_KV_HEADS, hpg * SLOT, LANES), F32),
                        pltpu.VMEM((A_KV_HEADS, hpg * SLOT, HEAD_DIM), F32),
                        pltpu.SMEM((1,), jnp.int32)])
    return pl.pallas_call(
        functools.partial(_dsa_sample_attend_body, group=group, topk=topk), grid_spec=grid_spec,
        out_shape=jax.ShapeDtypeStruct((db, SLOT, A_HEADS * HEAD_DIM), F32),
        compiler_params=_cp("arbitrary", "arbitrary"), name="dsa_sample_attend",
    )(page_table, qa, scores, kvnew, _upper_tri_bf16(PAGE_SIZE), *([cache_kv] * group))


def _diff_sample_body(pt_ref, q_ref, knew_ref, vnew_ref, lam_ref, sub_ref, *rest, group, lam_init):
    kpages, vpages, o_ref = rest[:group], rest[group:2 * group], rest[2 * group]
    m_sc, l_sc, acc_sc = rest[2 * group + 1:]
    j = pl.program_id(1)
    last = pl.num_programs(1) - 1
    nmap = 2 * D_HEADS
    w = nmap * HEAD_DIM

    @pl.when(j == 0)
    def _():
        m_sc[...] = jnp.full(m_sc.shape, NEG, F32)
        l_sc[...] = jnp.zeros(l_sc.shape, F32)
        acc_sc[...] = jnp.zeros(acc_sc.shape, F32)

    q = q_ref[0] * HEAD_DIM ** -0.5
    lane_map = _iota((SLOT, w), 1) // HEAD_DIM
    qb = jnp.concatenate([jnp.where(lane_map == m * D_HEADS + h, q, 0.0)
                          for h in range(D_HEADS) for m in range(2)], axis=0).astype(BF16)
    rows_h = 2 * SLOT

    def step(k_t, v_refs, mask):
        s = _dot(qb, k_t.astype(BF16))
        if mask is not None:
            s = jnp.where(mask, s, NEG)
        p, alpha, m_new, l_new = _softmax_probs(_lane_blocks(s), m_sc[...], l_sc[...])
        m_sc[...] = m_new
        l_sc[...] = l_new
        for h in range(D_HEADS):
            vh = jnp.concatenate([r[pl.ds(h, PAGE_SIZE, stride=D_HEADS), :] for r in v_refs], axis=0).astype(BF16)
            rs = slice(h * rows_h, (h + 1) * rows_h)
            acc_sc[h] = alpha[rs] * acc_sc[h] + _dot(p[rs], vh)

    @pl.when(j < last)
    def _():
        step(_cat_lanes(kpages), vpages, None)

    @pl.when(j == last)
    def _():
        step(knew_ref[0], [vnew_ref.at[0]], _new_key_valid(nmap * SLOT, PAGE_SIZE))
        lam = _diff_lambda(lam_ref, lam_init)
        l = l_sc[...]
        outs = []
        for h in range(D_HEADS):
            acc = acc_sc[h]
            lh = l[h * rows_h:(h + 1) * rows_h]
            outs.append(_diff_finish(acc[:SLOT], lh[:SLOT], acc[SLOT:], lh[SLOT:], lam, sub_ref[...], lam_init))
        o_ref[0] = jnp.concatenate(outs, axis=1)


def _diff_sample(page_table, q, knew_t, vnew, lam_par, subln, cache_k_t, cache_v, layer, group, lam_init):
    db, n_pages = page_table.shape
    nch = n_pages // group + 1
    nmap = 2 * D_HEADS
    w = q.shape[2]
    vw = D_HEADS * D_VDIM
    vrows = PAGE_SIZE * D_HEADS
    grid_spec = pltpu.PrefetchScalarGridSpec(
        num_scalar_prefetch=1, grid=(db, nch),
        in_specs=[pl.BlockSpec((1, SLOT, w), lambda b, j, pt: (b, 0, 0)),
                  pl.BlockSpec((1, w, PAGE_SIZE), lambda b, j, pt: (b, 0, 0)),
                  pl.BlockSpec((1, vrows, D_VDIM), lambda b, j, pt: (b, 0, 0)),
                  pl.BlockSpec(lam_par.shape, lambda b, j, pt: (0, 0)),
                  pl.BlockSpec((1, D_VDIM), lambda b, j, pt: (0, 0))]
                 + _page_specs(w, layer, group, n_pages) + _page_specs(vrows, layer, group, n_pages),
        out_specs=pl.BlockSpec((1, SLOT, vw), lambda b, j, pt: (b, 0, 0)),
        scratch_shapes=[pltpu.VMEM((nmap * SLOT, LANES), F32), pltpu.VMEM((nmap * SLOT, LANES), F32),
                        pltpu.VMEM((D_HEADS, 2 * SLOT, D_VDIM), F32)])
    return pl.pallas_call(
        functools.partial(_diff_sample_body, group=group, lam_init=lam_init), grid_spec=grid_spec,
        out_shape=jax.ShapeDtypeStruct((db, SLOT, vw), F32),
        compiler_params=_cp("arbitrary", "arbitrary"), name="diff_sample",
    )(page_table, q, knew_t, vnew, lam_par, subln.reshape(1, D_VDIM), *([cache_k_t] * group), *([cache_v] * group))


def _split_cols(w, sizes):
    out, o = [], 0
    for s in sizes:
        out.append(w[:, o:o + s])
        o += s
    return out


def _pad_cols(w, n):
    return jnp.pad(w, ((0, 0), (0, n - w.shape[1])))


def _pad_rows(a, n):
    return jnp.pad(a, ((0, 0), (0, n - a.shape[1]), (0, 0)))


def _keys_t(a):
    return jnp.pad(jnp.swapaxes(a, 1, 2), ((0, 0), (0, 0), (0, PAGE_SIZE - a.shape[1])))


def _pos_minor(cache):
    nd = cache.ndim
    t = jnp.transpose(cache, (0, 1) + tuple(range(3, nd)) + (2,))
    return t.reshape(cache.shape[0], cache.shape[1], -1, cache.shape[2])


def _ab_weights(w_in):
    hd = HEAD_DIM
    aq, ak, av, iq, ik, iw, bq, bk, bv, bg = _split_cols(
        w_in, (A_HEADS * hd, A_KV_HEADS * hd, A_KV_HEADS * hd, IDX_HEADS * IDX_DIM, IDX_DIM, IDX_HEADS,
               B_HEADS * hd, B_HEADS * hd, B_HEADS * hd, B_HEADS * hd))
    segs = [aq, jnp.concatenate([ak, av], axis=1), iq, _pad_cols(jnp.concatenate([ik, iw], axis=1), LANES), bq, bk, bv, bg]
    return [s.astype(BF16) for s in segs]


def _cd_weights(w_in):
    hd = HEAD_DIM
    cq, ck, cv, ci, cf, co, dq, dk, dv = _split_cols(
        w_in, (C_HEADS * hd, C_HEADS * hd, C_HEADS * hd, C_HEADS, C_HEADS, C_HEADS * hd,
               2 * D_HEADS * hd, 2 * D_HEADS * hd, D_HEADS * D_VDIM))
    segs = [cq, ck, cv, co, _pad_cols(jnp.concatenate([ci, cf], axis=1), LANES), dq, dk, dv]
    return [s.astype(BF16) for s in segs]


def kernel(x_prompt, x_sample, c_prompt, c_sample, page_table, cache_a_kv, cache_a_idx, cache_d_k, cache_d_v, state_ret, state_mlstm_c, state_mlstm_n, state_mlstm_m, state_ffn_conv, norm_pre_mix, norm_post_mix, norm_pre_ffn, norm_post_ffn, w_ada, b_ada, w_ab_in, w_ab_out, ret_gain, w_cd_in, w_cd_out, c_gate_bias, c_gain, d_lambda, d_subln, ffn_w_up, ffn_conv_w, ffn_conv_b, ffn_w_down):
    b, s, d = x_prompt.shape
    db, t, _ = x_sample.shape
    depth = w_ada.shape[0]
    n_pages = page_table.shape[1]
    past = n_pages * PAGE_SIZE
    d_ff = ffn_w_down.shape[1]
    assert t == SLOT - 2 * SLOT_LO and past >= TOPK_MAX and s % 1024 == 0
    hi = SLOT_LO + t
    ms = db * SLOT
    tm_in, tm_ffn, tf = 512, 256, 256
    group = math.gcd(n_pages, 32)
    group_a = math.gcd(n_pages, 32)
    topk_s = min(TOPK_MAX, (past + t) // 4)

    n_c = b + db
    c_all = jnp.pad(jnp.concatenate([c_prompt, c_sample], axis=0), ((0, -n_c % 8), (0, 0)))
    mods = _modulation(c_all, w_ada, b_ada)
    w_up, w_down = ffn_w_up.astype(BF16), ffn_w_down.astype(BF16)
    w_ab_in, w_ab_out = w_ab_in.astype(BF16), w_ab_out.astype(BF16)
    w_cd_in, w_cd_out = w_cd_in.astype(BF16), w_cd_out.astype(BF16)

    xp = x_prompt.reshape(b * s, d)
    xs = jnp.pad(x_sample, ((0, 0), (SLOT_LO, SLOT - hi), (0, 0))).reshape(ms, d)

    pos_s = past + (np.arange(ms) % SLOT) - SLOT_LO
    tp_p, tf_p = _rope_tables(np.arange(s), HEAD_DIM // 4, ROPE_THETA), _rope_tables(np.arange(s), HEAD_DIM, RET_THETA)
    tp_s, tf_s = _rope_tables(pos_s, HEAD_DIM // 4, ROPE_THETA), _rope_tables(pos_s, HEAD_DIM, RET_THETA)

    cache_a_kv = _pos_minor(cache_a_kv)
    cache_a_idx = _pos_minor(cache_a_idx)
    cache_d_k = _pos_minor(cache_d_k)
    cache_d_v = cache_d_v.reshape(*cache_d_v.shape[:2], -1, D_VDIM)

    hd = HEAD_DIM
    w512 = 8 * hd
    ab_defs = lambda dt: [(w512, dt), (4 * hd, F32), (4 * hd, dt), (LANES, F32), (w512, dt), (w512, F32), (w512, dt), (w512, F32)]
    cd_defs = lambda dt: [(w512, dt), (w512, F32), (w512, dt), (w512, F32), (LANES, F32), (w512, dt), (w512, F32), (w512, F32)]
    zeros = lambda *shape: jnp.zeros(shape, F32)

    outs = {k: [] for k in ("pa_kv", "pa_idx", "pd_k", "pd_v", "p_ret", "p_mc", "p_mn", "p_mm", "p_cv",
                            "sa_kv", "sa_idx", "sd_k", "sd_v", "s_ret", "s_mc", "s_mn", "s_mm", "s_cv")}
    for l in range(depth):
        p = l // 2
        m6 = mods[l, :n_c].reshape(n_c, 6, d)
        mp = [m6[:b, i][:, None, :] for i in range(6)]
        msm = [jnp.repeat(m6[b:, i], SLOT, axis=0)[None] for i in range(6)]
        if l % 2 == 0:
            wts = _ab_weights(w_ab_in[p])
            wo = w_ab_out[p].astype(BF16)
            qa, kv, qi, kiw, bq, bk, bv, bg = _in_proj(_ab_in_body, "ab_in", xp, norm_pre_mix[l], mp[0], mp[1],
                                                       [tp_p, tf_p], [], wts, ab_defs(BF16), tm_in, s // tm_in)
            r3 = lambda a: a.reshape(b, s, a.shape[-1])
            a_out = _dsa_prompt(r3(qa), r3(qi), r3(kiw), r3(kv))
            b_out, st = _retention(r3(bq), r3(bk), r3(bv), r3(bg), zeros(b, B_HEADS, hd, hd), ret_gain[p],
                                   CHUNK, 0, CHUNK, BF16)
            outs["pa_kv"].append(kv.reshape(b, s, 2, A_KV_HEADS, hd))
            outs["pa_idx"].append(r3(kiw)[:, :, :IDX_DIM])
            outs["p_ret"].append(st)
            mix_p = (a_out.reshape(b * s, -1), b_out.reshape(b * s, -1))
            qa, kv, qi, kiw, bq, bk, bv, bg = _in_proj(_ab_in_body, "ab_in", xs, norm_pre_mix[l], msm[0], msm[1],
                                                       [tp_s, tf_s], [], wts, ab_defs(F32), ms, 1)
            r3 = lambda a: a.reshape(db, SLOT, a.shape[-1])
            scores = _dsa_sample_scores(page_table, r3(qi), r3(kiw), _keys_t(r3(kiw)[:, :, :IDX_DIM]), cache_a_idx, p, group_a)
            a_out = _dsa_sample_attend(page_table, r3(qa), scores, _keys_t(r3(kv)), cache_a_kv, p, group_a, topk_s)
            b_out, st = _retention(r3(bq), r3(bk), r3(bv), r3(bg), state_ret[p], ret_gain[p], SLOT, SLOT_LO, hi, F32)
            outs["sa_kv"].append(r3(kv)[:, SLOT_LO:hi].reshape(db, t, 2, A_KV_HEADS, hd))
            outs["sa_idx"].append(r3(kiw)[:, SLOT_LO:hi, :IDX_DIM])
            outs["s_ret"].append(st)
            mix_s = (a_out.reshape(ms, -1), b_out.reshape(ms, -1))
        else:
            lam_init = 0.8 - 0.6 * math.exp(-0.3 * l)
            wts = _cd_weights(w_cd_in[p])
            wo = w_cd_out[p].astype(BF16)
            gbias = jnp.pad(c_gate_bias[p], (0, LANES - 2 * C_HEADS)).reshape(1, LANES)
            cq, ck, cv, co, gt, dq, dk, dv = _in_proj(_cd_in_body, "cd_in", xp, norm_pre_mix[l], mp[0], mp[1],
                                                      [tp_p], [gbias], wts, cd_defs(BF16), tm_in, s // tm_in)
            r3 = lambda a: a.reshape(b, s, a.shape[-1])
            c_out, mc, mn, mm = _mlstm(r3(cq), r3(ck), r3(cv), r3(co), r3(gt), zeros(b, C_HEADS, hd, hd),
                                       zeros(b, C_HEADS, hd), zeros(b, C_HEADS), c_gain[p], CHUNK, 0, CHUNK, BF16)
            d_out = _diff_prompt(r3(dq), r3(dk), r3(dv), d_lambda[p], d_subln[p], lam_init)
            outs["pd_k"].append(dk.reshape(b, s, 2, D_HEADS, hd))
            outs["pd_v"].append(dv.reshape(b, s, D_HEADS, D_VDIM))
            outs["p_mc"].append(mc); outs["p_mn"].append(mn); outs["p_mm"].append(mm)
            mix_p = (c_out.reshape(b * s, -1), d_out.reshape(b * s, -1))
            cq, ck, cv, co, gt, dq, dk, dv = _in_proj(_cd_in_body, "cd_in", xs, norm_pre_mix[l], msm[0], msm[1],
                                                      [tp_s], [gbias], wts, cd_defs(F32), ms, 1)
            r3 = lambda a: a.reshape(db, SLOT, a.shape[-1])
            c_out, mc, mn, mm = _mlstm(r3(cq), r3(ck), r3(cv), r3(co), r3(gt), state_mlstm_c[p], state_mlstm_n[p],
                                       state_mlstm_m[p], c_gain[p], SLOT, SLOT_LO, hi, F32)
            vnew = _pad_rows(dv.reshape(db, SLOT * D_HEADS, D_VDIM), PAGE_SIZE * D_HEADS)
            d_out = _diff_sample(page_table, r3(dq), _keys_t(r3(dk)), vnew,
                                 d_lambda[p], d_subln[p], cache_d_k, cache_d_v, p, group, lam_init)
            outs["sd_k"].append(r3(dk)[:, SLOT_LO:hi].reshape(db, t, 2, D_HEADS, hd))
            outs["sd_v"].append(r3(dv)[:, SLOT_LO:hi].reshape(db, t, D_HEADS, D_VDIM))
            outs["s_mc"].append(mc); outs["s_mn"].append(mn); outs["s_mm"].append(mm)
            mix_s = (c_out.reshape(ms, -1), d_out.reshape(ms, -1))

        xp, tails = _ffn(xp, *mix_p, wo, norm_post_mix[l], mp[2], norm_pre_ffn[l], mp[3], mp[4], w_up, w_down, l,
                         ffn_conv_w[l], ffn_conv_b[l], None, norm_post_ffn[l], mp[5], tm_ffn, s // tm_ffn, tf)
        outs["p_cv"].append(tails.reshape(b, s // tm_ffn, 2, d_ff)[:, -1])
        inj = jnp.pad(state_ffn_conv[l], ((0, 0), (0, SLOT - SLOT_LO), (0, 0))).reshape(ms, d_ff)
        xs, a_all = _ffn(xs, *mix_s, wo, norm_post_mix[l], msm[2], norm_pre_ffn[l], msm[3], msm[4], w_up, w_down, l,
                         ffn_conv_w[l], ffn_conv_b[l], inj, norm_post_ffn[l], msm[5], ms, 1, tf)
        outs["s_cv"].append(a_all.reshape(db, SLOT, d_ff)[:, hi - (CONV_W - 1):hi])

    st = lambda k: jnp.stack(outs[k])
    y_sample = xs.reshape(db, SLOT, d)[:, SLOT_LO:hi]
    return (xp.reshape(b, s, d), y_sample,
            st("pa_kv"), st("pa_idx"), st("pd_k"), st("pd_v"), st("p_ret"), st("p_mc"), st("p_mn"), st("p_mm"), st("p_cv"),
            st("sa_kv"), st("sa_idx"), st("sd_k"), st("sd_v"), st("s_ret"), st("s_mc"), st("s_mn"), st("s_mm"), st("s_cv"))
```

```python
import functools
import math

import numpy as np
import jax
import jax.numpy as jnp
from jax import lax
from jax.experimental import pallas as pl
from jax.experimental.pallas import tpu as pltpu

F32 = jnp.float32
BF16 = jnp.bfloat16

PAGE_SIZE = 128
HEAD_DIM = 64
ROPE_THETA = 500000.0
RET_THETA = 10000.0
A_HEADS = 8
A_KV_HEADS = 2
IDX_HEADS = 4
IDX_DIM = 64
TOPK_MAX = 256
B_HEADS = 8
C_HEADS = 8
D_HEADS = 4
D_VDIM = 2 * HEAD_DIM
CONV_W = 3
CHUNK = 128
EPS = 1e-6
NEG = -1e30
LOG2E = 1.4426950408889634
SLOT = 8
SLOT_LO = CONV_W - 1
LANES = 128
VMEM_LIMIT = 56 * 1024 * 1024

_NT = (((1,), (1,)), ((), ()))


def _cp(*sem):
    return pltpu.CompilerParams(dimension_semantics=sem, vmem_limit_bytes=VMEM_LIMIT)


def _dot(a, b):
    return jnp.dot(a, b, preferred_element_type=F32)


def _dot_nt(a, b):
    return lax.dot_general(a, b, _NT, preferred_element_type=F32)


def _iota(shape, dim):
    return lax.broadcasted_iota(jnp.int32, shape, dim)


def _eye_bf16(n):
    return jnp.where(_iota((n, n), 0) == _iota((n, n), 1), 1.0, 0.0).astype(BF16)


def _dot_tn(a_bf16, b_bf16):
    at = _dot_nt(_eye_bf16(a_bf16.shape[1]), a_bf16).astype(BF16)
    return _dot(at, b_bf16)


def _split3(x):
    hi = x.astype(BF16)
    r1 = x - hi.astype(F32)
    mid = r1.astype(BF16)
    lo = (r1 - mid.astype(F32)).astype(BF16)
    return hi, mid, lo


def _rope_tables(pos, rot, theta):
    half = rot // 2
    inv = theta ** (-np.arange(half, dtype=np.float64) / half)
    ang = np.asarray(pos, np.float64)[:, None] * inv[None]
    cos, sin = np.cos(ang), np.sin(ang)
    n = len(pos)
    t0 = np.ones((n, HEAD_DIM)); t1 = np.zeros((n, HEAD_DIM)); t2 = np.zeros((n, HEAD_DIM))
    t0[:, :half] = cos; t0[:, half:rot] = cos
    t1[:, :half] = -sin
    t2[:, half:rot] = sin
    tab = np.stack([np.tile(t, (1, LANES // HEAD_DIM)) for t in (t0, t1, t2)])
    return jnp.asarray(tab, F32)


def _retention_tables(chunk, lo, hi):
    n = hi - lo
    log_g = np.log1p(-np.exp2(-5.0 - np.arange(B_HEADS, dtype=np.float64)))
    r = np.arange(chunk)
    ok = (r >= lo) & (r < hi)
    pos = (r - lo).astype(np.float64)
    diff = pos[:, None] - pos[None, :]
    dm = np.where((diff >= 0) & ok[:, None] & ok[None, :], np.exp(log_g[:, None, None] * np.maximum(diff, 0.0)), 0.0)
    qd = np.exp(log_g[:, None] * (pos + 1.0))[:, :, None] * np.ones((1, 1, HEAD_DIM))
    kd = np.where(ok, np.exp(log_g[:, None] * (n - 1.0 - pos)), 0.0)[:, :, None] * np.ones((1, 1, HEAD_DIM))
    cd = np.exp(log_g * n)[:, None, None] * np.ones((1, 1, HEAD_DIM))
    return tuple(jnp.asarray(a, F32) for a in (dm, qd, kd, cd))


def _mod_body(c_ref, w_ref, b_ref, o_ref):
    c = c_ref[...]
    s = (c * jax.nn.sigmoid(c)).astype(BF16)
    o_ref[0] = _dot(s, w_ref[0].astype(BF16)) + b_ref[0]


def _modulation(c_all, w_ada, b_ada):
    depth, d, n = w_ada.shape
    r = c_all.shape[0]
    tn = 1536
    return pl.pallas_call(
        _mod_body, grid=(depth, n // tn),
        in_specs=[pl.BlockSpec((r, d), lambda l, j: (0, 0)),
                  pl.BlockSpec((1, d, tn), lambda l, j: (l, 0, j)),
                  pl.BlockSpec((1, 1, tn), lambda l, j: (l, 0, j))],
        out_specs=pl.BlockSpec((1, r, tn), lambda l, j: (l, 0, j)),
        out_shape=jax.ShapeDtypeStruct((depth, r, n), F32),
        compiler_params=_cp("arbitrary", "arbitrary"), name="modulation",
    )(c_all, w_ada, b_ada.reshape(depth, 1, n))


def _rmsnorm(x, g):
    return x * lax.rsqrt(jnp.mean(x * x, axis=-1, keepdims=True) + EPS) * g


def _ada_norm_bf16(x_ref, g_ref, sh_ref, sc_ref):
    return (_rmsnorm(x_ref[...], g_ref[...]) * (1.0 + sc_ref[0]) + sh_ref[0]).astype(BF16)


def _rope128(x, tab, shift):
    xl = pltpu.roll(x, LANES - shift, 1)
    xr = pltpu.roll(x, shift, 1)
    return x * tab[0] + xl * tab[1] + xr * tab[2]


def _rope(u, tab, shift):
    return jnp.concatenate(
        [_rope128(u[:, c * LANES:(c + 1) * LANES], tab, shift) for c in range(u.shape[1] // LANES)], axis=1)


def _ab_in_body(x_ref, g_ref, sh_ref, sc_ref, tp_ref, tf_ref,
                wqa, wkv, wqi, wki, wbq, wbk, wbv, wbg,
                qa_o, kv_o, qi_o, kiw_o, bq_o, bk_o, bv_o, bg_o):
    h = _ada_norm_bf16(x_ref, g_ref, sh_ref, sc_ref)
    tp = tp_ref[...]
    tf = tf_ref[...]
    pshift, fshift = HEAD_DIM // 8, HEAD_DIM // 2
    qa_o[...] = _rope(_dot(h, wqa[...]), tp, pshift).astype(qa_o.dtype)
    u = _dot(h, wkv[...])
    kv_o[...] = jnp.concatenate([_rope128(u[:, :LANES], tp, pshift), u[:, LANES:]], axis=1)
    qi_o[...] = _rope(_dot(h, wqi[...]), tp, pshift).astype(qi_o.dtype)
    u = _dot(h, wki[...])
    kiw_o[...] = jnp.where(_iota(u.shape, 1) < IDX_DIM, _rope128(u, tp, pshift), u)
    bq_o[...] = _rope(_dot(h, wbq[...]), tf, fshift).astype(bq_o.dtype)
    bk_o[...] = _rope(_dot(h, wbk[...]), tf, fshift) * HEAD_DIM ** -0.5
    bv_o[...] = _dot(h, wbv[...]).astype(bv_o.dtype)
    bg_o[...] = _dot(h, wbg[...])


def _cd_in_body(x_ref, g_ref, sh_ref, sc_ref, tp_ref, gb_ref,
                wcq, wck, wcv, wco, wgt, wdq, wdk, wdv,
                cq_o, ck_o, cv_o, co_o, gt_o, dq_o, dk_o, dv_o):
    h = _ada_norm_bf16(x_ref, g_ref, sh_ref, sc_ref)
    tp = tp_ref[...]
    pshift = HEAD_DIM // 8
    cq_o[...] = _dot(h, wcq[...]).astype(cq_o.dtype)
    ck_o[...] = _dot(h, wck[...]) * HEAD_DIM ** -0.5
    cv_o[...] = _dot(h, wcv[...]).astype(cv_o.dtype)
    co_o[...] = _dot(h, wco[...])
    gt_o[...] = _dot(h, wgt[...]) + gb_ref[...]
    dq_o[...] = _rope(_dot(h, wdq[...]), tp, pshift).astype(dq_o.dtype)
    dk_o[...] = _rope(_dot(h, wdk[...]), tp, pshift)
    dv_o[...] = _dot(h, wdv[...])


def _mod_spec(mods, layer, comp, tps):
    r, d = mods.shape[2], mods.shape[3] // 6
    return pl.BlockSpec((None, 1, r, d), lambda i: (layer, i // tps, 0, comp))


def _in_proj(body, name, x, g, mods, layer, tables, extra, weights, out_defs, tm, tps):
    m, d = x.shape
    in_specs = [pl.BlockSpec((tm, d), lambda i: (i, 0)), pl.BlockSpec((1, d), lambda i: (0, 0)),
                _mod_spec(mods, layer, 0, tps), _mod_spec(mods, layer, 1, tps)]
    in_specs += [pl.BlockSpec((3, tm, LANES), lambda i: (0, i % tps, 0)) for _ in tables]
    in_specs += [pl.BlockSpec(e.shape, lambda i: (0, 0)) for e in extra]
    in_specs += [pl.BlockSpec(w.shape, lambda i: (0, 0)) for w in weights]
    return pl.pallas_call(
        body, grid=(m // tm,), in_specs=in_specs,
        out_specs=[pl.BlockSpec((tm, n), lambda i: (i, 0)) for n, _ in out_defs],
        out_shape=[jax.ShapeDtypeStruct((m, n), dt) for n, dt in out_defs],
        compiler_params=_cp("arbitrary"), name=name,
    )(x, g.reshape(1, d), mods, mods, *tables, *extra, *weights)


def _ffn_body(x_ref, ma_ref, mb_ref, wo_ref, g0_ref, gate0_ref, g1_ref, sh_ref, sc_ref, wu_ref, wd_ref, cw_ref, cb_ref,
              inj_ref, g2_ref, gate_ref, o_ref, aux_ref, carry_sc, *, tps, tf, inject):
    i = pl.program_id(0)
    tm = x_ref.shape[0]
    ff = wd_ref.shape[0]
    mix = jnp.concatenate([ma_ref[...].astype(BF16), mb_ref[...].astype(BF16)], axis=1)
    x = x_ref[...] + gate0_ref[0] * _rmsnorm(_dot(mix, wo_ref[...]), g0_ref[...])
    h = (_rmsnorm(x, g1_ref[...]) * (1.0 + sc_ref[0]) + sh_ref[0]).astype(BF16)
    row = _iota((tm, tf), 0)
    cw = cw_ref[...]
    cb = cb_ref[...]
    if not inject:
        @pl.when(i % tps == 0)
        def _():
            carry_sc[...] = jnp.zeros(carry_sc.shape, F32)

    def up(f):
        return _dot(h, wu_ref[:, f * tf:(f + 1) * tf]), _dot(h, wu_ref[:, ff + f * tf:ff + (f + 1) * tf])

    yv = None
    nf = ff // tf
    nxt = up(0)
    for f in range(nf):
        fs = slice(f * tf, (f + 1) * tf)
        a, b = nxt
        if f + 1 < nf:
            nxt = up(f + 1)
        if inject:
            a = jnp.where(row % SLOT < SLOT_LO, inj_ref[:, fs], a)
            aux_ref[:, fs] = a
            p1 = jnp.zeros((1, tf), F32)
            p2 = jnp.zeros((2, tf), F32)
        else:
            p2 = carry_sc[:, fs]
            carry_sc[:, fs] = a[tm - 2:, :]
            aux_ref[0, :, fs] = a[tm - 2:, :]
            p1 = p2[1:2, :]
        a1 = jnp.where(row == 0, p1, pltpu.roll(a, 1, 0))
        a2 = pltpu.roll(a, 2, 0)
        a2 = jnp.where(row == 0, p2[0:1, :], jnp.where(row == 1, p2[1:2, :], a2))
        conv = cb[:, fs] + a2 * cw[0:1, fs]
        conv = conv + a1 * cw[1:2, fs]
        conv = conv + a * cw[2:3, fs]
        y = jax.nn.gelu(conv, approximate=True) * b
        part = _dot(y.astype(BF16), wd_ref[fs, :])
        yv = part if yv is None else yv + part
    o_ref[...] = x + gate_ref[0] * _rmsnorm(yv, g2_ref[...])


def _ffn(x, mix_a, mix_b, w_out, g0, g1, mods, w_up, w_down, layer, cw, cb, inj, g2, tm, tps, tf):
    m, d = x.shape
    ff = w_down.shape[1]
    inject = inj is not None
    if inject:
        aux_shape, aux_spec = (m, ff), pl.BlockSpec((tm, ff), lambda i: (i, 0))
        inj_spec = pl.BlockSpec((tm, ff), lambda i: (i, 0))
    else:
        aux_shape, aux_spec = (m // tm, 2, ff), pl.BlockSpec((1, 2, ff), lambda i: (i, 0, 0))
        inj = jnp.zeros((8, LANES), F32)
        inj_spec = pl.BlockSpec((8, LANES), lambda i: (0, 0))
    mod = lambda comp: _mod_spec(mods, layer, comp, tps)
    fix = lambda i: (0, 0)
    return pl.pallas_call(
        functools.partial(_ffn_body, tps=tps, tf=tf, inject=inject),
        grid=(m // tm,),
        in_specs=[pl.BlockSpec((tm, d), lambda i: (i, 0)),
                  pl.BlockSpec((tm, mix_a.shape[1]), lambda i: (i, 0)), pl.BlockSpec((tm, mix_b.shape[1]), lambda i: (i, 0)),
                  pl.BlockSpec(w_out.shape, fix), pl.BlockSpec((1, d), fix), mod(2),
                  pl.BlockSpec((1, d), fix), mod(3), mod(4),
                  pl.BlockSpec((None, d, 2 * ff), lambda i: (layer, 0, 0)),
                  pl.BlockSpec((None, ff, d), lambda i: (layer, 0, 0)),
                  pl.BlockSpec((CONV_W, ff), fix), pl.BlockSpec((1, ff), fix),
                  inj_spec, pl.BlockSpec((1, d), fix), mod(5)],
        out_specs=[pl.BlockSpec((tm, d), lambda i: (i, 0)), aux_spec],
        out_shape=[jax.ShapeDtypeStruct((m, d), F32), jax.ShapeDtypeStruct(aux_shape, F32)],
        scratch_shapes=[pltpu.VMEM((2, ff), F32)],
        compiler_params=_cp("arbitrary"), name="mix_out_conv_ffn",
    )(x, mix_a, mix_b, w_out, g0.reshape(1, d), mods, g1.reshape(1, d), mods, mods, w_up, w_down, cw,
      cb.reshape(1, ff), inj, g2.reshape(1, d), mods)


def _kth_largest(stats_fn, cmin0, cmax0, k):
    def flag(lo, hi):
        return jnp.max(jnp.where(lo < hi, 1.0, 0.0))

    def body(st):
        cmin, cmax, _ = st
        mid = cmin + 0.5 * (cmax - cmin)
        p = jnp.where(mid > cmin, jnp.where(mid <= cmax, mid, cmax), cmax)
        cnt, mn_ge, mx_lt = stats_fn(p)
        is_open = cmin < cmax
        take_lo = cnt >= k
        ncmin = jnp.where(is_open, jnp.where(take_lo, mn_ge, cmin), cmin)
        ncmax = jnp.where(is_open, jnp.where(take_lo, cmax, mx_lt), cmax)
        return ncmin, ncmax, flag(ncmin, ncmax)

    return lax.while_loop(lambda st: st[2] > 0.0, body, (cmin0, cmax0, flag(cmin0, cmax0)))[0]


def _kth_largest_bracketed(count_fn, stats_fn, row_min, row_max, k, light_passes):
    def light(i, st):
        lo, hi = st
        mid = lo + 0.5 * (hi - lo)
        take = count_fn(mid) >= k
        return jnp.where(take, mid, lo), jnp.where(take, hi, mid)

    lo, hi = lax.fori_loop(0, light_passes, light, (row_min, row_max))
    _, cmin, _ = stats_fn(lo)
    cnt_hi, mn_hi, mx_hi = stats_fn(hi)
    at_top = cnt_hi >= k
    return _kth_largest(stats_fn, jnp.where(at_top, mn_hi, cmin), jnp.where(at_top, row_max, mx_hi), k)


def _lane_blocks(s):
    return [s[:, c * LANES:(c + 1) * LANES] for c in range(s.shape[1] // LANES)]


def _softmax_probs(blocks, m_prev, l_prev, exp=jnp.exp):
    mloc = functools.reduce(jnp.maximum, blocks)
    m_new = jnp.maximum(m_prev, jnp.max(mloc, axis=1, keepdims=True))
    alpha = exp(m_prev - m_new)
    ps = [exp(b - m_new) for b in blocks]
    l_new = alpha * l_prev + jnp.sum(functools.reduce(jnp.add, ps), axis=1, keepdims=True)
    p = (jnp.concatenate(ps, axis=1) if len(ps) > 1 else ps[0]).astype(BF16)
    return p, alpha, m_new, l_new


def _scale_acc(alpha, acc):
    dv = acc.shape[-1]
    if dv < LANES:
        return alpha[:, :dv] * acc
    return (alpha if dv == LANES else jnp.concatenate([alpha] * (dv // LANES), axis=1)) * acc


def _softmax_step(s, v_bf16, m_ref, l_ref, acc_ref, idx, values_transposed=False, exp=jnp.exp):
    p, alpha, m_new, l_new = _softmax_probs(_lane_blocks(s), m_ref[idx], l_ref[idx], exp)
    pv = _dot_nt(p, v_bf16) if values_transposed else _dot(p, v_bf16)
    acc_ref[idx] = _scale_acc(alpha, acc_ref[idx]) + pv
    m_ref[idx] = m_new
    l_ref[idx] = l_new


def _tie_bias(x, t, need, carry, tri):
    eq = x == t
    pref = _dot(jnp.where(eq, 1.0, 0.0).astype(BF16), tri) + carry
    bias = jnp.where(x > t, 0.0, jnp.where(eq, jnp.where(pref <= need, 0.0, NEG), NEG))
    return bias, pref[:, x.shape[1] - 1:]


def _fold8(x, op):
    parts = [x[i * 8:(i + 1) * 8] for i in range(x.shape[0] // 8)]
    while len(parts) > 1:
        parts = [op(parts[i], parts[i + 1]) for i in range(0, len(parts) - 1, 2)] + (parts[-1:] if len(parts) % 2 else [])
    return parts[0]


def _dsa_prompt_body(qa_ref, qi_ref, kiwq_ref, kiwk_ref, kv_ref, tril_ref, o_ref,
                     sct, sc, m_sc, l_sc, acc_sc, *, tq, ch, topk, light_passes):
    j = pl.program_id(1)
    q0 = j * tq
    nc = (q0 + tq + ch - 1) // ch
    qpos = q0 + _iota((1, tq), 1)
    kf = float(topk)
    inf = jnp.inf
    qi = qi_ref[0] * IDX_DIM ** -0.5
    qi4 = jnp.concatenate([qi[:, h * IDX_DIM:(h + 1) * IDX_DIM] for h in range(IDX_HEADS)], axis=0).astype(BF16)
    pick = jnp.where(_iota((8, LANES), 1) == _iota((8, LANES), 0) + IDX_DIM, 1.0, 0.0).astype(BF16)
    wt = sum(_dot_nt(pick, part) for part in _split3(kiwq_ref[0])) * IDX_HEADS ** -0.5

    def score_chunk(c, st):
        mx, mn = st
        c0 = pl.multiple_of(c * ch, ch)
        kc = kiwk_ref[0, pl.ds(c0, ch), :][:, :IDX_DIM].astype(BF16)
        lg = _dot_nt(kc, qi4)
        s = jnp.zeros((ch, tq), F32)
        for h in range(IDX_HEADS):
            s = s + jnp.maximum(lg[:, h * tq:(h + 1) * tq], 0.0) * wt[h:h + 1, :]
        s = jnp.where(s == 0.0, 0.0, s)
        valid = (c0 + _iota((ch, tq), 0)) <= qpos
        sct[c] = jnp.where(valid, s, -inf)
        mx = jnp.maximum(mx, _fold8(jnp.where(valid, s, -inf), jnp.maximum))
        mn = jnp.minimum(mn, _fold8(jnp.where(valid, s, inf), jnp.minimum))
        return mx, mn

    mx, mn = lax.fori_loop(0, nc, score_chunk, (jnp.full((8, tq), -inf, F32), jnp.full((8, tq), inf, F32)))
    active = qpos + 1 > topk
    row_max = jnp.where(active, jnp.max(mx, axis=0, keepdims=True), 0.0)
    row_min = jnp.where(active, jnp.min(mn, axis=0, keepdims=True), 0.0)

    def count_ge(p):
        def body(c, cnt):
            return cnt + _fold8(jnp.where(sct[c] >= p, 1.0, 0.0), jnp.add)
        return jnp.sum(lax.fori_loop(0, nc, body, jnp.zeros((8, tq), F32)), axis=0, keepdims=True)

    def stats(p):
        def body(c, st):
            cnt, mnge, mxlt = st
            x = sct[c]
            ge = x >= p
            return (cnt + _fold8(jnp.where(ge, 1.0, 0.0), jnp.add),
                    jnp.minimum(mnge, _fold8(jnp.where(ge, x, inf), jnp.minimum)),
                    jnp.maximum(mxlt, _fold8(jnp.where(ge, -inf, x), jnp.maximum)))
        init = (jnp.zeros((8, tq), F32), jnp.full((8, tq), inf, F32), jnp.full((8, tq), -inf, F32))
        cnt, mnge, mxlt = lax.fori_loop(0, nc, body, init)
        return (jnp.sum(cnt, axis=0, keepdims=True), jnp.min(mnge, axis=0, keepdims=True),
                jnp.max(mxlt, axis=0, keepdims=True))

    t = jnp.where(active, _kth_largest_bracketed(count_ge, stats, row_min, row_max, kf, light_passes), -inf)

    def count_sel(c, st):
        x = sct[c]
        return (st[0] + _fold8(jnp.where(x > t, 1.0, 0.0), jnp.add),
                st[1] + _fold8(jnp.where(x >= t, 1.0, 0.0), jnp.add))

    gt, ge = lax.fori_loop(0, nc, count_sel, (jnp.zeros((8, tq), F32), jnp.zeros((8, tq), F32)))
    need = kf - jnp.sum(gt, axis=0, keepdims=True)
    over = jnp.sum(ge, axis=0, keepdims=True) > kf
    has_ties = jnp.max(jnp.where(active, jnp.where(over, 1.0, 0.0), 0.0)) > 0.0
    eye = _eye_bf16(tq)

    def store_bias(c, sel_t):
        sel = _dot_nt(eye, sel_t.astype(BF16))
        sc[c] = jnp.where(sel > 0.5, 0.0, NEG)

    @pl.when(has_ties)
    def _():
        tril = tril_ref[...]

        def tie_chunk(c, carry):
            x = sct[c]
            eq = x == t
            pref = _dot(tril, jnp.where(eq, 1.0, 0.0).astype(BF16)) + carry
            store_bias(c, jnp.where(x > t, 1.0, jnp.where(eq, jnp.where(pref <= need, 1.0, 0.0), 0.0)))
            return pref[ch - 1:, :]

        lax.fori_loop(0, nc, tie_chunk, jnp.zeros((1, tq), F32))

    @pl.when(jnp.logical_not(has_ties))
    def _():
        def plain_chunk(c, carry):
            store_bias(c, jnp.where(sct[c] >= t, 1.0, 0.0))
            return carry

        lax.fori_loop(0, nc, plain_chunk, 0)

    cl = nc - 1
    sc[cl] = jnp.where((cl * ch + _iota((tq, ch), 1)) <= q0 + _iota((tq, ch), 0), sc[cl], NEG)

    qa = qa_ref[0].astype(F32) * (HEAD_DIM ** -0.5 * LOG2E)
    hpg = A_HEADS // A_KV_HEADS
    qh = [qa[:, h * HEAD_DIM:(h + 1) * HEAD_DIM].astype(BF16) for h in range(A_HEADS)]
    m_sc[...] = jnp.full(m_sc.shape, NEG, F32)
    l_sc[...] = jnp.zeros(l_sc.shape, F32)
    acc_sc[...] = jnp.zeros(acc_sc.shape, F32)

    def attend(c, carry):
        c0 = pl.multiple_of(c * ch, ch)
        bias = _lane_blocks(sc[c])
        kvc = kv_ref[0, pl.ds(c0, ch), :]
        kcs = [kvc[:, g * HEAD_DIM:(g + 1) * HEAD_DIM].astype(BF16) for g in range(A_KV_HEADS)]
        vcs = [kvc[:, (A_KV_HEADS + g) * HEAD_DIM:(A_KV_HEADS + g + 1) * HEAD_DIM].astype(BF16) for g in range(A_KV_HEADS)]
        ss = [_dot_nt(qh[hh], kcs[hh // hpg]) for hh in range(A_HEADS)]
        sm = []
        for hh in range(A_HEADS):
            g, rs = hh // hpg, pl.ds((hh % hpg) * tq, tq)
            blocks = [sb + bb for sb, bb in zip(_lane_blocks(ss[hh]), bias)]
            sm.append(_softmax_probs(blocks, m_sc[g, rs, :], l_sc[g, rs, :], exp=jnp.exp2))
        pv = [_dot(jnp.concatenate([sm[g * hpg + h][0] for h in range(hpg)], axis=0), vcs[g]) for g in range(A_KV_HEADS)]
        for hh in range(A_HEADS):
            g, rs = hh // hpg, pl.ds((hh % hpg) * tq, tq)
            m_sc[g, rs, :] = sm[hh][2]
            l_sc[g, rs, :] = sm[hh][3]
        for g in range(A_KV_HEADS):
            alpha = jnp.concatenate([sm[g * hpg + h][1] for h in range(hpg)], axis=0)
            acc_sc[g] = _scale_acc(alpha, acc_sc[g]) + pv[g]
        return carry

    lax.fori_loop(0, nc, attend, 0)
    outs = []
    for g in range(A_KV_HEADS):
        og = acc_sc[g] / l_sc[g][:, :HEAD_DIM]
        outs += [og[h * tq:(h + 1) * tq] for h in range(hpg)]
    o_ref[0] = jnp.concatenate(outs, axis=1).astype(o_ref.dtype)


def _upper_tri_bf16(n):
    return jnp.asarray(np.triu(np.ones((n, n), np.float32)), BF16)


def _dsa_prompt(qa, qi, kiw, kv, tq=128, ch=512):
    b, s, _ = qa.shape
    ch = min(ch, s)
    topk = min(TOPK_MAX, s // 4)
    hpg = A_HEADS // A_KV_HEADS
    return pl.pallas_call(
        functools.partial(_dsa_prompt_body, tq=tq, ch=ch, topk=topk, light_passes=14),
        grid=(b, s // tq),
        in_specs=[pl.BlockSpec((1, tq, qa.shape[2]), lambda i, j: (i, j, 0)),
                  pl.BlockSpec((1, tq, qi.shape[2]), lambda i, j: (i, j, 0)),
                  pl.BlockSpec((1, tq, LANES), lambda i, j: (i, j, 0)),
                  pl.BlockSpec((1, s, LANES), lambda i, j: (i, 0, 0)),
                  pl.BlockSpec((1, s, kv.shape[2]), lambda i, j: (i, 0, 0)),
                  pl.BlockSpec((ch, ch), lambda i, j: (0, 0))],
        out_specs=pl.BlockSpec((1, tq, A_HEADS * HEAD_DIM), lambda i, j: (i, j, 0)),
        out_shape=jax.ShapeDtypeStruct((b, s, A_HEADS * HEAD_DIM), BF16),
        scratch_shapes=[pltpu.VMEM((s // ch, ch, tq), F32),
                        pltpu.VMEM((s // ch, tq, ch), F32),
                        pltpu.VMEM((A_KV_HEADS, hpg * tq, LANES), F32),
                        pltpu.VMEM((A_KV_HEADS, hpg * tq, LANES), F32),
                        pltpu.VMEM((A_KV_HEADS, hpg * tq, HEAD_DIM), F32)],
        compiler_params=_cp("arbitrary", "arbitrary"), name="dsa_prompt",
    )(qa, qi, kiw, kiw, kv, jnp.asarray(np.tril(np.ones((ch, ch), np.float32)), BF16))


def _head_avg_matrix(width):
    seg = np.arange(width) // HEAD_DIM
    return jnp.asarray((seg[:, None] == seg[None, :]) / HEAD_DIM, BF16)


def _head_mean(x, avg):
    hi = x.astype(BF16)
    lo = (x - hi.astype(F32)).astype(BF16)
    return _dot(hi, avg) + _dot(lo, avg)


def _head_layernorm(o, gain, avg):
    d = o - _head_mean(o, avg)
    return d * lax.rsqrt(_head_mean(d * d, avg) + EPS) * gain


def _retention_body(q_ref, k_ref, v_ref, g_ref, s0_ref, dm_ref, qd_ref, kd_ref, cd_ref, gain_ref, avg_ref,
                    o_ref, sf_ref, s_sc):
    c = pl.program_id(1)

    @pl.when(c == 0)
    def _():
        s_sc[...] = s0_ref[0]

    q = q_ref[0].astype(BF16)
    k = k_ref[0]
    v = v_ref[0].astype(BF16)
    gt = g_ref[0]
    heads = range(B_HEADS)
    sls = [slice(h * HEAD_DIM, (h + 1) * HEAD_DIM) for h in heads]
    eye = _eye_bf16(HEAD_DIM)
    st = [s_sc[h] for h in heads]
    att = [_dot_nt(q[:, sl], k[:, sl].astype(BF16)) for sl in sls]
    qs = [_dot(q[:, sls[h]], st[h].astype(BF16)) for h in heads]
    kt = [_dot_nt(eye, (k[:, sls[h]] * kd_ref[h]).astype(BF16)) for h in heads]
    outs = [_dot((att[h] * dm_ref[h]).astype(BF16), v[:, sls[h]]) + qs[h] * qd_ref[h] for h in heads]
    for h in heads:
        s_sc[h] = st[h] * cd_ref[h] + _dot(kt[h].astype(BF16), v[:, sls[h]])
    o = _head_layernorm(jnp.concatenate(outs, axis=1), gain_ref[...], avg_ref[...])
    o_ref[0] = (gt * jax.nn.sigmoid(gt) * o).astype(o_ref.dtype)

    @pl.when(c == pl.num_programs(1) - 1)
    def _():
        sf_ref[0] = s_sc[...]


def _retention(q, k, v, g, s0, gain, chunk, lo, hi, out_dtype):
    b, s, w = q.shape
    dm, qd, kd, cd = _retention_tables(chunk, lo, hi)
    row = lambda i, c: (i, c, 0)
    fix3 = lambda i, c: (0, 0, 0)
    st_spec = pl.BlockSpec((1, B_HEADS, HEAD_DIM, HEAD_DIM), lambda i, c: (i, 0, 0, 0))
    return pl.pallas_call(
        _retention_body, grid=(b, s // chunk),
        in_specs=[pl.BlockSpec((1, chunk, w), row)] * 4 + [st_spec] +
                 [pl.BlockSpec(t.shape, fix3) for t in (dm, qd, kd, cd)] +
                 [pl.BlockSpec((1, w), lambda i, c: (0, 0)), pl.BlockSpec((w, w), lambda i, c: (0, 0))],
        out_specs=[pl.BlockSpec((1, chunk, w), row), st_spec],
        out_shape=[jax.ShapeDtypeStruct((b, s, w), out_dtype),
                   jax.ShapeDtypeStruct((b, B_HEADS, HEAD_DIM, HEAD_DIM), F32)],
        scratch_shapes=[pltpu.VMEM((B_HEADS, HEAD_DIM, HEAD_DIM), F32)],
        compiler_params=_cp("arbitrary", "arbitrary"), name="retention",
    )(q, k, v, g, s0, dm, qd, kd, cd, gain.reshape(1, w), _head_avg_matrix(w))


def _mlstm_body(q_ref, k_ref, v_ref, og_ref, gt_ref, cn0_ref, m0_ref, gain_ref, avg_ref,
                o_ref, cnf_ref, mf_ref, cn_sc, m_sc, *, lo, hi):
    c = pl.program_id(1)
    ck = q_ref.shape[1]
    nh = C_HEADS
    hd = HEAD_DIM

    @pl.when(c == 0)
    def _():
        cn_sc[...] = cn0_ref[0]
        m_sc[...] = m0_ref[0]

    q = q_ref[0].astype(BF16)
    k = k_ref[0]
    v = v_ref[0].astype(BF16)
    gates = gt_ref[0]
    row = _iota((ck, LANES), 0)
    tok = (row >= lo) & (row < hi)
    log_sig = jnp.minimum(gates, 0.0) - jnp.log1p(jnp.exp(-jnp.abs(gates)))
    lf = jnp.where(tok, log_sig, 0.0)
    ii = jnp.where(tok, gates, NEG)
    tril = jnp.where(_iota((ck, ck), 0) >= _iota((ck, ck), 1), 1.0, 0.0).astype(BF16)
    fcum = sum(_dot(tril, part) for part in _split3(lf))
    f_all = pltpu.roll(fcum, LANES - nh, 1)
    a_all = ii - f_all
    cm = a_all
    step = 1
    while step < ck:
        cm = jnp.maximum(cm, jnp.where(row >= step, pltpu.roll(cm, step, 0), -jnp.inf))
        step *= 2
    m_prev = m_sc[...]
    m_t = f_all + jnp.maximum(m_prev, cm)
    inter_all = jnp.exp(f_all + m_prev - m_t)
    floor_all = jnp.exp(-m_t)
    g_all = f_all - m_t
    f_last = f_all[ck - 1:, :]
    m_end = m_t[ck - 1:, :]
    w_all = jnp.exp(f_last + a_all - m_end)
    dec_all = jnp.exp(f_last + m_prev - m_end)
    m_sc[...] = m_end
    pick = jnp.where(_iota((nh, LANES), 0) == _iota((nh, LANES), 1), 1.0, 0.0).astype(BF16)
    a_rows = sum(_dot_nt(pick, part) for part in _split3(a_all))
    causal = _iota((ck, ck), 0) >= _iota((ck, ck), 1)
    eye = _eye_bf16(hd)
    ones = jnp.ones((ck, hd), BF16)
    n_lanes = _iota((hd, 2 * hd), 1) >= hd
    heads = range(nh)
    sls = [slice(h * hd, (h + 1) * hd) for h in heads]
    col = lambda x, h: x[:, h:h + 1]
    cn = [cn_sc[h] for h in heads]
    v1 = [jnp.concatenate([v[:, sl], ones], axis=1) for sl in sls]
    qk = [_dot_nt(q[:, sl], k[:, sl].astype(BF16)) for sl in sls]
    qc = [_dot(q[:, sls[h]], cn[h].astype(BF16)) for h in heads]
    kw = [k[:, sls[h]] * col(w_all, h) for h in heads]
    kw_hi = [x.astype(BF16) for x in kw]
    kw_lo = [(x - y.astype(F32)).astype(BF16) for x, y in zip(kw, kw_hi)]
    kt_hi = [_dot_nt(eye, x) for x in kw_hi]
    kt_lo = [_dot_nt(eye, x) for x in kw_lo]
    dmat = [jnp.exp(jnp.where(causal, col(g_all, h) + a_rows[h:h + 1, :], -jnp.inf)) for h in heads]
    nd = [_dot((qk[h] * dmat[h]).astype(BF16), v1[h]) + col(inter_all, h) * qc[h] for h in heads]
    outs = [(nd[h] / jnp.maximum(jnp.abs(pltpu.roll(nd[h], hd, 1)), col(floor_all, h)))[:, :hd] for h in heads]
    upd = [_dot(kt_hi[h].astype(BF16), v1[h]) for h in heads]
    fix = [_dot(kt_lo[h].astype(BF16), v1[h]) for h in heads]
    for h in heads:
        cn_sc[h] = col(dec_all, h) * cn[h] + upd[h] + jnp.where(n_lanes, fix[h], 0.0)
    hc = jax.nn.sigmoid(og_ref[0]) * jnp.concatenate(outs, axis=1)
    o_ref[0] = _head_layernorm(hc, gain_ref[...], avg_ref[...]).astype(o_ref.dtype)

    @pl.when(c == pl.num_programs(1) - 1)
    def _():
        cnf_ref[0] = cn_sc[...]
        mf_ref[0] = m_sc[...]


def _mlstm(q, k, v, og, gates, c0, n0, m0, gain, chunk, lo, hi, out_dtype):
    b, s, w = q.shape
    nh = C_HEADS
    row = lambda i, c: (i, c, 0)
    hd = HEAD_DIM
    cn_spec = pl.BlockSpec((1, nh, hd, 2 * hd), lambda i, c: (i, 0, 0, 0))
    m_spec = pl.BlockSpec((1, 1, LANES), lambda i, c: (i, 0, 0))
    cn0 = jnp.concatenate([c0, jnp.broadcast_to(n0[..., None], (b, nh, hd, hd))], axis=-1)
    m0p = jnp.pad(m0, ((0, 0), (0, LANES - nh))).reshape(b, 1, LANES)
    o, cnf, mf = pl.pallas_call(
        functools.partial(_mlstm_body, lo=lo, hi=hi), grid=(b, s // chunk),
        in_specs=[pl.BlockSpec((1, chunk, w), row)] * 4 + [pl.BlockSpec((1, chunk, LANES), row), cn_spec, m_spec,
                                                          pl.BlockSpec((1, w), lambda i, c: (0, 0)),
                                                          pl.BlockSpec((w, w), lambda i, c: (0, 0))],
        out_specs=[pl.BlockSpec((1, chunk, w), row), cn_spec, m_spec],
        out_shape=[jax.ShapeDtypeStruct((b, s, w), out_dtype),
                   jax.ShapeDtypeStruct((b, nh, hd, 2 * hd), F32),
                   jax.ShapeDtypeStruct((b, 1, LANES), F32)],
        scratch_shapes=[pltpu.VMEM((nh, hd, 2 * hd), F32), pltpu.VMEM((1, LANES), F32)],
        compiler_params=_cp("arbitrary", "arbitrary"), name="mlstm",
    )(q, k, v, og, gates, cn0, m0p, gain.reshape(1, w), _head_avg_matrix(w))
    return o, cnf[..., :hd], cnf[..., hd], mf[:, 0, :nh]


def _diff_lambda(lam_ref, lam_init):
    lp = lam_ref[...]
    s01 = jnp.sum(lp[0:1] * lp[1:2], axis=1, keepdims=True)
    s23 = jnp.sum(lp[2:3] * lp[3:4], axis=1, keepdims=True)
    return jnp.exp(s01) - jnp.exp(s23) + lam_init


def _diff_finish(o0, l0, o1, l1, lam, subln, lam_init):
    od = o0 / l0 - lam * (o1 / l1)
    return od * lax.rsqrt(jnp.mean(od * od, axis=-1, keepdims=True) + EPS) * subln * (1.0 - lam_init)


def _diff_prompt_body(q_ref, k_ref, v_ref, lam_ref, sub_ref, o_ref, m_sc, l_sc, acc_sc, *, lam_init):
    i = pl.program_id(1)
    j = pl.program_id(2)
    t = q_ref.shape[1]

    @pl.when(j == 0)
    def _():
        m_sc[...] = jnp.full(m_sc.shape, NEG, F32)
        l_sc[...] = jnp.zeros(l_sc.shape, F32)
        acc_sc[...] = jnp.zeros(acc_sc.shape, F32)

    def step(diag):
        q = (q_ref[0].astype(F32) * (HEAD_DIM ** -0.5 * LOG2E)).astype(BF16)
        k = k_ref[0].astype(BF16)
        v = v_ref[0].astype(BF16)
        keep = (_iota((t, t), 0) >= _iota((t, t), 1)) if diag else None
        pairs = range(2 * D_HEADS)
        ss = [_dot_nt(q[:, idx * HEAD_DIM:(idx + 1) * HEAD_DIM], k[:, idx * HEAD_DIM:(idx + 1) * HEAD_DIM]) for idx in pairs]
        if diag:
            ss = [jnp.where(keep, s, NEG) for s in ss]
        sm = [_softmax_probs(_lane_blocks(ss[idx]), m_sc[idx], l_sc[idx], jnp.exp2) for idx in pairs]
        pv = [_dot(sm[idx][0], v[:, (idx % D_HEADS) * D_VDIM:(idx % D_HEADS + 1) * D_VDIM]) for idx in pairs]
        for idx in pairs:
            _, alpha, m_new, l_new = sm[idx]
            acc_sc[idx] = _scale_acc(alpha, acc_sc[idx]) + pv[idx]
            m_sc[idx] = m_new
            l_sc[idx] = l_new

    @pl.when(j < i)
    def _():
        step(False)

    @pl.when(j == i)
    def _():
        step(True)
        lam = _diff_lambda(lam_ref, lam_init)
        outs = [_diff_finish(acc_sc[h], l_sc[h], acc_sc[D_HEADS + h], l_sc[D_HEADS + h], lam, sub_ref[...], lam_init)
                for h in range(D_HEADS)]
        o_ref[0] = jnp.concatenate(outs, axis=1).astype(o_ref.dtype)


def _diff_prompt(q, k, v, lam_par, subln, lam_init, t=512):
    b, s, w = q.shape
    t = min(t, s)
    n = s // t
    return pl.pallas_call(
        functools.partial(_diff_prompt_body, lam_init=lam_init), grid=(b, n, n),
        in_specs=[pl.BlockSpec((1, t, w), lambda bi, i, j: (bi, i, 0)),
                  pl.BlockSpec((1, t, w), lambda bi, i, j: (bi, jnp.minimum(i, j), 0)),
                  pl.BlockSpec((1, t, w), lambda bi, i, j: (bi, jnp.minimum(i, j), 0)),
                  pl.BlockSpec(lam_par.shape, lambda bi, i, j: (0, 0)),
                  pl.BlockSpec((1, D_VDIM), lambda bi, i, j: (0, 0))],
        out_specs=pl.BlockSpec((1, t, w), lambda bi, i, j: (bi, i, 0)),
        out_shape=jax.ShapeDtypeStruct((b, s, w), BF16),
        scratch_shapes=[pltpu.VMEM((2 * D_HEADS, t, LANES), F32), pltpu.VMEM((2 * D_HEADS, t, LANES), F32),
                        pltpu.VMEM((2 * D_HEADS, t, D_VDIM), F32)],
        compiler_params=_cp("arbitrary", "arbitrary", "arbitrary"), name="diff_prompt",
    )(q, k, v, lam_par, subln.reshape(1, D_VDIM))


def _page_specs(rows, layer, group, n_pages):
    def spec(g):
        def imap(b, j, pt):
            return (layer, pt[b, jnp.minimum(j * group + g, n_pages - 1)], 0, 0)
        return pl.BlockSpec((None, None, rows, PAGE_SIZE), imap)
    return [spec(g) for g in range(group)]


def _cat_lanes(refs):
    return jnp.concatenate([r[...] for r in refs], axis=1)


def _new_key_valid(nq, nk):
    rq = _iota((nq, nk), 0) % SLOT
    rk = _iota((nq, nk), 1)
    return (rk >= SLOT_LO) & (rk < SLOT_LO + (SLOT - 2 * SLOT_LO)) & (rk <= rq)


def _idx_scores(qi4, w, keys_t_bf16):
    lg = _dot(qi4, keys_t_bf16)
    s = jnp.zeros((SLOT, keys_t_bf16.shape[1]), F32)
    for h in range(IDX_HEADS):
        s = s + jnp.maximum(lg[h * SLOT:(h + 1) * SLOT], 0.0) * w[:, h:h + 1]
    return jnp.where(s == 0.0, 0.0, s)


def _dsa_sample_scores_body(pt_ref, qi_ref, kiw_ref, knew_ref, *rest, group):
    pages, o_ref = rest[:group], rest[group]
    j = pl.program_id(1)
    last = pl.num_programs(1) - 1
    qi = qi_ref[0] * IDX_DIM ** -0.5
    qi4 = jnp.concatenate([qi[:, h * IDX_DIM:(h + 1) * IDX_DIM] for h in range(IDX_HEADS)], axis=0).astype(BF16)
    w = kiw_ref[0][:, IDX_DIM:IDX_DIM + IDX_HEADS] * IDX_HEADS ** -0.5

    @pl.when(j < last)
    def _():
        o_ref[0, 0] = _idx_scores(qi4, w, _cat_lanes(pages).astype(BF16))

    @pl.when(j == last)
    def _():
        s = _idx_scores(qi4, w, knew_ref[0].astype(BF16))
        s = jnp.where(_new_key_valid(SLOT, PAGE_SIZE), s, -jnp.inf)
        pad = jnp.full((SLOT, (group - 1) * PAGE_SIZE), -jnp.inf, F32)
        o_ref[0, 0] = jnp.concatenate([s, pad], axis=1) if group > 1 else s


def _dsa_sample_scores(page_table, qi, kiw, knew_t, cache_idx_t, layer, group):
    db, n_pages = page_table.shape
    nch = n_pages // group + 1
    gw = group * PAGE_SIZE
    grid_spec = pltpu.PrefetchScalarGridSpec(
        num_scalar_prefetch=1, grid=(db, nch),
        in_specs=[pl.BlockSpec((1, SLOT, qi.shape[2]), lambda b, j, pt: (b, 0, 0)),
                  pl.BlockSpec((1, SLOT, LANES), lambda b, j, pt: (b, 0, 0)),
                  pl.BlockSpec((1, IDX_DIM, PAGE_SIZE), lambda b, j, pt: (b, 0, 0))]
                 + _page_specs(IDX_DIM, layer, group, n_pages),
        out_specs=pl.BlockSpec((1, 1, SLOT, gw), lambda b, j, pt: (b, j, 0, 0)))
    return pl.pallas_call(
        functools.partial(_dsa_sample_scores_body, group=group), grid_spec=grid_spec,
        out_shape=jax.ShapeDtypeStruct((db, nch, SLOT, gw), F32),
        compiler_params=_cp("arbitrary", "arbitrary"), name="dsa_sample_scores",
    )(page_table, qi, kiw, knew_t, *([cache_idx_t] * group))


def _dsa_sample_attend_body(pt_ref, qa_ref, sc_ref, kvnew_ref, tri_ref, *rest, group, topk):
    pages, o_ref = rest[:group], rest[group]
    t_sc, need_sc, carry_sc, m_sc, l_sc, acc_sc, ties_sm = rest[group + 1:]
    j = pl.program_id(1)
    last = pl.num_programs(1) - 1
    hpg = A_HEADS // A_KV_HEADS

    @pl.when(j == 0)
    def _():
        x = sc_ref[0]

        def stats(p):
            ge = x >= p[None]
            cnt = jnp.sum(jnp.sum(jnp.where(ge, 1.0, 0.0), axis=0), axis=1, keepdims=True)
            mnge = jnp.min(jnp.min(jnp.where(ge, x, jnp.inf), axis=0), axis=1, keepdims=True)
            mxlt = jnp.max(jnp.max(jnp.where(ge, -jnp.inf, x), axis=0), axis=1, keepdims=True)
            return cnt, mnge, mxlt

        finite = x > -jnp.inf
        cmax0 = jnp.max(jnp.max(x, axis=0), axis=1, keepdims=True)
        cmin0 = jnp.min(jnp.min(jnp.where(finite, x, jnp.inf), axis=0), axis=1, keepdims=True)
        def count_ge(p):
            return jnp.sum(jnp.sum(jnp.where(x >= p[None], 1.0, 0.0), axis=0), axis=1, keepdims=True)

        t = _kth_largest_bracketed(count_ge, stats, cmin0, cmax0, float(topk), 14)
        n_gt =jnp.sum(jnp.sum(jnp.where(x > t[None], 1.0, 0.0), axis=0), axis=1, keepdims=True)
        n_ge = jnp.sum(jnp.sum(jnp.where(x >= t[None], 1.0, 0.0), axis=0), axis=1, keepdims=True)
        ties_sm[0] = (jnp.max(n_ge) > float(topk)).astype(jnp.int32)
        t_sc[...] = jnp.broadcast_to(t, t_sc.shape)
        need_sc[...] = jnp.broadcast_to(float(topk) - n_gt, need_sc.shape)
        carry_sc[...] = jnp.zeros(carry_sc.shape, F32)
        m_sc[...] = jnp.full(m_sc.shape, NEG, F32)
        l_sc[...] = jnp.zeros(l_sc.shape, F32)
        acc_sc[...] = jnp.zeros(acc_sc.shape, F32)

    qa = qa_ref[0] * HEAD_DIM ** -0.5
    qg = [jnp.concatenate([qa[:, (g * hpg + h) * HEAD_DIM:(g * hpg + h + 1) * HEAD_DIM] for h in range(hpg)],
                          axis=0).astype(BF16) for g in range(A_KV_HEADS)]
    t = t_sc[...][:, :1]
    need = need_sc[...][:, :1]
    tri = tri_ref[...]

    def tie_bias(x):
        carry = carry_sc[...][:, :1]
        biases = []
        for xb in _lane_blocks(x):
            bias, carry = _tie_bias(xb, t, need, carry, tri)
            biases.append(bias)
        carry_sc[...] = jnp.broadcast_to(carry, carry_sc.shape)
        return jnp.concatenate(biases, axis=1) if len(biases) > 1 else biases[0]

    def attend(x, kv_t):
        bias = lax.cond(ties_sm[0] > 0, tie_bias, lambda xs: jnp.where(xs >= t, 0.0, NEG), x)
        bias4 = jnp.concatenate([bias] * hpg, axis=0)
        for g in range(A_KV_HEADS):
            kc = kv_t[g * HEAD_DIM:(g + 1) * HEAD_DIM].astype(BF16)
            vc = kv_t[(A_KV_HEADS + g) * HEAD_DIM:(A_KV_HEADS + g + 1) * HEAD_DIM].astype(BF16)
            _softmax_step(_dot(qg[g], kc) + bias4, vc, m_sc, l_sc, acc_sc, g, values_transposed=True)

    @pl.when(j < last)
    def _():
        attend(sc_ref[0, j], _cat_lanes(pages))

    @pl.when(j == last)
    def _():
        attend(sc_ref[0, j][:, :PAGE_SIZE], kvnew_ref[0])
        outs = []
        for g in range(A_KV_HEADS):
            og = acc_sc[g] / l_sc[g][:, :HEAD_DIM]
            outs += [og[h * SLOT:(h + 1) * SLOT] for h in range(hpg)]
        o_ref[0] = jnp.concatenate(outs, axis=1)


def _dsa_sample_attend(page_table, qa, scores, kvnew, cache_kv, layer, group, topk):
    db, n_pages = page_table.shape
    nch = n_pages // group + 1
    gw = group * PAGE_SIZE
    hpg = A_HEADS // A_KV_HEADS
    kvw = 2 * A_KV_HEADS * HEAD_DIM
    grid_spec = pltpu.PrefetchScalarGridSpec(
        num_scalar_prefetch=1, grid=(db, nch),
        in_specs=[pl.BlockSpec((1, SLOT, qa.shape[2]), lambda b, j, pt: (b, 0, 0)),
                  pl.BlockSpec((1, nch, SLOT, gw), lambda b, j, pt: (b, 0, 0, 0)),
                  pl.BlockSpec((1, kvw, PAGE_SIZE), lambda b, j, pt: (b, 0, 0)),
                  pl.BlockSpec((PAGE_SIZE, PAGE_SIZE), lambda b, j, pt: (0, 0))]
                 + _page_specs(kvw, layer, group, n_pages),
        out_specs=pl.BlockSpec((1, SLOT, A_HEADS * HEAD_DIM), lambda b, j, pt: (b, 0, 0)),
        scratch_shapes=[pltpu.VMEM((SLOT, LANES), F32), pltpu.VMEM((SLOT, LANES), F32), pltpu.VMEM((SLOT, LANES), F32),
                        pltpu.VMEM((A_KV_HEADS, hpg * SLOT, LANES), F32),
                        pltpu.VMEM((A_KV_HEADS, hpg * SLOT, LANES), F32),
                        pltpu.VMEM((A_KV_HEADS, hpg * SLOT, HEAD_DIM), F32),
                        pltpu.SMEM((1,), jnp.int32)])
    return pl.pallas_call(
        functools.partial(_dsa_sample_attend_body, group=group, topk=topk), grid_spec=grid_spec,
        out_shape=jax.ShapeDtypeStruct((db, SLOT, A_HEADS * HEAD_DIM), F32),
        compiler_params=_cp("arbitrary", "arbitrary"), name="dsa_sample_attend",
    )(page_table, qa, scores, kvnew, _upper_tri_bf16(PAGE_SIZE), *([cache_kv] * group))


def _diff_sample_body(pt_ref, q_ref, knew_ref, vnew_ref, lam_ref, sub_ref, *rest, group, lam_init):
    kpages, vpages, o_ref = rest[:group], rest[group:2 * group], rest[2 * group]
    m_sc, l_sc, acc_sc = rest[2 * group + 1:]
    j = pl.program_id(1)
    last = pl.num_programs(1) - 1
    nmap = 2 * D_HEADS
    w = nmap * HEAD_DIM

    @pl.when(j == 0)
    def _():
        m_sc[...] = jnp.full(m_sc.shape, NEG, F32)
        l_sc[...] = jnp.zeros(l_sc.shape, F32)
        acc_sc[...] = jnp.zeros(acc_sc.shape, F32)

    q = q_ref[0] * HEAD_DIM ** -0.5
    lane_map = _iota((SLOT, w), 1) // HEAD_DIM
    qb = jnp.concatenate([jnp.where(lane_map == m * D_HEADS + h, q, 0.0)
                          for h in range(D_HEADS) for m in range(2)], axis=0).astype(BF16)
    rows_h = 2 * SLOT

    def step(k_t, v_refs, mask):
        s = _dot(qb, k_t.astype(BF16))
        if mask is not None:
            s = jnp.where(mask, s, NEG)
        p, alpha, m_new, l_new = _softmax_probs(_lane_blocks(s), m_sc[...], l_sc[...])
        m_sc[...] = m_new
        l_sc[...] = l_new
        for h in range(D_HEADS):
            vh = jnp.concatenate([r[pl.ds(h, PAGE_SIZE, stride=D_HEADS), :] for r in v_refs], axis=0).astype(BF16)
            rs = slice(h * rows_h, (h + 1) * rows_h)
            acc_sc[h] = alpha[rs] * acc_sc[h] + _dot(p[rs], vh)

    @pl.when(j < last)
    def _():
        step(_cat_lanes(kpages), vpages, None)

    @pl.when(j == last)
    def _():
        step(knew_ref[0], [vnew_ref.at[0]], _new_key_valid(nmap * SLOT, PAGE_SIZE))
        lam = _diff_lambda(lam_ref, lam_init)
        l = l_sc[...]
        outs = []
        for h in range(D_HEADS):
            acc = acc_sc[h]
            lh = l[h * rows_h:(h + 1) * rows_h]
            outs.append(_diff_finish(acc[:SLOT], lh[:SLOT], acc[SLOT:], lh[SLOT:], lam, sub_ref[...], lam_init))
        o_ref[0] = jnp.concatenate(outs, axis=1)


def _diff_sample(page_table, q, knew_t, vnew, lam_par, subln, cache_k_t, cache_v, layer, group, lam_init):
    db, n_pages = page_table.shape
    nch = n_pages // group + 1
    nmap = 2 * D_HEADS
    w = q.shape[2]
    vw = D_HEADS * D_VDIM
    vrows = PAGE_SIZE * D_HEADS
    grid_spec = pltpu.PrefetchScalarGridSpec(
        num_scalar_prefetch=1, grid=(db, nch),
        in_specs=[pl.BlockSpec((1, SLOT, w), lambda b, j, pt: (b, 0, 0)),
                  pl.BlockSpec((1, w, PAGE_SIZE), lambda b, j, pt: (b, 0, 0)),
                  pl.BlockSpec((1, vrows, D_VDIM), lambda b, j, pt: (b, 0, 0)),
                  pl.BlockSpec(lam_par.shape, lambda b, j, pt: (0, 0)),
                  pl.BlockSpec((1, D_VDIM), lambda b, j, pt: (0, 0))]
                 + _page_specs(w, layer, group, n_pages) + _page_specs(vrows, layer, group, n_pages),
        out_specs=pl.BlockSpec((1, SLOT, vw), lambda b, j, pt: (b, 0, 0)),
        scratch_shapes=[pltpu.VMEM((nmap * SLOT, LANES), F32), pltpu.VMEM((nmap * SLOT, LANES), F32),
                        pltpu.VMEM((D_HEADS, 2 * SLOT, D_VDIM), F32)])
    return pl.pallas_call(
        functools.partial(_diff_sample_body, group=group, lam_init=lam_init), grid_spec=grid_spec,
        out_shape=jax.ShapeDtypeStruct((db, SLOT, vw), F32),
        compiler_params=_cp("arbitrary", "arbitrary"), name="diff_sample",
    )(page_table, q, knew_t, vnew, lam_par, subln.reshape(1, D_VDIM), *([cache_k_t] * group), *([cache_v] * group))


def _split_cols(w, sizes):
    out, o = [], 0
    for s in sizes:
        out.append(w[:, o:o + s])
        o += s
    return out


def _pad_cols(w, n):
    return jnp.pad(w, ((0, 0), (0, n - w.shape[1])))


def _pad_rows(a, n):
    return jnp.pad(a, ((0, 0), (0, n - a.shape[1]), (0, 0)))


def _keys_t(a):
    return jnp.pad(jnp.swapaxes(a, 1, 2), ((0, 0), (0, 0), (0, PAGE_SIZE - a.shape[1])))


def _pos_minor(cache):
    nd = cache.ndim
    t = jnp.transpose(cache, (0, 1) + tuple(range(3, nd)) + (2,))
    return t.reshape(cache.shape[0], cache.shape[1], -1, cache.shape[2])


def _ab_weights(w_in):
    hd = HEAD_DIM
    aq, ak, av, iq, ik, iw, bq, bk, bv, bg = _split_cols(
        w_in, (A_HEADS * hd, A_KV_HEADS * hd, A_KV_HEADS * hd, IDX_HEADS * IDX_DIM, IDX_DIM, IDX_HEADS,
               B_HEADS * hd, B_HEADS * hd, B_HEADS * hd, B_HEADS * hd))
    segs = [aq, jnp.concatenate([ak, av], axis=1), iq, _pad_cols(jnp.concatenate([ik, iw], axis=1), LANES), bq, bk, bv, bg]
    return [s.astype(BF16) for s in segs]


def _cd_weights(w_in):
    hd = HEAD_DIM
    cq, ck, cv, ci, cf, co, dq, dk, dv = _split_cols(
        w_in, (C_HEADS * hd, C_HEADS * hd, C_HEADS * hd, C_HEADS, C_HEADS, C_HEADS * hd,
               2 * D_HEADS * hd, 2 * D_HEADS * hd, D_HEADS * D_VDIM))
    segs = [cq, ck, cv, co, _pad_cols(jnp.concatenate([ci, cf], axis=1), LANES), dq, dk, dv]
    return [s.astype(BF16) for s in segs]


def kernel(x_prompt, x_sample, c_prompt, c_sample, page_table, cache_a_kv, cache_a_idx, cache_d_k, cache_d_v, state_ret, state_mlstm_c, state_mlstm_n, state_mlstm_m, state_ffn_conv, norm_pre_mix, norm_post_mix, norm_pre_ffn, norm_post_ffn, w_ada, b_ada, w_ab_in, w_ab_out, ret_gain, w_cd_in, w_cd_out, c_gate_bias, c_gain, d_lambda, d_subln, ffn_w_up, ffn_conv_w, ffn_conv_b, ffn_w_down):
    b, s, d = x_prompt.shape
    db, t, _ = x_sample.shape
    depth = w_ada.shape[0]
    n_pages = page_table.shape[1]
    past = n_pages * PAGE_SIZE
    d_ff = ffn_w_down.shape[1]
    assert t == SLOT - 2 * SLOT_LO and past >= TOPK_MAX and s % 1024 == 0
    hi = SLOT_LO + t
    ms = db * SLOT
    tm_in, tm_ffn, tf = 512, 256, 256
    group = math.gcd(n_pages, 16)
    group_a = math.gcd(n_pages, 32)
    topk_s = min(TOPK_MAX, (past + t) // 4)

    n_c = b + db
    c_all = jnp.pad(jnp.concatenate([c_prompt, c_sample], axis=0), ((0, -n_c % 8), (0, 0)))
    mods = _modulation(c_all, w_ada, b_ada)
    mods_p = mods[:, :b].reshape(depth, b, 1, 6 * d)
    mods_s = jnp.repeat(mods[:, b:n_c], SLOT, axis=1).reshape(depth, 1, ms, 6 * d)
    w_up, w_down = ffn_w_up.astype(BF16), ffn_w_down.astype(BF16)
    w_ab_in, w_ab_out = w_ab_in.astype(BF16), w_ab_out.astype(BF16)
    w_cd_in, w_cd_out = w_cd_in.astype(BF16), w_cd_out.astype(BF16)

    xp = x_prompt.reshape(b * s, d)
    xs = jnp.pad(x_sample, ((0, 0), (SLOT_LO, SLOT - hi), (0, 0))).reshape(ms, d)

    pos_s = past + (np.arange(ms) % SLOT) - SLOT_LO
    tp_p, tf_p = _rope_tables(np.arange(s), HEAD_DIM // 4, ROPE_THETA), _rope_tables(np.arange(s), HEAD_DIM, RET_THETA)
    tp_s, tf_s = _rope_tables(pos_s, HEAD_DIM // 4, ROPE_THETA), _rope_tables(pos_s, HEAD_DIM, RET_THETA)

    cache_a_kv = _pos_minor(cache_a_kv)
    cache_a_idx = _pos_minor(cache_a_idx)
    cache_d_k = _pos_minor(cache_d_k)
    cache_d_v = cache_d_v.reshape(*cache_d_v.shape[:2], -1, D_VDIM)

    hd = HEAD_DIM
    w512 = 8 * hd
    ab_defs = lambda dt: [(w512, dt), (4 * hd, F32), (4 * hd, dt), (LANES, F32), (w512, dt), (w512, F32), (w512, dt), (w512, F32)]
    cd_defs = lambda dt: [(w512, dt), (w512, F32), (w512, dt), (w512, F32), (LANES, F32), (w512, dt), (w512, F32), (w512, F32)]
    zeros = lambda *shape: jnp.zeros(shape, F32)

    outs = {k: [] for k in ("pa_kv", "pa_idx", "pd_k", "pd_v", "p_ret", "p_mc", "p_mn", "p_mm", "p_cv",
                            "sa_kv", "sa_idx", "sd_k", "sd_v", "s_ret", "s_mc", "s_mn", "s_mm", "s_cv")}
    for l in range(depth):
        p = l // 2
        if l % 2 == 0:
            wts = _ab_weights(w_ab_in[p])
            wo = w_ab_out[p].astype(BF16)
            qa, kv, qi, kiw, bq, bk, bv, bg = _in_proj(_ab_in_body, "ab_in", xp, norm_pre_mix[l], mods_p, l,
                                                       [tp_p, tf_p], [], wts, ab_defs(BF16), tm_in, s // tm_in)
            r3 = lambda a: a.reshape(b, s, a.shape[-1])
            a_out = _dsa_prompt(r3(qa), r3(qi), r3(kiw), r3(kv))
            b_out, st = _retention(r3(bq), r3(bk), r3(bv), r3(bg), zeros(b, B_HEADS, hd, hd), ret_gain[p],
                                   CHUNK, 0, CHUNK, BF16)
            outs["pa_kv"].append(kv.reshape(b, s, 2, A_KV_HEADS, hd))
            outs["pa_idx"].append(r3(kiw)[:, :, :IDX_DIM])
            outs["p_ret"].append(st)
            mix_p = (a_out.reshape(b * s, -1), b_out.reshape(b * s, -1))
            qa, kv, qi, kiw, bq, bk, bv, bg = _in_proj(_ab_in_body, "ab_in", xs, norm_pre_mix[l], mods_s, l,
                                                       [tp_s, tf_s], [], wts, ab_defs(F32), ms, 1)
            r3 = lambda a: a.reshape(db, SLOT, a.shape[-1])
            scores = _dsa_sample_scores(page_table, r3(qi), r3(kiw), _keys_t(r3(kiw)[:, :, :IDX_DIM]), cache_a_idx, p, group_a)
            a_out = _dsa_sample_attend(page_table, r3(qa), scores, _keys_t(r3(kv)), cache_a_kv, p, group_a, topk_s)
            b_out, st = _retention(r3(bq), r3(bk), r3(bv), r3(bg), state_ret[p], ret_gain[p], SLOT, SLOT_LO, hi, F32)
            outs["sa_kv"].append(r3(kv)[:, SLOT_LO:hi].reshape(db, t, 2, A_KV_HEADS, hd))
            outs["sa_idx"].append(r3(kiw)[:, SLOT_LO:hi, :IDX_DIM])
            outs["s_ret"].append(st)
            mix_s = (a_out.reshape(ms, -1), b_out.reshape(ms, -1))
        else:
            lam_init = 0.8 - 0.6 * math.exp(-0.3 * l)
            wts = _cd_weights(w_cd_in[p])
            wo = w_cd_out[p].astype(BF16)
            gbias = jnp.pad(c_gate_bias[p], (0, LANES - 2 * C_HEADS)).reshape(1, LANES)
            cq, ck, cv, co, gt, dq, dk, dv = _in_proj(_cd_in_body, "cd_in", xp, norm_pre_mix[l], mods_p, l,
                                                      [tp_p], [gbias], wts, cd_defs(BF16), tm_in, s // tm_in)
            r3 = lambda a: a.reshape(b, s, a.shape[-1])
            c_out, mc, mn, mm = _mlstm(r3(cq), r3(ck), r3(cv), r3(co), r3(gt), zeros(b, C_HEADS, hd, hd),
                                       zeros(b, C_HEADS, hd), zeros(b, C_HEADS), c_gain[p], CHUNK, 0, CHUNK, BF16)
            d_out = _diff_prompt(r3(dq), r3(dk), r3(dv), d_lambda[p], d_subln[p], lam_init)
            outs["pd_k"].append(dk.reshape(b, s, 2, D_HEADS, hd))
            outs["pd_v"].append(dv.reshape(b, s, D_HEADS, D_VDIM))
            outs["p_mc"].append(mc); outs["p_mn"].append(mn); outs["p_mm"].append(mm)
            mix_p = (c_out.reshape(b * s, -1), d_out.reshape(b * s, -1))
            cq, ck, cv, co, gt, dq, dk, dv = _in_proj(_cd_in_body, "cd_in", xs, norm_pre_mix[l], mods_s, l,
                                                      [tp_s], [gbias], wts, cd_defs(F32), ms, 1)
            r3 = lambda a: a.reshape(db, SLOT, a.shape[-1])
            c_out, mc, mn, mm = _mlstm(r3(cq), r3(ck), r3(cv), r3(co), r3(gt), state_mlstm_c[p], state_mlstm_n[p],
                                       state_mlstm_m[p], c_gain[p], SLOT, SLOT_LO, hi, F32)
            vnew = _pad_rows(dv.reshape(db, SLOT * D_HEADS, D_VDIM), PAGE_SIZE * D_HEADS)
            d_out = _diff_sample(page_table, r3(dq), _keys_t(r3(dk)), vnew,
                                 d_lambda[p], d_subln[p], cache_d_k, cache_d_v, p, group, lam_init)
            outs["sd_k"].append(r3(dk)[:, SLOT_LO:hi].reshape(db, t, 2, D_HEADS, hd))
            outs["sd_v"].append(r3(dv)[:, SLOT_LO:hi].reshape(db, t, D_HEADS, D_VDIM))
            outs["s_mc"].append(mc); outs["s_mn"].append(mn); outs["s_mm"].append(mm)
            mix_s = (c_out.reshape(ms, -1), d_out.reshape(ms, -1))

        xp, tails = _ffn(xp, *mix_p, wo, norm_post_mix[l], norm_pre_ffn[l], mods_p, w_up, w_down, l,
                         ffn_conv_w[l], ffn_conv_b[l], None, norm_post_ffn[l], tm_ffn, s // tm_ffn, tf)
        outs["p_cv"].append(tails.reshape(b, s // tm_ffn, 2, d_ff)[:, -1])
        inj = jnp.pad(state_ffn_conv[l], ((0, 0), (0, SLOT - SLOT_LO), (0, 0))).reshape(ms, d_ff)
        xs, a_all = _ffn(xs, *mix_s, wo, norm_post_mix[l], norm_pre_ffn[l], mods_s, w_up, w_down, l,
                         ffn_conv_w[l], ffn_conv_b[l], inj, norm_post_ffn[l], ms, 1, tf)
        outs["s_cv"].append(a_all.reshape(db, SLOT, d_ff)[:, hi - (CONV_W - 1):hi])

    st = lambda k: jnp.stack(outs[k])
    y_sample = xs.reshape(db, SLOT, d)[:, SLOT_LO:hi]
    return (xp.reshape(b, s, d), y_sample,
            st("pa_kv"), st("pa_idx"), st("pd_k"), st("pd_v"), st("p_ret"), st("p_mc"), st("p_mn"), st("p_mm"), st("p_cv"),
            st("sa_kv"), st("sa_idx"), st("sd_k"), st("sd_v"), st("s_ret"), st("s_mc"), st("s_mn"), st("s_mm"), st("s_cv"))
```

```python
import functools
import math

import numpy as np
import jax
import jax.numpy as jnp
from jax import lax
from jax.experimental import pallas as pl
from jax.experimental.pallas import tpu as pltpu

F32 = jnp.float32
BF16 = jnp.bfloat16

PAGE_SIZE = 128
HEAD_DIM = 64
ROPE_THETA = 500000.0
RET_THETA = 10000.0
A_HEADS = 8
A_KV_HEADS = 2
IDX_HEADS = 4
IDX_DIM = 64
TOPK_MAX = 256
B_HEADS = 8
C_HEADS = 8
D_HEADS = 4
D_VDIM = 2 * HEAD_DIM
CONV_W = 3
CHUNK = 128
EPS = 1e-6
NEG = -1e30
LOG2E = 1.4426950408889634
SLOT = 8
SLOT_LO = CONV_W - 1
LANES = 128
VMEM_LIMIT = 56 * 1024 * 1024

_NT = (((1,), (1,)), ((), ()))


def _cp(*sem):
    return pltpu.CompilerParams(dimension_semantics=sem, vmem_limit_bytes=VMEM_LIMIT)


def _dot(a, b):
    return jnp.dot(a, b, preferred_element_type=F32)


def _dot_nt(a, b):
    return lax.dot_general(a, b, _NT, preferred_element_type=F32)


def _iota(shape, dim):
    return lax.broadcasted_iota(jnp.int32, shape, dim)


def _eye_bf16(n):
    return jnp.where(_iota((n, n), 0) == _iota((n, n), 1), 1.0, 0.0).astype(BF16)


def _split3(x):
    hi = x.astype(BF16)
    r1 = x - hi.astype(F32)
    mid = r1.astype(BF16)
    lo = (r1 - mid.astype(F32)).astype(BF16)
    return hi, mid, lo


def _rope_tables(pos, rot, theta):
    half = rot // 2
    inv = theta ** (-np.arange(half, dtype=np.float64) / half)
    ang = np.asarray(pos, np.float64)[:, None] * inv[None]
    cos, sin = np.cos(ang), np.sin(ang)
    n = len(pos)
    t0 = np.ones((n, HEAD_DIM)); t1 = np.zeros((n, HEAD_DIM)); t2 = np.zeros((n, HEAD_DIM))
    t0[:, :half] = cos; t0[:, half:rot] = cos
    t1[:, :half] = -sin
    t2[:, half:rot] = sin
    tab = np.stack([np.tile(t, (1, LANES // HEAD_DIM)) for t in (t0, t1, t2)])
    return jnp.asarray(tab, F32)


def _retention_tables(chunk, lo, hi):
    n = hi - lo
    log_g = np.log1p(-np.exp2(-5.0 - np.arange(B_HEADS, dtype=np.float64)))
    r = np.arange(chunk)
    ok = (r >= lo) & (r < hi)
    pos = (r - lo).astype(np.float64)
    diff = pos[:, None] - pos[None, :]
    dm = np.where((diff >= 0) & ok[:, None] & ok[None, :], np.exp(log_g[:, None, None] * np.maximum(diff, 0.0)), 0.0)
    qd = np.exp(log_g[:, None] * (pos + 1.0))[:, :, None] * np.ones((1, 1, HEAD_DIM))
    kd = np.where(ok, np.exp(log_g[:, None] * (n - 1.0 - pos)), 0.0)[:, :, None] * np.ones((1, 1, HEAD_DIM))
    cd = np.exp(log_g * n)[:, None, None] * np.ones((1, 1, HEAD_DIM))
    return tuple(jnp.asarray(a, F32) for a in (dm, qd, kd, cd))


def _mod_body(c_ref, w_ref, b_ref, o_ref):
    c = c_ref[...]
    s = (c * jax.nn.sigmoid(c)).astype(BF16)
    o_ref[0] = _dot(s, w_ref[0].astype(BF16)) + b_ref[0]


def _modulation(c_all, w_ada, b_ada):
    depth, d, n = w_ada.shape
    r = c_all.shape[0]
    tn = 1536
    return pl.pallas_call(
        _mod_body, grid=(depth, n // tn),
        in_specs=[pl.BlockSpec((r, d), lambda l, j: (0, 0)),
                  pl.BlockSpec((1, d, tn), lambda l, j: (l, 0, j)),
                  pl.BlockSpec((1, 1, tn), lambda l, j: (l, 0, j))],
        out_specs=pl.BlockSpec((1, r, tn), lambda l, j: (l, 0, j)),
        out_shape=jax.ShapeDtypeStruct((depth, r, n), F32),
        compiler_params=_cp("arbitrary", "arbitrary"), name="modulation",
    )(c_all, w_ada, b_ada.reshape(depth, 1, n))


def _rmsnorm(x, g):
    return x * lax.rsqrt(jnp.mean(x * x, axis=-1, keepdims=True) + EPS) * g


def _ada_norm_bf16(x_ref, g_ref, sh_ref, sc_ref):
    return (_rmsnorm(x_ref[...], g_ref[...]) * (1.0 + sc_ref[0]) + sh_ref[0]).astype(BF16)


def _rope128(x, tab, shift):
    xl = pltpu.roll(x, LANES - shift, 1)
    xr = pltpu.roll(x, shift, 1)
    return x * tab[0] + xl * tab[1] + xr * tab[2]


def _rope(u, tab, shift):
    return jnp.concatenate(
        [_rope128(u[:, c * LANES:(c + 1) * LANES], tab, shift) for c in range(u.shape[1] // LANES)], axis=1)


def _ab_in_body(x_ref, g_ref, sh_ref, sc_ref, tp_ref, tf_ref,
                wqa, wkv, wqi, wki, wbq, wbk, wbv, wbg,
                qa_o, kv_o, qi_o, kiw_o, bq_o, bk_o, bv_o, bg_o):
    h = _ada_norm_bf16(x_ref, g_ref, sh_ref, sc_ref)
    tp = tp_ref[...]
    tf = tf_ref[...]
    pshift, fshift = HEAD_DIM // 8, HEAD_DIM // 2
    qa_o[...] = _rope(_dot(h, wqa[...]), tp, pshift).astype(qa_o.dtype)
    u = _dot(h, wkv[...])
    kv_o[...] = jnp.concatenate([_rope128(u[:, :LANES], tp, pshift), u[:, LANES:]], axis=1)
    qi_o[...] = _rope(_dot(h, wqi[...]), tp, pshift).astype(qi_o.dtype)
    u = _dot(h, wki[...])
    kiw_o[...] = jnp.where(_iota(u.shape, 1) < IDX_DIM, _rope128(u, tp, pshift), u)
    bq_o[...] = _rope(_dot(h, wbq[...]), tf, fshift).astype(bq_o.dtype)
    bk_o[...] = _rope(_dot(h, wbk[...]), tf, fshift) * HEAD_DIM ** -0.5
    bv_o[...] = _dot(h, wbv[...]).astype(bv_o.dtype)
    bg_o[...] = _dot(h, wbg[...])


def _cd_in_body(x_ref, g_ref, sh_ref, sc_ref, tp_ref, gb_ref,
                wcq, wck, wcv, wco, wgt, wdq, wdk, wdv,
                cq_o, ck_o, cv_o, co_o, gt_o, dq_o, dk_o, dv_o):
    h = _ada_norm_bf16(x_ref, g_ref, sh_ref, sc_ref)
    tp = tp_ref[...]
    pshift = HEAD_DIM // 8
    cq_o[...] = _dot(h, wcq[...]).astype(cq_o.dtype)
    ck_o[...] = _dot(h, wck[...]) * HEAD_DIM ** -0.5
    cv_o[...] = _dot(h, wcv[...]).astype(cv_o.dtype)
    co_o[...] = _dot(h, wco[...])
    gt_o[...] = _dot(h, wgt[...]) + gb_ref[...]
    dq_o[...] = _rope(_dot(h, wdq[...]), tp, pshift).astype(dq_o.dtype)
    dk_o[...] = _rope(_dot(h, wdk[...]), tp, pshift)
    dv_o[...] = _dot(h, wdv[...])


def _mod_spec(mods, layer, comp, tps):
    r, d = mods.shape[2], mods.shape[3] // 6
    return pl.BlockSpec((None, 1, r, d), lambda i: (layer, i // tps, 0, comp))


def _in_proj(body, name, x, g, mods, layer, tables, extra, weights, out_defs, tm, tps):
    m, d = x.shape
    in_specs = [pl.BlockSpec((tm, d), lambda i: (i, 0)), pl.BlockSpec((1, d), lambda i: (0, 0)),
                _mod_spec(mods, layer, 0, tps), _mod_spec(mods, layer, 1, tps)]
    in_specs += [pl.BlockSpec((3, tm, LANES), lambda i: (0, i % tps, 0)) for _ in tables]
    in_specs += [pl.BlockSpec(e.shape, lambda i: (0, 0)) for e in extra]
    in_specs += [pl.BlockSpec(w.shape, lambda i: (0, 0)) for w in weights]
    return pl.pallas_call(
        body, grid=(m // tm,), in_specs=in_specs,
        out_specs=[pl.BlockSpec((tm, n), lambda i: (i, 0)) for n, _ in out_defs],
        out_shape=[jax.ShapeDtypeStruct((m, n), dt) for n, dt in out_defs],
        compiler_params=_cp("arbitrary"), name=name,
    )(x, g.reshape(1, d), mods, mods, *tables, *extra, *weights)


def _ffn_body(x_ref, ma_ref, mb_ref, wo_ref, g0_ref, gate0_ref, g1_ref, sh_ref, sc_ref, wu_ref, wd_ref, cw_ref, cb_ref,
              inj_ref, g2_ref, gate_ref, o_ref, aux_ref, carry_sc, *, tps, tf, inject):
    i = pl.program_id(0)
    tm = x_ref.shape[0]
    ff = wd_ref.shape[0]
    mix = jnp.concatenate([ma_ref[...].astype(BF16), mb_ref[...].astype(BF16)], axis=1)
    x = x_ref[...] + gate0_ref[0] * _rmsnorm(_dot(mix, wo_ref[...]), g0_ref[...])
    h = (_rmsnorm(x, g1_ref[...]) * (1.0 + sc_ref[0]) + sh_ref[0]).astype(BF16)
    row = _iota((tm, tf), 0)
    cw = cw_ref[...]
    cb = cb_ref[...]
    if not inject:
        @pl.when(i % tps == 0)
        def _():
            carry_sc[...] = jnp.zeros(carry_sc.shape, F32)

    def up(f):
        return _dot(h, wu_ref[:, f * tf:(f + 1) * tf]), _dot(h, wu_ref[:, ff + f * tf:ff + (f + 1) * tf])

    yv = None
    nf = ff // tf
    nxt = up(0)
    for f in range(nf):
        fs = slice(f * tf, (f + 1) * tf)
        a, b = nxt
        if f + 1 < nf:
            nxt = up(f + 1)
        if inject:
            a = jnp.where(row % SLOT < SLOT_LO, inj_ref[:, fs], a)
            aux_ref[:, fs] = a
            p1 = jnp.zeros((1, tf), F32)
            p2 = jnp.zeros((2, tf), F32)
        else:
            p2 = carry_sc[:, fs]
            carry_sc[:, fs] = a[tm - 2:, :]
            aux_ref[0, :, fs] = a[tm - 2:, :]
            p1 = p2[1:2, :]
        a1 = jnp.where(row == 0, p1, pltpu.roll(a, 1, 0))
        a2 = pltpu.roll(a, 2, 0)
        a2 = jnp.where(row == 0, p2[0:1, :], jnp.where(row == 1, p2[1:2, :], a2))
        conv = cb[:, fs] + a2 * cw[0:1, fs]
        conv = conv + a1 * cw[1:2, fs]
        conv = conv + a * cw[2:3, fs]
        y = jax.nn.gelu(conv, approximate=True) * b
        part = _dot(y.astype(BF16), wd_ref[fs, :])
        yv = part if yv is None else yv + part
    o_ref[...] = x + gate_ref[0] * _rmsnorm(yv, g2_ref[...])


def _ffn(x, mix_a, mix_b, w_out, g0, g1, mods, w_up, w_down, layer, cw, cb, inj, g2, tm, tps, tf):
    m, d = x.shape
    ff = w_down.shape[1]
    inject = inj is not None
    if inject:
        aux_shape, aux_spec = (m, ff), pl.BlockSpec((tm, ff), lambda i: (i, 0))
        inj_spec = pl.BlockSpec((tm, ff), lambda i: (i, 0))
    else:
        aux_shape, aux_spec = (m // tm, 2, ff), pl.BlockSpec((1, 2, ff), lambda i: (i, 0, 0))
        inj = jnp.zeros((8, LANES), F32)
        inj_spec = pl.BlockSpec((8, LANES), lambda i: (0, 0))
    mod = lambda comp: _mod_spec(mods, layer, comp, tps)
    fix = lambda i: (0, 0)
    return pl.pallas_call(
        functools.partial(_ffn_body, tps=tps, tf=tf, inject=inject),
        grid=(m // tm,),
        in_specs=[pl.BlockSpec((tm, d), lambda i: (i, 0)),
                  pl.BlockSpec((tm, mix_a.shape[1]), lambda i: (i, 0)), pl.BlockSpec((tm, mix_b.shape[1]), lambda i: (i, 0)),
                  pl.BlockSpec(w_out.shape, fix), pl.BlockSpec((1, d), fix), mod(2),
                  pl.BlockSpec((1, d), fix), mod(3), mod(4),
                  pl.BlockSpec((None, d, 2 * ff), lambda i: (layer, 0, 0)),
                  pl.BlockSpec((None, ff, d), lambda i: (layer, 0, 0)),
                  pl.BlockSpec((CONV_W, ff), fix), pl.BlockSpec((1, ff), fix),
                  inj_spec, pl.BlockSpec((1, d), fix), mod(5)],
        out_specs=[pl.BlockSpec((tm, d), lambda i: (i, 0)), aux_spec],
        out_shape=[jax.ShapeDtypeStruct((m, d), F32), jax.ShapeDtypeStruct(aux_shape, F32)],
        scratch_shapes=[pltpu.VMEM((2, ff), F32)],
        compiler_params=_cp("arbitrary"), name="mix_out_conv_ffn",
    )(x, mix_a, mix_b, w_out, g0.reshape(1, d), mods, g1.reshape(1, d), mods, mods, w_up, w_down, cw,
      cb.reshape(1, ff), inj, g2.reshape(1, d), mods)


def _kth_largest(stats_fn, cmin0, cmax0, k):
    def flag(lo, hi):
        return jnp.max(jnp.where(lo < hi, 1.0, 0.0))

    def body(st):
        cmin, cmax, _ = st
        mid = cmin + 0.5 * (cmax - cmin)
        p = jnp.where(mid > cmin, jnp.where(mid <= cmax, mid, cmax), cmax)
        cnt, mn_ge, mx_lt = stats_fn(p)
        is_open = cmin < cmax
        take_lo = cnt >= k
        ncmin = jnp.where(is_open, jnp.where(take_lo, mn_ge, cmin), cmin)
        ncmax = jnp.where(is_open, jnp.where(take_lo, cmax, mx_lt), cmax)
        return ncmin, ncmax, flag(ncmin, ncmax)

    return lax.while_loop(lambda st: st[2] > 0.0, body, (cmin0, cmax0, flag(cmin0, cmax0)))[0]


def _kth_largest_bracketed(count_fn, stats_fn, row_min, row_max, k, light_passes):
    def light(i, st):
        lo, hi = st
        mid = lo + 0.5 * (hi - lo)
        take = count_fn(mid) >= k
        return jnp.where(take, mid, lo), jnp.where(take, hi, mid)

    lo, hi = lax.fori_loop(0, light_passes, light, (row_min, row_max))
    _, cmin, _ = stats_fn(lo)
    cnt_hi, mn_hi, mx_hi = stats_fn(hi)
    at_top = cnt_hi >= k
    return _kth_largest(stats_fn, jnp.where(at_top, mn_hi, cmin), jnp.where(at_top, row_max, mx_hi), k)


def _lane_blocks(s):
    return [s[:, c * LANES:(c + 1) * LANES] for c in range(s.shape[1] // LANES)]


def _softmax_probs(blocks, m_prev, l_prev, exp=jnp.exp):
    mloc = functools.reduce(jnp.maximum, blocks)
    m_new = jnp.maximum(m_prev, jnp.max(mloc, axis=1, keepdims=True))
    alpha = exp(m_prev - m_new)
    ps = [exp(b - m_new) for b in blocks]
    l_new = alpha * l_prev + jnp.sum(functools.reduce(jnp.add, ps), axis=1, keepdims=True)
    p = (jnp.concatenate(ps, axis=1) if len(ps) > 1 else ps[0]).astype(BF16)
    return p, alpha, m_new, l_new


def _scale_acc(alpha, acc):
    dv = acc.shape[-1]
    if dv < LANES:
        return alpha[:, :dv] * acc
    return (alpha if dv == LANES else jnp.concatenate([alpha] * (dv // LANES), axis=1)) * acc


def _softmax_step(s, v_bf16, m_ref, l_ref, acc_ref, idx, values_transposed=False, exp=jnp.exp):
    p, alpha, m_new, l_new = _softmax_probs(_lane_blocks(s), m_ref[idx], l_ref[idx], exp)
    pv = _dot_nt(p, v_bf16) if values_transposed else _dot(p, v_bf16)
    acc_ref[idx] = _scale_acc(alpha, acc_ref[idx]) + pv
    m_ref[idx] = m_new
    l_ref[idx] = l_new


def _tie_bias(x, t, need, carry, tri):
    eq = x == t
    pref = _dot(jnp.where(eq, 1.0, 0.0).astype(BF16), tri) + carry
    bias = jnp.where(x > t, 0.0, jnp.where(eq, jnp.where(pref <= need, 0.0, NEG), NEG))
    return bias, pref[:, x.shape[1] - 1:]


def _fold8(x, op):
    parts = [x[i * 8:(i + 1) * 8] for i in range(x.shape[0] // 8)]
    while len(parts) > 1:
        parts = [op(parts[i], parts[i + 1]) for i in range(0, len(parts) - 1, 2)] + (parts[-1:] if len(parts) % 2 else [])
    return parts[0]


def _dsa_prompt_body(qa_ref, qi_ref, kiwq_ref, kiwk_ref, kv_ref, tril_ref, o_ref,
                     sct, sc, m_sc, l_sc, acc_sc, *, tq, ch, topk, light_passes):
    j = pl.program_id(1)
    q0 = j * tq
    nc = (q0 + tq + ch - 1) // ch
    qpos = q0 + _iota((1, tq), 1)
    kf = float(topk)
    inf = jnp.inf
    qi = qi_ref[0] * IDX_DIM ** -0.5
    qi4 = jnp.concatenate([qi[:, h * IDX_DIM:(h + 1) * IDX_DIM] for h in range(IDX_HEADS)], axis=0).astype(BF16)
    pick = jnp.where(_iota((8, LANES), 1) == _iota((8, LANES), 0) + IDX_DIM, 1.0, 0.0).astype(BF16)
    wt = sum(_dot_nt(pick, part) for part in _split3(kiwq_ref[0])) * IDX_HEADS ** -0.5

    def score_chunk(c, st):
        mx, mn = st
        c0 = pl.multiple_of(c * ch, ch)
        kc = kiwk_ref[0, pl.ds(c0, ch), :][:, :IDX_DIM].astype(BF16)
        lg = _dot_nt(kc, qi4)
        s = jnp.zeros((ch, tq), F32)
        for h in range(IDX_HEADS):
            s = s + jnp.maximum(lg[:, h * tq:(h + 1) * tq], 0.0) * wt[h:h + 1, :]
        s = jnp.where(s == 0.0, 0.0, s)
        valid = (c0 + _iota((ch, tq), 0)) <= qpos
        sct[c] = jnp.where(valid, s, -inf)
        mx = jnp.maximum(mx, _fold8(jnp.where(valid, s, -inf), jnp.maximum))
        mn = jnp.minimum(mn, _fold8(jnp.where(valid, s, inf), jnp.minimum))
        return mx, mn

    mx, mn = lax.fori_loop(0, nc, score_chunk, (jnp.full((8, tq), -inf, F32), jnp.full((8, tq), inf, F32)))
    active = qpos + 1 > topk
    row_max = jnp.where(active, jnp.max(mx, axis=0, keepdims=True), 0.0)
    row_min = jnp.where(active, jnp.min(mn, axis=0, keepdims=True), 0.0)

    def count_ge(p):
        def body(c, cnt):
            return cnt + _fold8(jnp.where(sct[c] >= p, 1.0, 0.0), jnp.add)
        return jnp.sum(lax.fori_loop(0, nc, body, jnp.zeros((8, tq), F32)), axis=0, keepdims=True)

    def stats(p):
        def body(c, st):
            cnt, mnge, mxlt = st
            x = sct[c]
            ge = x >= p
            return (cnt + _fold8(jnp.where(ge, 1.0, 0.0), jnp.add),
                    jnp.minimum(mnge, _fold8(jnp.where(ge, x, inf), jnp.minimum)),
                    jnp.maximum(mxlt, _fold8(jnp.where(ge, -inf, x), jnp.maximum)))
        init = (jnp.zeros((8, tq), F32), jnp.full((8, tq), inf, F32), jnp.full((8, tq), -inf, F32))
        cnt, mnge, mxlt = lax.fori_loop(0, nc, body, init)
        return (jnp.sum(cnt, axis=0, keepdims=True), jnp.min(mnge, axis=0, keepdims=True),
                jnp.max(mxlt, axis=0, keepdims=True))

    t = jnp.where(active, _kth_largest_bracketed(count_ge, stats, row_min, row_max, kf, light_passes), -inf)

    def count_sel(c, st):
        x = sct[c]
        return (st[0] + _fold8(jnp.where(x > t, 1.0, 0.0), jnp.add),
                st[1] + _fold8(jnp.where(x >= t, 1.0, 0.0), jnp.add))

    gt, ge = lax.fori_loop(0, nc, count_sel, (jnp.zeros((8, tq), F32), jnp.zeros((8, tq), F32)))
    need = kf - jnp.sum(gt, axis=0, keepdims=True)
    over = jnp.sum(ge, axis=0, keepdims=True) > kf
    has_ties = jnp.max(jnp.where(active, jnp.where(over, 1.0, 0.0), 0.0)) > 0.0
    eye = _eye_bf16(tq)

    def store_bias(c, sel_t):
        sel = _dot_nt(eye, sel_t.astype(BF16))
        sc[c] = jnp.where(sel > 0.5, 0.0, NEG)

    @pl.when(has_ties)
    def _():
        tril = tril_ref[...]

        def tie_chunk(c, carry):
            x = sct[c]
            eq = x == t
            pref = _dot(tril, jnp.where(eq, 1.0, 0.0).astype(BF16)) + carry
            store_bias(c, jnp.where(x > t, 1.0, jnp.where(eq, jnp.where(pref <= need, 1.0, 0.0), 0.0)))
            return pref[ch - 1:, :]

        lax.fori_loop(0, nc, tie_chunk, jnp.zeros((1, tq), F32))

    @pl.when(jnp.logical_not(has_ties))
    def _():
        def plain_chunk(c, carry):
            store_bias(c, jnp.where(sct[c] >= t, 1.0, 0.0))
            return carry

        lax.fori_loop(0, nc, plain_chunk, 0)

    cl = nc - 1
    sc[cl] = jnp.where((cl * ch + _iota((tq, ch), 1)) <= q0 + _iota((tq, ch), 0), sc[cl], NEG)

    qa = qa_ref[0].astype(F32) * (HEAD_DIM ** -0.5 * LOG2E)
    hpg = A_HEADS // A_KV_HEADS
    qh = [qa[:, h * HEAD_DIM:(h + 1) * HEAD_DIM].astype(BF16) for h in range(A_HEADS)]
    m_sc[...] = jnp.full(m_sc.shape, NEG, F32)
    l_sc[...] = jnp.zeros(l_sc.shape, F32)
    acc_sc[...] = jnp.zeros(acc_sc.shape, F32)

    def attend(c, carry):
        c0 = pl.multiple_of(c * ch, ch)
        bias = _lane_blocks(sc[c])
        kvc = kv_ref[0, pl.ds(c0, ch), :]
        kcs = [kvc[:, g * HEAD_DIM:(g + 1) * HEAD_DIM].astype(BF16) for g in range(A_KV_HEADS)]
        vcs = [kvc[:, (A_KV_HEADS + g) * HEAD_DIM:(A_KV_HEADS + g + 1) * HEAD_DIM].astype(BF16) for g in range(A_KV_HEADS)]
        ss, sm = {}, {}
        for i in range(A_HEADS + 1):
            if i < A_HEADS:
                ss[i] = _dot_nt(qh[i], kcs[i // hpg])
            if i >= 1:
                hh = i - 1
                g, rs = hh // hpg, pl.ds((hh % hpg) * tq, tq)
                blocks = [sb + bb for sb, bb in zip(_lane_blocks(ss.pop(hh)), bias)]
                sm[hh] = _softmax_probs(blocks, m_sc[g, rs, :], l_sc[g, rs, :], exp=jnp.exp2)
                m_sc[g, rs, :] = sm[hh][2]
                l_sc[g, rs, :] = sm[hh][3]
                if hh % hpg == hpg - 1:
                    heads = [sm.pop(g * hpg + h) for h in range(hpg)]
                    pv = _dot(jnp.concatenate([x[0] for x in heads], axis=0), vcs[g])
                    acc_sc[g] = _scale_acc(jnp.concatenate([x[1] for x in heads], axis=0), acc_sc[g]) + pv
        return carry

    lax.fori_loop(0, nc, attend, 0)
    outs = []
    for g in range(A_KV_HEADS):
        og = acc_sc[g] / l_sc[g][:, :HEAD_DIM]
        outs += [og[h * tq:(h + 1) * tq] for h in range(hpg)]
    o_ref[0] = jnp.concatenate(outs, axis=1).astype(o_ref.dtype)


def _upper_tri_bf16(n):
    return jnp.asarray(np.triu(np.ones((n, n), np.float32)), BF16)


def _dsa_prompt(qa, qi, kiw, kv, tq=128, ch=512):
    b, s, _ = qa.shape
    ch = min(ch, s)
    topk = min(TOPK_MAX, s // 4)
    hpg = A_HEADS // A_KV_HEADS
    return pl.pallas_call(
        functools.partial(_dsa_prompt_body, tq=tq, ch=ch, topk=topk, light_passes=16),
        grid=(b, s // tq),
        in_specs=[pl.BlockSpec((1, tq, qa.shape[2]), lambda i, j: (i, j, 0)),
                  pl.BlockSpec((1, tq, qi.shape[2]), lambda i, j: (i, j, 0)),
                  pl.BlockSpec((1, tq, LANES), lambda i, j: (i, j, 0)),
                  pl.BlockSpec((1, s, LANES), lambda i, j: (i, 0, 0)),
                  pl.BlockSpec((1, s, kv.shape[2]), lambda i, j: (i, 0, 0)),
                  pl.BlockSpec((ch, ch), lambda i, j: (0, 0))],
        out_specs=pl.BlockSpec((1, tq, A_HEADS * HEAD_DIM), lambda i, j: (i, j, 0)),
        out_shape=jax.ShapeDtypeStruct((b, s, A_HEADS * HEAD_DIM), BF16),
        scratch_shapes=[pltpu.VMEM((s // ch, ch, tq), F32),
                        pltpu.VMEM((s // ch, tq, ch), F32),
                        pltpu.VMEM((A_KV_HEADS, hpg * tq, LANES), F32),
                        pltpu.VMEM((A_KV_HEADS, hpg * tq, LANES), F32),
                        pltpu.VMEM((A_KV_HEADS, hpg * tq, HEAD_DIM), F32)],
        compiler_params=_cp("arbitrary", "arbitrary"), name="dsa_prompt",
    )(qa, qi, kiw, kiw, kv, jnp.asarray(np.tril(np.ones((ch, ch), np.float32)), BF16))


def _head_avg_matrix(width):
    seg = np.arange(width) // HEAD_DIM
    return jnp.asarray((seg[:, None] == seg[None, :]) / HEAD_DIM, BF16)


def _head_mean(x, avg):
    hi = x.astype(BF16)
    lo = (x - hi.astype(F32)).astype(BF16)
    return _dot(hi, avg) + _dot(lo, avg)


def _head_layernorm(o, gain, avg):
    d = o - _head_mean(o, avg)
    return d * lax.rsqrt(_head_mean(d * d, avg) + EPS) * gain


def _retention_body(q_ref, k_ref, v_ref, g_ref, s0_ref, dm_ref, qd_ref, kd_ref, cd_ref, gain_ref, avg_ref,
                    o_ref, sf_ref, s_sc):
    c = pl.program_id(1)

    @pl.when(c == 0)
    def _():
        s_sc[...] = s0_ref[0]

    q = q_ref[0].astype(BF16)
    k = k_ref[0]
    v = v_ref[0].astype(BF16)
    gt = g_ref[0]
    heads = range(B_HEADS)
    sls = [slice(h * HEAD_DIM, (h + 1) * HEAD_DIM) for h in heads]
    eye = _eye_bf16(HEAD_DIM)
    st = [s_sc[h] for h in heads]
    att = [_dot_nt(q[:, sl], k[:, sl].astype(BF16)) for sl in sls]
    qs = [_dot(q[:, sls[h]], st[h].astype(BF16)) for h in heads]
    kt = [_dot_nt(eye, (k[:, sls[h]] * kd_ref[h]).astype(BF16)) for h in heads]
    outs = [_dot((att[h] * dm_ref[h]).astype(BF16), v[:, sls[h]]) + qs[h] * qd_ref[h] for h in heads]
    for h in heads:
        s_sc[h] = st[h] * cd_ref[h] + _dot(kt[h].astype(BF16), v[:, sls[h]])
    o = _head_layernorm(jnp.concatenate(outs, axis=1), gain_ref[...], avg_ref[...])
    o_ref[0] = (gt * jax.nn.sigmoid(gt) * o).astype(o_ref.dtype)

    @pl.when(c == pl.num_programs(1) - 1)
    def _():
        sf_ref[0] = s_sc[...]


def _retention(q, k, v, g, s0, gain, chunk, lo, hi, out_dtype):
    b, s, w = q.shape
    dm, qd, kd, cd = _retention_tables(chunk, lo, hi)
    row = lambda i, c: (i, c, 0)
    fix3 = lambda i, c: (0, 0, 0)
    st_spec = pl.BlockSpec((1, B_HEADS, HEAD_DIM, HEAD_DIM), lambda i, c: (i, 0, 0, 0))
    return pl.pallas_call(
        _retention_body, grid=(b, s // chunk),
        in_specs=[pl.BlockSpec((1, chunk, w), row)] * 4 + [st_spec] +
                 [pl.BlockSpec(t.shape, fix3) for t in (dm, qd, kd, cd)] +
                 [pl.BlockSpec((1, w), lambda i, c: (0, 0)), pl.BlockSpec((w, w), lambda i, c: (0, 0))],
        out_specs=[pl.BlockSpec((1, chunk, w), row), st_spec],
        out_shape=[jax.ShapeDtypeStruct((b, s, w), out_dtype),
                   jax.ShapeDtypeStruct((b, B_HEADS, HEAD_DIM, HEAD_DIM), F32)],
        scratch_shapes=[pltpu.VMEM((B_HEADS, HEAD_DIM, HEAD_DIM), F32)],
        compiler_params=_cp("arbitrary", "arbitrary"), name="retention",
    )(q, k, v, g, s0, dm, qd, kd, cd, gain.reshape(1, w), _head_avg_matrix(w))


def _mlstm_body(q_ref, k_ref, v_ref, og_ref, gt_ref, cn0_ref, m0_ref, gain_ref, avg_ref,
                o_ref, cnf_ref, mf_ref, cn_sc, m_sc, *, lo, hi):
    c = pl.program_id(1)
    ck = q_ref.shape[1]
    nh = C_HEADS
    hd = HEAD_DIM

    @pl.when(c == 0)
    def _():
        cn_sc[...] = cn0_ref[0]
        m_sc[...] = m0_ref[0]

    q = q_ref[0].astype(BF16)
    k = k_ref[0]
    v = v_ref[0].astype(BF16)
    gates = gt_ref[0]
    row = _iota((ck, LANES), 0)
    tok = (row >= lo) & (row < hi)
    log_sig = jnp.minimum(gates, 0.0) - jnp.log1p(jnp.exp(-jnp.abs(gates)))
    lf = jnp.where(tok, log_sig, 0.0)
    ii = jnp.where(tok, gates, NEG)
    tril = jnp.where(_iota((ck, ck), 0) >= _iota((ck, ck), 1), 1.0, 0.0).astype(BF16)
    fcum = sum(_dot(tril, part) for part in _split3(lf))
    f_all = pltpu.roll(fcum, LANES - nh, 1)
    a_all = ii - f_all
    cm = a_all
    step = 1
    while step < ck:
        cm = jnp.maximum(cm, jnp.where(row >= step, pltpu.roll(cm, step, 0), -jnp.inf))
        step *= 2
    m_prev = m_sc[...]
    m_t = f_all + jnp.maximum(m_prev, cm)
    inter_all = jnp.exp(f_all + m_prev - m_t)
    floor_all = jnp.exp(-m_t)
    g_all = f_all - m_t
    f_last = f_all[ck - 1:, :]
    m_end = m_t[ck - 1:, :]
    w_all = jnp.exp(f_last + a_all - m_end)
    dec_all = jnp.exp(f_last + m_prev - m_end)
    m_sc[...] = m_end
    pick = jnp.where(_iota((nh, LANES), 0) == _iota((nh, LANES), 1), 1.0, 0.0).astype(BF16)
    a_rows = sum(_dot_nt(pick, part) for part in _split3(a_all))
    causal = _iota((ck, ck), 0) >= _iota((ck, ck), 1)
    eye = _eye_bf16(hd)
    ones = jnp.ones((ck, hd), BF16)
    n_lanes = _iota((hd, 2 * hd), 1) >= hd
    heads = range(nh)
    sls = [slice(h * hd, (h + 1) * hd) for h in heads]
    col = lambda x, h: x[:, h:h + 1]
    cn = [cn_sc[h] for h in heads]
    v1 = [jnp.concatenate([v[:, sl], ones], axis=1) for sl in sls]
    qk = [_dot_nt(q[:, sl], k[:, sl].astype(BF16)) for sl in sls]
    qc = [_dot(q[:, sls[h]], cn[h].astype(BF16)) for h in heads]
    kw = [k[:, sls[h]] * col(w_all, h) for h in heads]
    kw_hi = [x.astype(BF16) for x in kw]
    kw_lo = [(x - y.astype(F32)).astype(BF16) for x, y in zip(kw, kw_hi)]
    kt_hi = [_dot_nt(eye, x) for x in kw_hi]
    kt_lo = [_dot_nt(eye, x) for x in kw_lo]
    dmat = [jnp.exp(jnp.where(causal, col(g_all, h) + a_rows[h:h + 1, :], -jnp.inf)) for h in heads]
    nd = [_dot((qk[h] * dmat[h]).astype(BF16), v1[h]) + col(inter_all, h) * qc[h] for h in heads]
    outs = [(nd[h] / jnp.maximum(jnp.abs(pltpu.roll(nd[h], hd, 1)), col(floor_all, h)))[:, :hd] for h in heads]
    upd = [_dot(kt_hi[h].astype(BF16), v1[h]) for h in heads]
    fix = [_dot(kt_lo[h].astype(BF16), v1[h]) for h in heads]
    for h in heads:
        cn_sc[h] = col(dec_all, h) * cn[h] + upd[h] + jnp.where(n_lanes, fix[h], 0.0)
    hc = jax.nn.sigmoid(og_ref[0]) * jnp.concatenate(outs, axis=1)
    o_ref[0] = _head_layernorm(hc, gain_ref[...], avg_ref[...]).astype(o_ref.dtype)

    @pl.when(c == pl.num_programs(1) - 1)
    def _():
        cnf_ref[0] = cn_sc[...]
        mf_ref[0] = m_sc[...]


def _mlstm(q, k, v, og, gates, c0, n0, m0, gain, chunk, lo, hi, out_dtype):
    b, s, w = q.shape
    nh = C_HEADS
    row = lambda i, c: (i, c, 0)
    hd = HEAD_DIM
    cn_spec = pl.BlockSpec((1, nh, hd, 2 * hd), lambda i, c: (i, 0, 0, 0))
    m_spec = pl.BlockSpec((1, 1, LANES), lambda i, c: (i, 0, 0))
    cn0 = jnp.concatenate([c0, jnp.broadcast_to(n0[..., None], (b, nh, hd, hd))], axis=-1)
    m0p = jnp.pad(m0, ((0, 0), (0, LANES - nh))).reshape(b, 1, LANES)
    o, cnf, mf = pl.pallas_call(
        functools.partial(_mlstm_body, lo=lo, hi=hi), grid=(b, s // chunk),
        in_specs=[pl.BlockSpec((1, chunk, w), row)] * 4 + [pl.BlockSpec((1, chunk, LANES), row), cn_spec, m_spec,
                                                          pl.BlockSpec((1, w), lambda i, c: (0, 0)),
                                                          pl.BlockSpec((w, w), lambda i, c: (0, 0))],
        out_specs=[pl.BlockSpec((1, chunk, w), row), cn_spec, m_spec],
        out_shape=[jax.ShapeDtypeStruct((b, s, w), out_dtype),
                   jax.ShapeDtypeStruct((b, nh, hd, 2 * hd), F32),
                   jax.ShapeDtypeStruct((b, 1, LANES), F32)],
        scratch_shapes=[pltpu.VMEM((nh, hd, 2 * hd), F32), pltpu.VMEM((1, LANES), F32)],
        compiler_params=_cp("arbitrary", "arbitrary"), name="mlstm",
    )(q, k, v, og, gates, cn0, m0p, gain.reshape(1, w), _head_avg_matrix(w))
    return o, cnf[..., :hd], cnf[..., hd], mf[:, 0, :nh]


def _diff_lambda(lam_ref, lam_init):
    lp = lam_ref[...]
    s01 = jnp.sum(lp[0:1] * lp[1:2], axis=1, keepdims=True)
    s23 = jnp.sum(lp[2:3] * lp[3:4], axis=1, keepdims=True)
    return jnp.exp(s01) - jnp.exp(s23) + lam_init


def _diff_finish(o0, l0, o1, l1, lam, subln, lam_init):
    od = o0 / l0 - lam * (o1 / l1)
    return od * lax.rsqrt(jnp.mean(od * od, axis=-1, keepdims=True) + EPS) * subln * (1.0 - lam_init)


def _diff_prompt_body(q_ref, k_ref, v_ref, lam_ref, sub_ref, o_ref, m_sc, l_sc, acc_sc, *, lam_init):
    qi = pl.program_id(1)
    t = q_ref.shape[1]
    m_sc[...] = jnp.full(m_sc.shape, NEG, F32)
    l_sc[...] = jnp.zeros(l_sc.shape, F32)
    acc_sc[...] = jnp.zeros(acc_sc.shape, F32)
    q = (q_ref[0].astype(F32) * (HEAD_DIM ** -0.5 * LOG2E)).astype(BF16)

    def step(j, diag):
        r0 = pl.multiple_of(j * t, t)
        k = k_ref[0, pl.ds(r0, t), :].astype(BF16)
        v = v_ref[0, pl.ds(r0, t), :].astype(BF16)
        keep = (_iota((t, t), 0) >= _iota((t, t), 1)) if diag else None
        n = 2 * D_HEADS
        ss, sm = {}, {}
        for i in range(n + 2):
            if i < n:
                sl = slice(i * HEAD_DIM, (i + 1) * HEAD_DIM)
                s = _dot_nt(q[:, sl], k[:, sl])
                ss[i] = jnp.where(keep, s, NEG) if diag else s
            if 1 <= i <= n:
                sm[i - 1] = _softmax_probs(_lane_blocks(ss.pop(i - 1)), m_sc[i - 1], l_sc[i - 1], jnp.exp2)
            if i >= 2:
                idx = i - 2
                p, alpha, m_new, l_new = sm.pop(idx)
                h = idx % D_HEADS
                acc_sc[idx] = _scale_acc(alpha, acc_sc[idx]) + _dot(p, v[:, h * D_VDIM:(h + 1) * D_VDIM])
                m_sc[idx] = m_new
                l_sc[idx] = l_new

    def below_diagonal(j, carry):
        step(j, False)
        return carry

    lax.fori_loop(0, qi, below_diagonal, 0)
    step(qi, True)
    lam = _diff_lambda(lam_ref, lam_init)
    outs = [_diff_finish(acc_sc[h], l_sc[h], acc_sc[D_HEADS + h], l_sc[D_HEADS + h], lam, sub_ref[...], lam_init)
            for h in range(D_HEADS)]
    o_ref[0] = jnp.concatenate(outs, axis=1).astype(o_ref.dtype)


def _diff_prompt(q, k, v, lam_par, subln, lam_init, t=512):
    b, s, w = q.shape
    t = min(t, s)
    return pl.pallas_call(
        functools.partial(_diff_prompt_body, lam_init=lam_init), grid=(b, s // t),
        in_specs=[pl.BlockSpec((1, t, w), lambda bi, i: (bi, i, 0)),
                  pl.BlockSpec((1, s, w), lambda bi, i: (bi, 0, 0)),
                  pl.BlockSpec((1, s, w), lambda bi, i: (bi, 0, 0)),
                  pl.BlockSpec(lam_par.shape, lambda bi, i: (0, 0)),
                  pl.BlockSpec((1, D_VDIM), lambda bi, i: (0, 0))],
        out_specs=pl.BlockSpec((1, t, w), lambda bi, i: (bi, i, 0)),
        out_shape=jax.ShapeDtypeStruct((b, s, w), BF16),
        scratch_shapes=[pltpu.VMEM((2 * D_HEADS, t, LANES), F32), pltpu.VMEM((2 * D_HEADS, t, LANES), F32),
                        pltpu.VMEM((2 * D_HEADS, t, D_VDIM), F32)],
        compiler_params=_cp("arbitrary", "arbitrary"), name="diff_prompt",
    )(q, k, v, lam_par, subln.reshape(1, D_VDIM))


def _page_specs(rows, layer, group, n_pages):
    def spec(g):
        def imap(b, j, pt):
            return (layer, pt[b, jnp.minimum(j * group + g, n_pages - 1)], 0, 0)
        return pl.BlockSpec((None, None, rows, PAGE_SIZE), imap)
    return [spec(g) for g in range(group)]


def _cat_lanes(refs):
    return jnp.concatenate([r[...] for r in refs], axis=1)


def _new_key_valid(nq, nk):
    rq = _iota((nq, nk), 0) % SLOT
    rk = _iota((nq, nk), 1)
    return (rk >= SLOT_LO) & (rk < SLOT_LO + (SLOT - 2 * SLOT_LO)) & (rk <= rq)


def _idx_scores(qi4, w, keys_t_bf16):
    lg = _dot(qi4, keys_t_bf16)
    s = jnp.zeros((SLOT, keys_t_bf16.shape[1]), F32)
    for h in range(IDX_HEADS):
        s = s + jnp.maximum(lg[h * SLOT:(h + 1) * SLOT], 0.0) * w[:, h:h + 1]
    return jnp.where(s == 0.0, 0.0, s)


def _dsa_sample_scores_body(pt_ref, qi_ref, kiw_ref, knew_ref, *rest, group):
    pages, o_ref = rest[:group], rest[group]
    j = pl.program_id(1)
    last = pl.num_programs(1) - 1
    qi = qi_ref[0] * IDX_DIM ** -0.5
    qi4 = jnp.concatenate([qi[:, h * IDX_DIM:(h + 1) * IDX_DIM] for h in range(IDX_HEADS)], axis=0).astype(BF16)
    w = kiw_ref[0][:, IDX_DIM:IDX_DIM + IDX_HEADS] * IDX_HEADS ** -0.5

    @pl.when(j < last)
    def _():
        o_ref[0, 0] = _idx_scores(qi4, w, _cat_lanes(pages).astype(BF16))

    @pl.when(j == last)
    def _():
        s = _idx_scores(qi4, w, knew_ref[0].astype(BF16))
        s = jnp.where(_new_key_valid(SLOT, PAGE_SIZE), s, -jnp.inf)
        pad = jnp.full((SLOT, (group - 1) * PAGE_SIZE), -jnp.inf, F32)
        o_ref[0, 0] = jnp.concatenate([s, pad], axis=1) if group > 1 else s


def _dsa_sample_scores(page_table, qi, kiw, knew_t, cache_idx_t, layer, group):
    db, n_pages = page_table.shape
    nch = n_pages // group + 1
    gw = group * PAGE_SIZE
    grid_spec = pltpu.PrefetchScalarGridSpec(
        num_scalar_prefetch=1, grid=(db, nch),
        in_specs=[pl.BlockSpec((1, SLOT, qi.shape[2]), lambda b, j, pt: (b, 0, 0)),
                  pl.BlockSpec((1, SLOT, LANES), lambda b, j, pt: (b, 0, 0)),
                  pl.BlockSpec((1, IDX_DIM, PAGE_SIZE), lambda b, j, pt: (b, 0, 0))]
                 + _page_specs(IDX_DIM, layer, group, n_pages),
        out_specs=pl.BlockSpec((1, 1, SLOT, gw), lambda b, j, pt: (b, j, 0, 0)))
    return pl.pallas_call(
        functools.partial(_dsa_sample_scores_body, group=group), grid_spec=grid_spec,
        out_shape=jax.ShapeDtypeStruct((db, nch, SLOT, gw), F32),
        compiler_params=_cp("arbitrary", "arbitrary"), name="dsa_sample_scores",
    )(page_table, qi, kiw, knew_t, *([cache_idx_t] * group))


def _dsa_sample_attend_body(pt_ref, qa_ref, sc_ref, kvnew_ref, tri_ref, *rest, group, topk):
    pages, o_ref = rest[:group], rest[group]
    t_sc, need_sc, carry_sc, m_sc, l_sc, acc_sc, ties_sm = rest[group + 1:]
    j = pl.program_id(1)
    last = pl.num_programs(1) - 1
    hpg = A_HEADS // A_KV_HEADS

    @pl.when(j == 0)
    def _():
        x = sc_ref[0]

        def stats(p):
            ge = x >= p[None]
            cnt = jnp.sum(jnp.sum(jnp.where(ge, 1.0, 0.0), axis=0), axis=1, keepdims=True)
            mnge = jnp.min(jnp.min(jnp.where(ge, x, jnp.inf), axis=0), axis=1, keepdims=True)
            mxlt = jnp.max(jnp.max(jnp.where(ge, -jnp.inf, x), axis=0), axis=1, keepdims=True)
            return cnt, mnge, mxlt

        finite = x > -jnp.inf
        cmax0 = jnp.max(jnp.max(x, axis=0), axis=1, keepdims=True)
        cmin0 = jnp.min(jnp.min(jnp.where(finite, x, jnp.inf), axis=0), axis=1, keepdims=True)
        def count_ge(p):
            return jnp.sum(jnp.sum(jnp.where(x >= p[None], 1.0, 0.0), axis=0), axis=1, keepdims=True)

        t = _kth_largest_bracketed(count_ge, stats, cmin0, cmax0, float(topk), 14)
        n_gt =jnp.sum(jnp.sum(jnp.where(x > t[None], 1.0, 0.0), axis=0), axis=1, keepdims=True)
        n_ge = jnp.sum(jnp.sum(jnp.where(x >= t[None], 1.0, 0.0), axis=0), axis=1, keepdims=True)
        ties_sm[0] = (jnp.max(n_ge) > float(topk)).astype(jnp.int32)
        t_sc[...] = jnp.broadcast_to(t, t_sc.shape)
        need_sc[...] = jnp.broadcast_to(float(topk) - n_gt, need_sc.shape)
        carry_sc[...] = jnp.zeros(carry_sc.shape, F32)
        m_sc[...] = jnp.full(m_sc.shape, NEG, F32)
        l_sc[...] = jnp.zeros(l_sc.shape, F32)
        acc_sc[...] = jnp.zeros(acc_sc.shape, F32)

    qa = qa_ref[0] * HEAD_DIM ** -0.5
    qg = [jnp.concatenate([qa[:, (g * hpg + h) * HEAD_DIM:(g * hpg + h + 1) * HEAD_DIM] for h in range(hpg)],
                          axis=0).astype(BF16) for g in range(A_KV_HEADS)]
    t = t_sc[...][:, :1]
    need = need_sc[...][:, :1]
    tri = tri_ref[...]

    def tie_bias(x):
        carry = carry_sc[...][:, :1]
        biases = []
        for xb in _lane_blocks(x):
            bias, carry = _tie_bias(xb, t, need, carry, tri)
            biases.append(bias)
        carry_sc[...] = jnp.broadcast_to(carry, carry_sc.shape)
        return jnp.concatenate(biases, axis=1) if len(biases) > 1 else biases[0]

    def attend(x, kv_t):
        bias = lax.cond(ties_sm[0] > 0, tie_bias, lambda xs: jnp.where(xs >= t, 0.0, NEG), x)
        bias4 = jnp.concatenate([bias] * hpg, axis=0)
        for g in range(A_KV_HEADS):
            kc = kv_t[g * HEAD_DIM:(g + 1) * HEAD_DIM].astype(BF16)
            vc = kv_t[(A_KV_HEADS + g) * HEAD_DIM:(A_KV_HEADS + g + 1) * HEAD_DIM].astype(BF16)
            _softmax_step(_dot(qg[g], kc) + bias4, vc, m_sc, l_sc, acc_sc, g, values_transposed=True)

    @pl.when(j < last)
    def _():
        attend(sc_ref[0, j], _cat_lanes(pages))

    @pl.when(j == last)
    def _():
        attend(sc_ref[0, j][:, :PAGE_SIZE], kvnew_ref[0])
        outs = []
        for g in range(A_KV_HEADS):
            og = acc_sc[g] / l_sc[g][:, :HEAD_DIM]
            outs += [og[h * SLOT:(h + 1) * SLOT] for h in range(hpg)]
        o_ref[0] = jnp.concatenate(outs, axis=1)


def _dsa_sample_attend(page_table, qa, scores, kvnew, cache_kv, layer, group, topk):
    db, n_pages = page_table.shape
    nch = n_pages // group + 1
    gw = group * PAGE_SIZE
    hpg = A_HEADS // A_KV_HEADS
    kvw = 2 * A_KV_HEADS * HEAD_DIM
    grid_spec = pltpu.PrefetchScalarGridSpec(
        num_scalar_prefetch=1, grid=(db, nch),
        in_specs=[pl.BlockSpec((1, SLOT, qa.shape[2]), lambda b, j, pt: (b, 0, 0)),
                  pl.BlockSpec((1, nch, SLOT, gw), lambda b, j, pt: (b, 0, 0, 0)),
                  pl.BlockSpec((1, kvw, PAGE_SIZE), lambda b, j, pt: (b, 0, 0)),
                  pl.BlockSpec((PAGE_SIZE, PAGE_SIZE), lambda b, j, pt: (0, 0))]
                 + _page_specs(kvw, layer, group, n_pages),
        out_specs=pl.BlockSpec((1, SLOT, A_HEADS * HEAD_DIM), lambda b, j, pt: (b, 0, 0)),
        scratch_shapes=[pltpu.VMEM((SLOT, LANES), F32), pltpu.VMEM((SLOT, LANES), F32), pltpu.VMEM((SLOT, LANES), F32),
                        pltpu.VMEM((A_KV_HEADS, hpg * SLOT, LANES), F32),
                        pltpu.VMEM((A_KV_HEADS, hpg * SLOT, LANES), F32),
                        pltpu.VMEM((A_KV_HEADS, hpg * SLOT, HEAD_DIM), F32),
                        pltpu.SMEM((1,), jnp.int32)])
    return pl.pallas_call(
        functools.partial(_dsa_sample_attend_body, group=group, topk=topk), grid_spec=grid_spec,
        out_shape=jax.ShapeDtypeStruct((db, SLOT, A_HEADS * HEAD_DIM), F32),
        compiler_params=_cp("arbitrary", "arbitrary"), name="dsa_sample_attend",
    )(page_table, qa, scores, kvnew, _upper_tri_bf16(PAGE_SIZE), *([cache_kv] * group))


def _diff_sample_body(pt_ref, q_ref, knew_ref, vnew_ref, lam_ref, sub_ref, *rest, group, lam_init):
    kpages, vpages, o_ref = rest[:group], rest[group:2 * group], rest[2 * group]
    m_sc, l_sc, acc_sc = rest[2 * group + 1:]
    j = pl.program_id(1)
    last = pl.num_programs(1) - 1
    nmap = 2 * D_HEADS
    w = nmap * HEAD_DIM

    @pl.when(j == 0)
    def _():
        m_sc[...] = jnp.full(m_sc.shape, NEG, F32)
        l_sc[...] = jnp.zeros(l_sc.shape, F32)
        acc_sc[...] = jnp.zeros(acc_sc.shape, F32)

    q = q_ref[0] * HEAD_DIM ** -0.5
    lane_map = _iota((SLOT, w), 1) // HEAD_DIM
    qb = jnp.concatenate([jnp.where(lane_map == m * D_HEADS + h, q, 0.0)
                          for h in range(D_HEADS) for m in range(2)], axis=0).astype(BF16)
    rows_h = 2 * SLOT

    def step(k_t, v_refs, mask):
        s = _dot(qb, k_t.astype(BF16))
        if mask is not None:
            s = jnp.where(mask, s, NEG)
        p, alpha, m_new, l_new = _softmax_probs(_lane_blocks(s), m_sc[...], l_sc[...])
        m_sc[...] = m_new
        l_sc[...] = l_new
        for h in range(D_HEADS):
            vh = jnp.concatenate([r[pl.ds(h, PAGE_SIZE, stride=D_HEADS), :] for r in v_refs], axis=0).astype(BF16)
            rs = slice(h * rows_h, (h + 1) * rows_h)
            acc_sc[h] = alpha[rs] * acc_sc[h] + _dot(p[rs], vh)

    @pl.when(j < last)
    def _():
        step(_cat_lanes(kpages), vpages, None)

    @pl.when(j == last)
    def _():
        step(knew_ref[0], [vnew_ref.at[0]], _new_key_valid(nmap * SLOT, PAGE_SIZE))
        lam = _diff_lambda(lam_ref, lam_init)
        l = l_sc[...]
        outs = []
        for h in range(D_HEADS):
            acc = acc_sc[h]
            lh = l[h * rows_h:(h + 1) * rows_h]
            outs.append(_diff_finish(acc[:SLOT], lh[:SLOT], acc[SLOT:], lh[SLOT:], lam, sub_ref[...], lam_init))
        o_ref[0] = jnp.concatenate(outs, axis=1)


def _diff_sample(page_table, q, knew_t, vnew, lam_par, subln, cache_k_t, cache_v, layer, group, lam_init):
    db, n_pages = page_table.shape
    nch = n_pages // group + 1
    nmap = 2 * D_HEADS
    w = q.shape[2]
    vw = D_HEADS * D_VDIM
    vrows = PAGE_SIZE * D_HEADS
    grid_spec = pltpu.PrefetchScalarGridSpec(
        num_scalar_prefetch=1, grid=(db, nch),
        in_specs=[pl.BlockSpec((1, SLOT, w), lambda b, j, pt: (b, 0, 0)),
                  pl.BlockSpec((1, w, PAGE_SIZE), lambda b, j, pt: (b, 0, 0)),
                  pl.BlockSpec((1, vrows, D_VDIM), lambda b, j, pt: (b, 0, 0)),
                  pl.BlockSpec(lam_par.shape, lambda b, j, pt: (0, 0)),
                  pl.BlockSpec((1, D_VDIM), lambda b, j, pt: (0, 0))]
                 + _page_specs(w, layer, group, n_pages) + _page_specs(vrows, layer, group, n_pages),
        out_specs=pl.BlockSpec((1, SLOT, vw), lambda b, j, pt: (b, 0, 0)),
        scratch_shapes=[pltpu.VMEM((nmap * SLOT, LANES), F32), pltpu.VMEM((nmap * SLOT, LANES), F32),
                        pltpu.VMEM((D_HEADS, 2 * SLOT, D_VDIM), F32)])
    return pl.pallas_call(
        functools.partial(_diff_sample_body, group=group, lam_init=lam_init), grid_spec=grid_spec,
        out_shape=jax.ShapeDtypeStruct((db, SLOT, vw), F32),
        compiler_params=_cp("arbitrary", "arbitrary"), name="diff_sample",
    )(page_table, q, knew_t, vnew, lam_par, subln.reshape(1, D_VDIM), *([cache_k_t] * group), *([cache_v] * group))


def _split_cols(w, sizes):
    out, o = [], 0
    for s in sizes:
        out.append(w[:, o:o + s])
        o += s
    return out


def _pad_cols(w, n):
    return jnp.pad(w, ((0, 0), (0, n - w.shape[1])))


def _pad_rows(a, n):
    return jnp.pad(a, ((0, 0), (0, n - a.shape[1]), (0, 0)))


def _keys_t(a):
    return jnp.pad(jnp.swapaxes(a, 1, 2), ((0, 0), (0, 0), (0, PAGE_SIZE - a.shape[1])))


def _pos_minor(cache):
    nd = cache.ndim
    t = jnp.transpose(cache, (0, 1) + tuple(range(3, nd)) + (2,))
    return t.reshape(cache.shape[0], cache.shape[1], -1, cache.shape[2])


def _ab_weights(w_in):
    hd = HEAD_DIM
    aq, ak, av, iq, ik, iw, bq, bk, bv, bg = _split_cols(
        w_in, (A_HEADS * hd, A_KV_HEADS * hd, A_KV_HEADS * hd, IDX_HEADS * IDX_DIM, IDX_DIM, IDX_HEADS,
               B_HEADS * hd, B_HEADS * hd, B_HEADS * hd, B_HEADS * hd))
    segs = [aq, jnp.concatenate([ak, av], axis=1), iq, _pad_cols(jnp.concatenate([ik, iw], axis=1), LANES), bq, bk, bv, bg]
    return segs


def _cd_weights(w_in):
    hd = HEAD_DIM
    cq, ck, cv, ci, cf, co, dq, dk, dv = _split_cols(
        w_in, (C_HEADS * hd, C_HEADS * hd, C_HEADS * hd, C_HEADS, C_HEADS, C_HEADS * hd,
               2 * D_HEADS * hd, 2 * D_HEADS * hd, D_HEADS * D_VDIM))
    segs = [cq, ck, cv, co, _pad_cols(jnp.concatenate([ci, cf], axis=1), LANES), dq, dk, dv]
    return segs


def kernel(x_prompt, x_sample, c_prompt, c_sample, page_table, cache_a_kv, cache_a_idx, cache_d_k, cache_d_v, state_ret, state_mlstm_c, state_mlstm_n, state_mlstm_m, state_ffn_conv, norm_pre_mix, norm_post_mix, norm_pre_ffn, norm_post_ffn, w_ada, b_ada, w_ab_in, w_ab_out, ret_gain, w_cd_in, w_cd_out, c_gate_bias, c_gain, d_lambda, d_subln, ffn_w_up, ffn_conv_w, ffn_conv_b, ffn_w_down):
    b, s, d = x_prompt.shape
    db, t, _ = x_sample.shape
    depth = w_ada.shape[0]
    n_pages = page_table.shape[1]
    past = n_pages * PAGE_SIZE
    d_ff = ffn_w_down.shape[1]
    assert t == SLOT - 2 * SLOT_LO and past >= TOPK_MAX and s % 1024 == 0
    hi = SLOT_LO + t
    ms = db * SLOT
    tm_in, tm_ffn, tf = 512, 256, 256
    group = math.gcd(n_pages, 16)
    group_a = math.gcd(n_pages, 32)
    topk_s = min(TOPK_MAX, (past + t) // 4)

    n_c = b + db
    c_all = jnp.pad(jnp.concatenate([c_prompt, c_sample], axis=0), ((0, -n_c % 8), (0, 0)))
    mods = _modulation(c_all, w_ada, b_ada)
    mods_p = mods[:, :b].reshape(depth, b, 1, 6 * d)
    mods_s = jnp.repeat(mods[:, b:n_c], SLOT, axis=1).reshape(depth, 1, ms, 6 * d)
    w_up, w_down = ffn_w_up.astype(BF16), ffn_w_down.astype(BF16)
    w_ab_in, w_ab_out = w_ab_in.astype(BF16), w_ab_out.astype(BF16)
    w_cd_in, w_cd_out = w_cd_in.astype(BF16), w_cd_out.astype(BF16)

    xp = x_prompt.reshape(b * s, d)
    xs = jnp.pad(x_sample, ((0, 0), (SLOT_LO, SLOT - hi), (0, 0))).reshape(ms, d)

    pos_s = past + (np.arange(ms) % SLOT) - SLOT_LO
    tp_p, tf_p = _rope_tables(np.arange(s), HEAD_DIM // 4, ROPE_THETA), _rope_tables(np.arange(s), HEAD_DIM, RET_THETA)
    tp_s, tf_s = _rope_tables(pos_s, HEAD_DIM // 4, ROPE_THETA), _rope_tables(pos_s, HEAD_DIM, RET_THETA)

    cache_a_kv = _pos_minor(cache_a_kv)
    cache_a_idx = _pos_minor(cache_a_idx)
    cache_d_k = _pos_minor(cache_d_k)
    cache_d_v = cache_d_v.reshape(*cache_d_v.shape[:2], -1, D_VDIM)

    hd = HEAD_DIM
    w512 = 8 * hd
    ab_defs = lambda dt: [(w512, dt), (4 * hd, F32), (4 * hd, dt), (LANES, F32), (w512, dt), (w512, F32), (w512, dt), (w512, F32)]
    cd_defs = lambda dt: [(w512, dt), (w512, F32), (w512, dt), (w512, F32), (LANES, F32), (w512, dt), (w512, F32), (w512, F32)]
    zeros = lambda *shape: jnp.zeros(shape, F32)

    outs = {k: [] for k in ("pa_kv", "pa_idx", "pd_k", "pd_v", "p_ret", "p_mc", "p_mn", "p_mm", "p_cv",
                            "sa_kv", "sa_idx", "sd_k", "sd_v", "s_ret", "s_mc", "s_mn", "s_mm", "s_cv")}
    for l in range(depth):
        p = l // 2
        if l % 2 == 0:
            wts = _ab_weights(w_ab_in[p])
            wo = w_ab_out[p]
            qa, kv, qi, kiw, bq, bk, bv, bg = _in_proj(_ab_in_body, "ab_in", xp, norm_pre_mix[l], mods_p, l,
                                                       [tp_p, tf_p], [], wts, ab_defs(BF16), tm_in, s // tm_in)
            r3 = lambda a: a.reshape(b, s, a.shape[-1])
            a_out = _dsa_prompt(r3(qa), r3(qi), r3(kiw), r3(kv))
            b_out, st = _retention(r3(bq), r3(bk), r3(bv), r3(bg), zeros(b, B_HEADS, hd, hd), ret_gain[p],
                                   CHUNK, 0, CHUNK, BF16)
            outs["pa_kv"].append(kv.reshape(b, s, 2, A_KV_HEADS, hd))
            outs["pa_idx"].append(r3(kiw)[:, :, :IDX_DIM])
            outs["p_ret"].append(st)
            mix_p = (a_out.reshape(b * s, -1), b_out.reshape(b * s, -1))
            qa, kv, qi, kiw, bq, bk, bv, bg = _in_proj(_ab_in_body, "ab_in", xs, norm_pre_mix[l], mods_s, l,
                                                       [tp_s, tf_s], [], wts, ab_defs(F32), ms, 1)
            r3 = lambda a: a.reshape(db, SLOT, a.shape[-1])
            scores = _dsa_sample_scores(page_table, r3(qi), r3(kiw), _keys_t(r3(kiw)[:, :, :IDX_DIM]), cache_a_idx, p, group_a)
            a_out = _dsa_sample_attend(page_table, r3(qa), scores, _keys_t(r3(kv)), cache_a_kv, p, group_a, topk_s)
            b_out, st = _retention(r3(bq), r3(bk), r3(bv), r3(bg), state_ret[p], ret_gain[p], SLOT, SLOT_LO, hi, F32)
            outs["sa_kv"].append(r3(kv)[:, SLOT_LO:hi].reshape(db, t, 2, A_KV_HEADS, hd))
            outs["sa_idx"].append(r3(kiw)[:, SLOT_LO:hi, :IDX_DIM])
            outs["s_ret"].append(st)
            mix_s = (a_out.reshape(ms, -1), b_out.reshape(ms, -1))
        else:
            lam_init = 0.8 - 0.6 * math.exp(-0.3 * l)
            wts = _cd_weights(w_cd_in[p])
            wo = w_cd_out[p]
            gbias = jnp.pad(c_gate_bias[p], (0, LANES - 2 * C_HEADS)).reshape(1, LANES)
            cq, ck, cv, co, gt, dq, dk, dv = _in_proj(_cd_in_body, "cd_in", xp, norm_pre_mix[l], mods_p, l,
                                                      [tp_p], [gbias], wts, cd_defs(BF16), tm_in, s // tm_in)
            r3 = lambda a: a.reshape(b, s, a.shape[-1])
            c_out, mc, mn, mm = _mlstm(r3(cq), r3(ck), r3(cv), r3(co), r3(gt), zeros(b, C_HEADS, hd, hd),
                                       zeros(b, C_HEADS, hd), zeros(b, C_HEADS), c_gain[p], CHUNK, 0, CHUNK, BF16)
            d_out = _diff_prompt(r3(dq), r3(dk), r3(dv), d_lambda[p], d_subln[p], lam_init)
            outs["pd_k"].append(dk.reshape(b, s, 2, D_HEADS, hd))
            outs["pd_v"].append(dv.reshape(b, s, D_HEADS, D_VDIM))
            outs["p_mc"].append(mc); outs["p_mn"].append(mn); outs["p_mm"].append(mm)
            mix_p = (c_out.reshape(b * s, -1), d_out.reshape(b * s, -1))
            cq, ck, cv, co, gt, dq, dk, dv = _in_proj(_cd_in_body, "cd_in", xs, norm_pre_mix[l], mods_s, l,
                                                      [tp_s], [gbias], wts, cd_defs(F32), ms, 1)
            r3 = lambda a: a.reshape(db, SLOT, a.shape[-1])
            c_out, mc, mn, mm = _mlstm(r3(cq), r3(ck), r3(cv), r3(co), r3(gt), state_mlstm_c[p], state_mlstm_n[p],
                                       state_mlstm_m[p], c_gain[p], SLOT, SLOT_LO, hi, F32)
            vnew = _pad_rows(dv.reshape(db, SLOT * D_HEADS, D_VDIM), PAGE_SIZE * D_HEADS)
            d_out = _diff_sample(page_table, r3(dq), _keys_t(r3(dk)), vnew,
                                 d_lambda[p], d_subln[p], cache_d_k, cache_d_v, p, group, lam_init)
            outs["sd_k"].append(r3(dk)[:, SLOT_LO:hi].reshape(db, t, 2, D_HEADS, hd))
            outs["sd_v"].append(r3(dv)[:, SLOT_LO:hi].reshape(db, t, D_HEADS, D_VDIM))
            outs["s_mc"].append(mc); outs["s_mn"].append(mn); outs["s_mm"].append(mm)
            mix_s = (c_out.reshape(ms, -1), d_out.reshape(ms, -1))

        xp, tails = _ffn(xp, *mix_p, wo, norm_post_mix[l], norm_pre_ffn[l], mods_p, w_up, w_down, l,
                         ffn_conv_w[l], ffn_conv_b[l], None, norm_post_ffn[l], tm_ffn, s // tm_ffn, tf)
        outs["p_cv"].append(tails.reshape(b, s // tm_ffn, 2, d_ff)[:, -1])
        inj = jnp.pad(state_ffn_conv[l], ((0, 0), (0, SLOT - SLOT_LO), (0, 0))).reshape(ms, d_ff)
        xs, a_all = _ffn(xs, *mix_s, wo, norm_post_mix[l], norm_pre_ffn[l], mods_s, w_up, w_down, l,
                         ffn_conv_w[l], ffn_conv_b[l], inj, norm_post_ffn[l], ms, 1, tf)
        outs["s_cv"].append(a_all.reshape(db, SLOT, d_ff)[:, hi - (CONV_W - 1):hi])

    st = lambda k: jnp.stack(outs[k])
    y_sample = xs.reshape(db, SLOT, d)[:, SLOT_LO:hi]
    return (xp.reshape(b, s, d), y_sample,
            st("pa_kv"), st("pa_idx"), st("pd_k"), st("pd_v"), st("p_ret"), st("p_mc"), st("p_mn"), st("p_mm"), st("p_cv"),
            st("sa_kv"), st("sa_idx"), st("sd_k"), st("sd_v"), st("s_ret"), st("s_mc"), st("s_mn"), st("s_mm"), st("s_cv"))
```

```python
import functools
import math

import numpy as np
import jax
import jax.numpy as jnp
from jax import lax
from jax.experimental import pallas as pl
from jax.experimental.pallas import tpu as pltpu

F32 = jnp.float32
BF16 = jnp.bfloat16

PAGE_SIZE = 128
HEAD_DIM = 64
ROPE_THETA = 500000.0
RET_THETA = 10000.0
A_HEADS = 8
A_KV_HEADS = 2
IDX_HEADS = 4
IDX_DIM = 64
TOPK_MAX = 256
B_HEADS = 8
C_HEADS = 8
D_HEADS = 4
D_VDIM = 2 * HEAD_DIM
CONV_W = 3
CHUNK = 128
EPS = 1e-6
NEG = -1e30
LOG2E = 1.4426950408889634
SLOT = 8
SLOT_LO = CONV_W - 1
LANES = 128
VMEM_LIMIT = 56 * 1024 * 1024

_NT = (((1,), (1,)), ((), ()))


def _cp(*sem):
    return pltpu.CompilerParams(dimension_semantics=sem, vmem_limit_bytes=VMEM_LIMIT)


def _dot(a, b):
    return jnp.dot(a, b, preferred_element_type=F32)


def _dot_nt(a, b):
    return lax.dot_general(a, b, _NT, preferred_element_type=F32)


def _iota(shape, dim):
    return lax.broadcasted_iota(jnp.int32, shape, dim)


def _eye_bf16(n):
    return jnp.where(_iota((n, n), 0) == _iota((n, n), 1), 1.0, 0.0).astype(BF16)


def _split3(x):
    hi = x.astype(BF16)
    r1 = x - hi.astype(F32)
    mid = r1.astype(BF16)
    lo = (r1 - mid.astype(F32)).astype(BF16)
    return hi, mid, lo


def _rope_tables(pos, rot, theta):
    half = rot // 2
    inv = theta ** (-np.arange(half, dtype=np.float64) / half)
    ang = np.asarray(pos, np.float64)[:, None] * inv[None]
    cos, sin = np.cos(ang), np.sin(ang)
    n = len(pos)
    t0 = np.ones((n, HEAD_DIM)); t1 = np.zeros((n, HEAD_DIM)); t2 = np.zeros((n, HEAD_DIM))
    t0[:, :half] = cos; t0[:, half:rot] = cos
    t1[:, :half] = -sin
    t2[:, half:rot] = sin
    tab = np.stack([np.tile(t, (1, LANES // HEAD_DIM)) for t in (t0, t1, t2)])
    return jnp.asarray(tab, F32)


def _retention_tables(chunk, lo, hi):
    n = hi - lo
    log_g = np.log1p(-np.exp2(-5.0 - np.arange(B_HEADS, dtype=np.float64)))
    r = np.arange(chunk)
    ok = (r >= lo) & (r < hi)
    pos = (r - lo).astype(np.float64)
    diff = pos[:, None] - pos[None, :]
    dm = np.where((diff >= 0) & ok[:, None] & ok[None, :], np.exp(log_g[:, None, None] * np.maximum(diff, 0.0)), 0.0)
    qd = np.exp(log_g[:, None] * (pos + 1.0))[:, :, None] * np.ones((1, 1, HEAD_DIM))
    kd = np.where(ok, np.exp(log_g[:, None] * (n - 1.0 - pos)), 0.0)[:, :, None] * np.ones((1, 1, HEAD_DIM))
    cd = np.exp(log_g * n)[:, None, None] * np.ones((1, 1, HEAD_DIM))
    return tuple(jnp.asarray(a, F32) for a in (dm, qd, kd, cd))


def _mod_body(c_ref, w_ref, b_ref, o_ref):
    c = c_ref[...]
    s = (c * jax.nn.sigmoid(c)).astype(BF16)
    o_ref[0] = _dot(s, w_ref[0].astype(BF16)) + b_ref[0]


def _modulation(c_all, w_ada, b_ada):
    depth, d, n = w_ada.shape
    r = c_all.shape[0]
    tn = 1536
    return pl.pallas_call(
        _mod_body, grid=(depth, n // tn),
        in_specs=[pl.BlockSpec((r, d), lambda l, j: (0, 0)),
                  pl.BlockSpec((1, d, tn), lambda l, j: (l, 0, j)),
                  pl.BlockSpec((1, 1, tn), lambda l, j: (l, 0, j))],
        out_specs=pl.BlockSpec((1, r, tn), lambda l, j: (l, 0, j)),
        out_shape=jax.ShapeDtypeStruct((depth, r, n), F32),
        compiler_params=_cp("arbitrary", "arbitrary"), name="modulation",
    )(c_all, w_ada, b_ada.reshape(depth, 1, n))


def _rmsnorm(x, g):
    return x * lax.rsqrt(jnp.mean(x * x, axis=-1, keepdims=True) + EPS) * g


def _ada_norm_bf16(x_ref, g_ref, sh_ref, sc_ref):
    return (_rmsnorm(x_ref[...], g_ref[...]) * (1.0 + sc_ref[0]) + sh_ref[0]).astype(BF16)


def _rope128(x, tab, shift):
    xl = pltpu.roll(x, LANES - shift, 1)
    xr = pltpu.roll(x, shift, 1)
    return x * tab[0] + xl * tab[1] + xr * tab[2]


def _rope(u, tab, shift):
    return jnp.concatenate(
        [_rope128(u[:, c * LANES:(c + 1) * LANES], tab, shift) for c in range(u.shape[1] // LANES)], axis=1)


def _ab_in_body(x_ref, g_ref, sh_ref, sc_ref, tp_ref, tf_ref,
                wqa, wkv, wqi, wki, wbq, wbk, wbv, wbg,
                qa_o, kv_o, qi_o, kiw_o, bq_o, bk_o, bv_o, bg_o):
    h = _ada_norm_bf16(x_ref, g_ref, sh_ref, sc_ref)
    tp = tp_ref[...]
    tf = tf_ref[...]
    pshift, fshift = HEAD_DIM // 8, HEAD_DIM // 2
    qa_o[...] = _rope(_dot(h, wqa[...]), tp, pshift).astype(qa_o.dtype)
    u = _dot(h, wkv[...])
    kv_o[...] = jnp.concatenate([_rope128(u[:, :LANES], tp, pshift), u[:, LANES:]], axis=1)
    qi_o[...] = _rope(_dot(h, wqi[...]), tp, pshift).astype(qi_o.dtype)
    u = _dot(h, wki[...])
    kiw_o[...] = jnp.where(_iota(u.shape, 1) < IDX_DIM, _rope128(u, tp, pshift), u)
    bq_o[...] = _rope(_dot(h, wbq[...]), tf, fshift).astype(bq_o.dtype)
    bk_o[...] = _rope(_dot(h, wbk[...]), tf, fshift) * HEAD_DIM ** -0.5
    bv_o[...] = _dot(h, wbv[...]).astype(bv_o.dtype)
    bg_o[...] = _dot(h, wbg[...])


def _cd_in_body(x_ref, g_ref, sh_ref, sc_ref, tp_ref, gb_ref,
                wcq, wck, wcv, wco, wgt, wdq, wdk, wdv,
                cq_o, ck_o, cv_o, co_o, gt_o, dq_o, dk_o, dv_o):
    h = _ada_norm_bf16(x_ref, g_ref, sh_ref, sc_ref)
    tp = tp_ref[...]
    pshift = HEAD_DIM // 8
    cq_o[...] = _dot(h, wcq[...]).astype(cq_o.dtype)
    ck_o[...] = _dot(h, wck[...]) * HEAD_DIM ** -0.5
    cv_o[...] = _dot(h, wcv[...]).astype(cv_o.dtype)
    co_o[...] = _dot(h, wco[...])
    gt_o[...] = _dot(h, wgt[...]) + gb_ref[...]
    dq_o[...] = _rope(_dot(h, wdq[...]), tp, pshift).astype(dq_o.dtype)
    dk_o[...] = _rope(_dot(h, wdk[...]), tp, pshift)
    dv_o[...] = _dot(h, wdv[...])


def _mod_spec(mods, layer, comp, tps):
    r, d = mods.shape[2], mods.shape[3] // 6
    return pl.BlockSpec((None, 1, r, d), lambda i: (layer, i // tps, 0, comp))


def _in_proj(body, name, x, g, mods, layer, tables, extra, weights, out_defs, tm, tps):
    m, d = x.shape
    in_specs = [pl.BlockSpec((tm, d), lambda i: (i, 0)), pl.BlockSpec((1, d), lambda i: (0, 0)),
                _mod_spec(mods, layer, 0, tps), _mod_spec(mods, layer, 1, tps)]
    in_specs += [pl.BlockSpec((3, tm, LANES), lambda i: (0, i % tps, 0)) for _ in tables]
    in_specs += [pl.BlockSpec(e.shape, lambda i: (0, 0)) for e in extra]
    in_specs += [pl.BlockSpec(w.shape, lambda i: (0, 0)) for w in weights]
    return pl.pallas_call(
        body, grid=(m // tm,), in_specs=in_specs,
        out_specs=[pl.BlockSpec((tm, n), lambda i: (i, 0)) for n, _ in out_defs],
        out_shape=[jax.ShapeDtypeStruct((m, n), dt) for n, dt in out_defs],
        compiler_params=_cp("arbitrary"), name=name,
    )(x, g.reshape(1, d), mods, mods, *tables, *extra, *weights)


def _ffn_body(x_ref, ma_ref, mb_ref, wo_ref, g0_ref, gate0_ref, g1_ref, sh_ref, sc_ref, wu_ref, wd_ref, cw_ref, cb_ref,
              inj_ref, g2_ref, gate_ref, o_ref, aux_ref, carry_sc, *, tps, tf, inject):
    i = pl.program_id(0)
    tm = x_ref.shape[0]
    ff = wd_ref.shape[0]
    mix = jnp.concatenate([ma_ref[...].astype(BF16), mb_ref[...].astype(BF16)], axis=1)
    x = x_ref[...] + gate0_ref[0] * _rmsnorm(_dot(mix, wo_ref[...]), g0_ref[...])
    h = (_rmsnorm(x, g1_ref[...]) * (1.0 + sc_ref[0]) + sh_ref[0]).astype(BF16)
    row = _iota((tm, tf), 0)
    cw = cw_ref[...]
    cb = cb_ref[...]
    if not inject:
        @pl.when(i % tps == 0)
        def _():
            carry_sc[...] = jnp.zeros(carry_sc.shape, F32)

    def up(f):
        return _dot(h, wu_ref[:, f * tf:(f + 1) * tf]), _dot(h, wu_ref[:, ff + f * tf:ff + (f + 1) * tf])

    yv = None
    nf = ff // tf
    nxt = up(0)
    for f in range(nf):
        fs = slice(f * tf, (f + 1) * tf)
        a, b = nxt
        if f + 1 < nf:
            nxt = up(f + 1)
        if inject:
            a = jnp.where(row % SLOT < SLOT_LO, inj_ref[:, fs], a)
            aux_ref[:, fs] = a
            p1 = jnp.zeros((1, tf), F32)
            p2 = jnp.zeros((2, tf), F32)
        else:
            p2 = carry_sc[:, fs]
            carry_sc[:, fs] = a[tm - 2:, :]
            aux_ref[0, :, fs] = a[tm - 2:, :]
            p1 = p2[1:2, :]
        a1 = jnp.where(row == 0, p1, pltpu.roll(a, 1, 0))
        a2 = pltpu.roll(a, 2, 0)
        a2 = jnp.where(row == 0, p2[0:1, :], jnp.where(row == 1, p2[1:2, :], a2))
        conv = cb[:, fs] + a2 * cw[0:1, fs]
        conv = conv + a1 * cw[1:2, fs]
        conv = conv + a * cw[2:3, fs]
        y = jax.nn.gelu(conv, approximate=True) * b
        part = _dot(y.astype(BF16), wd_ref[fs, :])
        yv = part if yv is None else yv + part
    o_ref[...] = x + gate_ref[0] * _rmsnorm(yv, g2_ref[...])


def _ffn(x, mix_a, mix_b, w_out, g0, g1, mods, w_up, w_down, layer, cw, cb, inj, g2, tm, tps, tf):
    m, d = x.shape
    ff = w_down.shape[1]
    inject = inj is not None
    if inject:
        aux_shape, aux_spec = (m, ff), pl.BlockSpec((tm, ff), lambda i: (i, 0))
        inj_spec = pl.BlockSpec((tm, ff), lambda i: (i, 0))
    else:
        aux_shape, aux_spec = (m // tm, 2, ff), pl.BlockSpec((1, 2, ff), lambda i: (i, 0, 0))
        inj = jnp.zeros((8, LANES), F32)
        inj_spec = pl.BlockSpec((8, LANES), lambda i: (0, 0))
    mod = lambda comp: _mod_spec(mods, layer, comp, tps)
    fix = lambda i: (0, 0)
    return pl.pallas_call(
        functools.partial(_ffn_body, tps=tps, tf=tf, inject=inject),
        grid=(m // tm,),
        in_specs=[pl.BlockSpec((tm, d), lambda i: (i, 0)),
                  pl.BlockSpec((tm, mix_a.shape[1]), lambda i: (i, 0)), pl.BlockSpec((tm, mix_b.shape[1]), lambda i: (i, 0)),
                  pl.BlockSpec(w_out.shape, fix), pl.BlockSpec((1, d), fix), mod(2),
                  pl.BlockSpec((1, d), fix), mod(3), mod(4),
                  pl.BlockSpec((None, d, 2 * ff), lambda i: (layer, 0, 0)),
                  pl.BlockSpec((None, ff, d), lambda i: (layer, 0, 0)),
                  pl.BlockSpec((CONV_W, ff), fix), pl.BlockSpec((1, ff), fix),
                  inj_spec, pl.BlockSpec((1, d), fix), mod(5)],
        out_specs=[pl.BlockSpec((tm, d), lambda i: (i, 0)), aux_spec],
        out_shape=[jax.ShapeDtypeStruct((m, d), F32), jax.ShapeDtypeStruct(aux_shape, F32)],
        scratch_shapes=[pltpu.VMEM((2, ff), F32)],
        compiler_params=_cp("arbitrary"), name="mix_out_conv_ffn",
    )(x, mix_a, mix_b, w_out, g0.reshape(1, d), mods, g1.reshape(1, d), mods, mods, w_up, w_down, cw,
      cb.reshape(1, ff), inj, g2.reshape(1, d), mods)


def _kth_largest(stats_fn, cmin0, cmax0, k):
    def flag(lo, hi):
        return jnp.max(jnp.where(lo < hi, 1.0, 0.0))

    def body(st):
        cmin, cmax, _ = st
        mid = cmin + 0.5 * (cmax - cmin)
        p = jnp.where(mid > cmin, jnp.where(mid <= cmax, mid, cmax), cmax)
        cnt, mn_ge, mx_lt = stats_fn(p)
        is_open = cmin < cmax
        take_lo = cnt >= k
        ncmin = jnp.where(is_open, jnp.where(take_lo, mn_ge, cmin), cmin)
        ncmax = jnp.where(is_open, jnp.where(take_lo, cmax, mx_lt), cmax)
        return ncmin, ncmax, flag(ncmin, ncmax)

    return lax.while_loop(lambda st: st[2] > 0.0, body, (cmin0, cmax0, flag(cmin0, cmax0)))[0]


def _kth_largest_bracketed(count_fn, stats_fn, row_min, row_max, k, light_passes):
    def light(i, st):
        lo, hi = st
        mid = lo + 0.5 * (hi - lo)
        take = count_fn(mid) >= k
        return jnp.where(take, mid, lo), jnp.where(take, hi, mid)

    lo, hi = lax.fori_loop(0, light_passes, light, (row_min, row_max))
    _, cmin, _ = stats_fn(lo)
    cnt_hi, mn_hi, mx_hi = stats_fn(hi)
    at_top = cnt_hi >= k
    return _kth_largest(stats_fn, jnp.where(at_top, mn_hi, cmin), jnp.where(at_top, row_max, mx_hi), k)


def _lane_blocks(s):
    return [s[:, c * LANES:(c + 1) * LANES] for c in range(s.shape[1] // LANES)]


def _softmax_probs(blocks, m_prev, l_prev, exp=jnp.exp):
    mloc = functools.reduce(jnp.maximum, blocks)
    m_new = jnp.maximum(m_prev, jnp.max(mloc, axis=1, keepdims=True))
    alpha = exp(m_prev - m_new)
    ps = [exp(b - m_new) for b in blocks]
    l_new = alpha * l_prev + jnp.sum(functools.reduce(jnp.add, ps), axis=1, keepdims=True)
    p = (jnp.concatenate(ps, axis=1) if len(ps) > 1 else ps[0]).astype(BF16)
    return p, alpha, m_new, l_new


def _scale_acc(alpha, acc):
    dv = acc.shape[-1]
    if dv < LANES:
        return alpha[:, :dv] * acc
    return (alpha if dv == LANES else jnp.concatenate([alpha] * (dv // LANES), axis=1)) * acc


def _softmax_step(s, v_bf16, m_ref, l_ref, acc_ref, idx, values_transposed=False, exp=jnp.exp):
    p, alpha, m_new, l_new = _softmax_probs(_lane_blocks(s), m_ref[idx], l_ref[idx], exp)
    pv = _dot_nt(p, v_bf16) if values_transposed else _dot(p, v_bf16)
    acc_ref[idx] = _scale_acc(alpha, acc_ref[idx]) + pv
    m_ref[idx] = m_new
    l_ref[idx] = l_new


def _tie_bias(x, t, need, carry, tri):
    eq = x == t
    pref = _dot(jnp.where(eq, 1.0, 0.0).astype(BF16), tri) + carry
    bias = jnp.where(x > t, 0.0, jnp.where(eq, jnp.where(pref <= need, 0.0, NEG), NEG))
    return bias, pref[:, x.shape[1] - 1:]


def _fold8(x, op):
    parts = [x[i * 8:(i + 1) * 8] for i in range(x.shape[0] // 8)]
    while len(parts) > 1:
        parts = [op(parts[i], parts[i + 1]) for i in range(0, len(parts) - 1, 2)] + (parts[-1:] if len(parts) % 2 else [])
    return parts[0]


def _dsa_prompt_body(qa_ref, qi_ref, kiwq_ref, kiwk_ref, kv_ref, tril_ref, o_ref,
                     sct, sc, m_sc, l_sc, acc_sc, *, tq, ch, topk, light_passes):
    j = pl.program_id(1)
    q0 = j * tq
    nc = (q0 + tq + ch - 1) // ch
    qpos = q0 + _iota((1, tq), 1)
    kf = float(topk)
    inf = jnp.inf
    qi = qi_ref[0] * IDX_DIM ** -0.5
    qi4 = jnp.concatenate([qi[:, h * IDX_DIM:(h + 1) * IDX_DIM] for h in range(IDX_HEADS)], axis=0).astype(BF16)
    pick = jnp.where(_iota((8, LANES), 1) == _iota((8, LANES), 0) + IDX_DIM, 1.0, 0.0).astype(BF16)
    wt = sum(_dot_nt(pick, part) for part in _split3(kiwq_ref[0])) * IDX_HEADS ** -0.5

    def score_chunk(c, st):
        mx, mn = st
        c0 = pl.multiple_of(c * ch, ch)
        kc = kiwk_ref[0, pl.ds(c0, ch), :][:, :IDX_DIM].astype(BF16)
        lg = _dot_nt(kc, qi4)
        s = jnp.zeros((ch, tq), F32)
        for h in range(IDX_HEADS):
            s = s + jnp.maximum(lg[:, h * tq:(h + 1) * tq], 0.0) * wt[h:h + 1, :]
        s = jnp.where(s == 0.0, 0.0, s)
        valid = (c0 + _iota((ch, tq), 0)) <= qpos
        sct[c] = jnp.where(valid, s, -inf)
        mx = jnp.maximum(mx, _fold8(jnp.where(valid, s, -inf), jnp.maximum))
        mn = jnp.minimum(mn, _fold8(jnp.where(valid, s, inf), jnp.minimum))
        return mx, mn

    mx, mn = lax.fori_loop(0, nc, score_chunk, (jnp.full((8, tq), -inf, F32), jnp.full((8, tq), inf, F32)))
    active = qpos + 1 > topk
    row_max = jnp.where(active, jnp.max(mx, axis=0, keepdims=True), 0.0)
    row_min = jnp.where(active, jnp.min(mn, axis=0, keepdims=True), 0.0)

    def count_ge(p):
        def body(c, cnt):
            return cnt + _fold8(jnp.where(sct[c] >= p, 1.0, 0.0), jnp.add)
        return jnp.sum(lax.fori_loop(0, nc, body, jnp.zeros((8, tq), F32)), axis=0, keepdims=True)

    def stats(p):
        def body(c, st):
            cnt, mnge, mxlt = st
            x = sct[c]
            ge = x >= p
            return (cnt + _fold8(jnp.where(ge, 1.0, 0.0), jnp.add),
                    jnp.minimum(mnge, _fold8(jnp.where(ge, x, inf), jnp.minimum)),
                    jnp.maximum(mxlt, _fold8(jnp.where(ge, -inf, x), jnp.maximum)))
        init = (jnp.zeros((8, tq), F32), jnp.full((8, tq), inf, F32), jnp.full((8, tq), -inf, F32))
        cnt, mnge, mxlt = lax.fori_loop(0, nc, body, init)
        return (jnp.sum(cnt, axis=0, keepdims=True), jnp.min(mnge, axis=0, keepdims=True),
                jnp.max(mxlt, axis=0, keepdims=True))

    t = jnp.where(active, _kth_largest_bracketed(count_ge, stats, row_min, row_max, kf, light_passes), -inf)

    def count_sel(c, st):
        x = sct[c]
        return (st[0] + _fold8(jnp.where(x > t, 1.0, 0.0), jnp.add),
                st[1] + _fold8(jnp.where(x >= t, 1.0, 0.0), jnp.add))

    gt, ge = lax.fori_loop(0, nc, count_sel, (jnp.zeros((8, tq), F32), jnp.zeros((8, tq), F32)))
    need = kf - jnp.sum(gt, axis=0, keepdims=True)
    over = jnp.sum(ge, axis=0, keepdims=True) > kf
    has_ties = jnp.max(jnp.where(active, jnp.where(over, 1.0, 0.0), 0.0)) > 0.0
    eye = _eye_bf16(tq)

    def store_bias(c, sel_t):
        sel = _dot_nt(eye, sel_t.astype(BF16))
        sc[c] = jnp.where(sel > 0.5, 0.0, NEG)

    @pl.when(has_ties)
    def _():
        tril = tril_ref[...]

        def tie_chunk(c, carry):
            x = sct[c]
            eq = x == t
            pref = _dot(tril, jnp.where(eq, 1.0, 0.0).astype(BF16)) + carry
            store_bias(c, jnp.where(x > t, 1.0, jnp.where(eq, jnp.where(pref <= need, 1.0, 0.0), 0.0)))
            return pref[ch - 1:, :]

        lax.fori_loop(0, nc, tie_chunk, jnp.zeros((1, tq), F32))

    @pl.when(jnp.logical_not(has_ties))
    def _():
        def plain_chunk(c, carry):
            store_bias(c, jnp.where(sct[c] >= t, 1.0, 0.0))
            return carry

        lax.fori_loop(0, nc, plain_chunk, 0)

    cl = nc - 1
    sc[cl] = jnp.where((cl * ch + _iota((tq, ch), 1)) <= q0 + _iota((tq, ch), 0), sc[cl], NEG)

    qa = qa_ref[0].astype(F32) * (HEAD_DIM ** -0.5 * LOG2E)
    hpg = A_HEADS // A_KV_HEADS
    qh = [qa[:, h * HEAD_DIM:(h + 1) * HEAD_DIM].astype(BF16) for h in range(A_HEADS)]
    m_sc[...] = jnp.full(m_sc.shape, NEG, F32)
    l_sc[...] = jnp.zeros(l_sc.shape, F32)
    acc_sc[...] = jnp.zeros(acc_sc.shape, F32)

    def attend(c, carry):
        c0 = pl.multiple_of(c * ch, ch)
        bias = _lane_blocks(sc[c])
        kvc = kv_ref[0, pl.ds(c0, ch), :]
        kcs = [kvc[:, g * HEAD_DIM:(g + 1) * HEAD_DIM].astype(BF16) for g in range(A_KV_HEADS)]
        vcs = [kvc[:, (A_KV_HEADS + g) * HEAD_DIM:(A_KV_HEADS + g + 1) * HEAD_DIM].astype(BF16) for g in range(A_KV_HEADS)]
        ss, sm = {}, {}
        for i in range(A_HEADS + 1):
            if i < A_HEADS:
                ss[i] = _dot_nt(qh[i], kcs[i // hpg])
            if i >= 1:
                hh = i - 1
                g, rs = hh // hpg, pl.ds((hh % hpg) * tq, tq)
                blocks = [sb + bb for sb, bb in zip(_lane_blocks(ss.pop(hh)), bias)]
                sm[hh] = _softmax_probs(blocks, m_sc[g, rs, :], l_sc[g, rs, :], exp=jnp.exp2)
                m_sc[g, rs, :] = sm[hh][2]
                l_sc[g, rs, :] = sm[hh][3]
                if hh % hpg == hpg - 1:
                    heads = [sm.pop(g * hpg + h) for h in range(hpg)]
                    pv = _dot(jnp.concatenate([x[0] for x in heads], axis=0), vcs[g])
                    acc_sc[g] = _scale_acc(jnp.concatenate([x[1] for x in heads], axis=0), acc_sc[g]) + pv
        return carry

    lax.fori_loop(0, nc, attend, 0)
    outs = []
    for g in range(A_KV_HEADS):
        og = acc_sc[g] / l_sc[g][:, :HEAD_DIM]
        outs += [og[h * tq:(h + 1) * tq] for h in range(hpg)]
    o_ref[0] = jnp.concatenate(outs, axis=1).astype(o_ref.dtype)


def _upper_tri_bf16(n):
    return jnp.asarray(np.triu(np.ones((n, n), np.float32)), BF16)


def _dsa_prompt(qa, qi, kiw, kv, tq=128, ch=512):
    b, s, _ = qa.shape
    ch = min(ch, s)
    topk = min(TOPK_MAX, s // 4)
    hpg = A_HEADS // A_KV_HEADS
    return pl.pallas_call(
        functools.partial(_dsa_prompt_body, tq=tq, ch=ch, topk=topk, light_passes=18),
        grid=(b, s // tq),
        in_specs=[pl.BlockSpec((1, tq, qa.shape[2]), lambda i, j: (i, j, 0)),
                  pl.BlockSpec((1, tq, qi.shape[2]), lambda i, j: (i, j, 0)),
                  pl.BlockSpec((1, tq, LANES), lambda i, j: (i, j, 0)),
                  pl.BlockSpec((1, s, LANES), lambda i, j: (i, 0, 0)),
                  pl.BlockSpec((1, s, kv.shape[2]), lambda i, j: (i, 0, 0)),
                  pl.BlockSpec((ch, ch), lambda i, j: (0, 0))],
        out_specs=pl.BlockSpec((1, tq, A_HEADS * HEAD_DIM), lambda i, j: (i, j, 0)),
        out_shape=jax.ShapeDtypeStruct((b, s, A_HEADS * HEAD_DIM), BF16),
        scratch_shapes=[pltpu.VMEM((s // ch, ch, tq), F32),
                        pltpu.VMEM((s // ch, tq, ch), F32),
                        pltpu.VMEM((A_KV_HEADS, hpg * tq, LANES), F32),
                        pltpu.VMEM((A_KV_HEADS, hpg * tq, LANES), F32),
                        pltpu.VMEM((A_KV_HEADS, hpg * tq, HEAD_DIM), F32)],
        compiler_params=_cp("arbitrary", "arbitrary"), name="dsa_prompt",
    )(qa, qi, kiw, kiw, kv, jnp.asarray(np.tril(np.ones((ch, ch), np.float32)), BF16))


def _head_avg_matrix(width):
    seg = np.arange(width) // HEAD_DIM
    return jnp.asarray((seg[:, None] == seg[None, :]) / HEAD_DIM, BF16)


def _head_mean(x, avg):
    hi = x.astype(BF16)
    lo = (x - hi.astype(F32)).astype(BF16)
    return _dot(hi, avg) + _dot(lo, avg)


def _head_layernorm(o, gain, avg):
    d = o - _head_mean(o, avg)
    return d * lax.rsqrt(_head_mean(d * d, avg) + EPS) * gain


def _retention_body(q_ref, k_ref, v_ref, g_ref, s0_ref, dm_ref, qd_ref, kd_ref, cd_ref, gain_ref, avg_ref,
                    o_ref, sf_ref, s_sc):
    c = pl.program_id(1)

    @pl.when(c == 0)
    def _():
        s_sc[...] = s0_ref[0]

    q = q_ref[0].astype(BF16)
    k = k_ref[0]
    v = v_ref[0].astype(BF16)
    gt = g_ref[0]
    heads = range(B_HEADS)
    sls = [slice(h * HEAD_DIM, (h + 1) * HEAD_DIM) for h in heads]
    eye = _eye_bf16(HEAD_DIM)
    st = [s_sc[h] for h in heads]
    att = [_dot_nt(q[:, sl], k[:, sl].astype(BF16)) for sl in sls]
    qs = [_dot(q[:, sls[h]], st[h].astype(BF16)) for h in heads]
    kt = [_dot_nt(eye, (k[:, sls[h]] * kd_ref[h]).astype(BF16)) for h in heads]
    outs = [_dot((att[h] * dm_ref[h]).astype(BF16), v[:, sls[h]]) + qs[h] * qd_ref[h] for h in heads]
    for h in heads:
        s_sc[h] = st[h] * cd_ref[h] + _dot(kt[h].astype(BF16), v[:, sls[h]])
    o = _head_layernorm(jnp.concatenate(outs, axis=1), gain_ref[...], avg_ref[...])
    o_ref[0] = (gt * jax.nn.sigmoid(gt) * o).astype(o_ref.dtype)

    @pl.when(c == pl.num_programs(1) - 1)
    def _():
        sf_ref[0] = s_sc[...]


def _retention(q, k, v, g, s0, gain, chunk, lo, hi, out_dtype):
    b, s, w = q.shape
    dm, qd, kd, cd = _retention_tables(chunk, lo, hi)
    row = lambda i, c: (i, c, 0)
    fix3 = lambda i, c: (0, 0, 0)
    st_spec = pl.BlockSpec((1, B_HEADS, HEAD_DIM, HEAD_DIM), lambda i, c: (i, 0, 0, 0))
    return pl.pallas_call(
        _retention_body, grid=(b, s // chunk),
        in_specs=[pl.BlockSpec((1, chunk, w), row)] * 4 + [st_spec] +
                 [pl.BlockSpec(t.shape, fix3) for t in (dm, qd, kd, cd)] +
                 [pl.BlockSpec((1, w), lambda i, c: (0, 0)), pl.BlockSpec((w, w), lambda i, c: (0, 0))],
        out_specs=[pl.BlockSpec((1, chunk, w), row), st_spec],
        out_shape=[jax.ShapeDtypeStruct((b, s, w), out_dtype),
                   jax.ShapeDtypeStruct((b, B_HEADS, HEAD_DIM, HEAD_DIM), F32)],
        scratch_shapes=[pltpu.VMEM((B_HEADS, HEAD_DIM, HEAD_DIM), F32)],
        compiler_params=_cp("arbitrary", "arbitrary"), name="retention",
    )(q, k, v, g, s0, dm, qd, kd, cd, gain.reshape(1, w), _head_avg_matrix(w))


def _mlstm_body(q_ref, k_ref, v_ref, og_ref, gt_ref, cn0_ref, m0_ref, gain_ref, avg_ref,
                o_ref, cnf_ref, mf_ref, cn_sc, m_sc, *, lo, hi):
    c = pl.program_id(1)
    ck = q_ref.shape[1]
    nh = C_HEADS
    hd = HEAD_DIM

    @pl.when(c == 0)
    def _():
        cn_sc[...] = cn0_ref[0]
        m_sc[...] = m0_ref[0]

    q = q_ref[0].astype(BF16)
    k = k_ref[0]
    v = v_ref[0].astype(BF16)
    gates = gt_ref[0]
    row = _iota((ck, LANES), 0)
    tok = (row >= lo) & (row < hi)
    log_sig = jnp.minimum(gates, 0.0) - jnp.log1p(jnp.exp(-jnp.abs(gates)))
    lf = jnp.where(tok, log_sig, 0.0)
    ii = jnp.where(tok, gates, NEG)
    tril = jnp.where(_iota((ck, ck), 0) >= _iota((ck, ck), 1), 1.0, 0.0).astype(BF16)
    fcum = sum(_dot(tril, part) for part in _split3(lf))
    f_all = pltpu.roll(fcum, LANES - nh, 1)
    a_all = ii - f_all
    cm = a_all
    step = 1
    while step < ck:
        cm = jnp.maximum(cm, jnp.where(row >= step, pltpu.roll(cm, step, 0), -jnp.inf))
        step *= 2
    m_prev = m_sc[...]
    m_t = f_all + jnp.maximum(m_prev, cm)
    inter_all = jnp.exp(f_all + m_prev - m_t)
    floor_all = jnp.exp(-m_t)
    g_all = f_all - m_t
    f_last = f_all[ck - 1:, :]
    m_end = m_t[ck - 1:, :]
    w_all = jnp.exp(f_last + a_all - m_end)
    dec_all = jnp.exp(f_last + m_prev - m_end)
    m_sc[...] = m_end
    pick = jnp.where(_iota((nh, LANES), 0) == _iota((nh, LANES), 1), 1.0, 0.0).astype(BF16)
    a_rows = sum(_dot_nt(pick, part) for part in _split3(a_all))
    causal = _iota((ck, ck), 0) >= _iota((ck, ck), 1)
    eye = _eye_bf16(hd)
    ones = jnp.ones((ck, hd), BF16)
    n_lanes = _iota((hd, 2 * hd), 1) >= hd
    heads = range(nh)
    sls = [slice(h * hd, (h + 1) * hd) for h in heads]
    col = lambda x, h: x[:, h:h + 1]
    cn = [cn_sc[h] for h in heads]
    v1 = [jnp.concatenate([v[:, sl], ones], axis=1) for sl in sls]
    qk = [_dot_nt(q[:, sl], k[:, sl].astype(BF16)) for sl in sls]
    qc = [_dot(q[:, sls[h]], cn[h].astype(BF16)) for h in heads]
    kw = [k[:, sls[h]] * col(w_all, h) for h in heads]
    kw_hi = [x.astype(BF16) for x in kw]
    kw_lo = [(x - y.astype(F32)).astype(BF16) for x, y in zip(kw, kw_hi)]
    kt_hi = [_dot_nt(eye, x) for x in kw_hi]
    kt_lo = [_dot_nt(eye, x) for x in kw_lo]
    dmat = [jnp.exp(jnp.where(causal, col(g_all, h) + a_rows[h:h + 1, :], -jnp.inf)) for h in heads]
    nd = [_dot((qk[h] * dmat[h]).astype(BF16), v1[h]) + col(inter_all, h) * qc[h] for h in heads]
    outs = [(nd[h] / jnp.maximum(jnp.abs(pltpu.roll(nd[h], hd, 1)), col(floor_all, h)))[:, :hd] for h in heads]
    upd = [_dot(kt_hi[h].astype(BF16), v1[h]) for h in heads]
    fix = [_dot(kt_lo[h].astype(BF16), v1[h]) for h in heads]
    for h in heads:
        cn_sc[h] = col(dec_all, h) * cn[h] + upd[h] + jnp.where(n_lanes, fix[h], 0.0)
    hc = jax.nn.sigmoid(og_ref[0]) * jnp.concatenate(outs, axis=1)
    o_ref[0] = _head_layernorm(hc, gain_ref[...], avg_ref[...]).astype(o_ref.dtype)

    @pl.when(c == pl.num_programs(1) - 1)
    def _():
        cnf_ref[0] = cn_sc[...]
        mf_ref[0] = m_sc[...]


def _mlstm(q, k, v, og, gates, c0, n0, m0, gain, chunk, lo, hi, out_dtype):
    b, s, w = q.shape
    nh = C_HEADS
    row = lambda i, c: (i, c, 0)
    hd = HEAD_DIM
    cn_spec = pl.BlockSpec((1, nh, hd, 2 * hd), lambda i, c: (i, 0, 0, 0))
    m_spec = pl.BlockSpec((1, 1, LANES), lambda i, c: (i, 0, 0))
    cn0 = jnp.concatenate([c0, jnp.broadcast_to(n0[..., None], (b, nh, hd, hd))], axis=-1)
    m0p = jnp.pad(m0, ((0, 0), (0, LANES - nh))).reshape(b, 1, LANES)
    o, cnf, mf = pl.pallas_call(
        functools.partial(_mlstm_body, lo=lo, hi=hi), grid=(b, s // chunk),
        in_specs=[pl.BlockSpec((1, chunk, w), row)] * 4 + [pl.BlockSpec((1, chunk, LANES), row), cn_spec, m_spec,
                                                          pl.BlockSpec((1, w), lambda i, c: (0, 0)),
                                                          pl.BlockSpec((w, w), lambda i, c: (0, 0))],
        out_specs=[pl.BlockSpec((1, chunk, w), row), cn_spec, m_spec],
        out_shape=[jax.ShapeDtypeStruct((b, s, w), out_dtype),
                   jax.ShapeDtypeStruct((b, nh, hd, 2 * hd), F32),
                   jax.ShapeDtypeStruct((b, 1, LANES), F32)],
        scratch_shapes=[pltpu.VMEM((nh, hd, 2 * hd), F32), pltpu.VMEM((1, LANES), F32)],
        compiler_params=_cp("arbitrary", "arbitrary"), name="mlstm",
    )(q, k, v, og, gates, cn0, m0p, gain.reshape(1, w), _head_avg_matrix(w))
    return o, cnf[..., :hd], cnf[..., hd], mf[:, 0, :nh]


def _diff_lambda(lam_ref, lam_init):
    lp = lam_ref[...]
    s01 = jnp.sum(lp[0:1] * lp[1:2], axis=1, keepdims=True)
    s23 = jnp.sum(lp[2:3] * lp[3:4], axis=1, keepdims=True)
    return jnp.exp(s01) - jnp.exp(s23) + lam_init


def _diff_finish(o0, l0, o1, l1, lam, subln, lam_init):
    od = o0 / l0 - lam * (o1 / l1)
    return od * lax.rsqrt(jnp.mean(od * od, axis=-1, keepdims=True) + EPS) * subln * (1.0 - lam_init)


def _diff_prompt_body(q_ref, k_ref, v_ref, lam_ref, sub_ref, o_ref, m_sc, l_sc, acc_sc, *, lam_init):
    qi = pl.program_id(1)
    t = q_ref.shape[1]
    m_sc[...] = jnp.full(m_sc.shape, NEG, F32)
    l_sc[...] = jnp.zeros(l_sc.shape, F32)
    acc_sc[...] = jnp.zeros(acc_sc.shape, F32)
    q = (q_ref[0].astype(F32) * (HEAD_DIM ** -0.5 * LOG2E)).astype(BF16)

    def step(j, diag):
        r0 = pl.multiple_of(j * t, t)
        k = k_ref[0, pl.ds(r0, t), :].astype(BF16)
        v = v_ref[0, pl.ds(r0, t), :].astype(BF16)
        keep = (_iota((t, t), 0) >= _iota((t, t), 1)) if diag else None
        n = 2 * D_HEADS
        ss, sm = {}, {}
        for i in range(n + 2):
            if i < n:
                sl = slice(i * HEAD_DIM, (i + 1) * HEAD_DIM)
                s = _dot_nt(q[:, sl], k[:, sl])
                ss[i] = jnp.where(keep, s, NEG) if diag else s
            if 1 <= i <= n:
                sm[i - 1] = _softmax_probs(_lane_blocks(ss.pop(i - 1)), m_sc[i - 1], l_sc[i - 1], jnp.exp2)
            if i >= 2:
                idx = i - 2
                p, alpha, m_new, l_new = sm.pop(idx)
                h = idx % D_HEADS
                acc_sc[idx] = _scale_acc(alpha, acc_sc[idx]) + _dot(p, v[:, h * D_VDIM:(h + 1) * D_VDIM])
                m_sc[idx] = m_new
                l_sc[idx] = l_new

    def below_diagonal(j, carry):
        step(j, False)
        return carry

    lax.fori_loop(0, qi, below_diagonal, 0)
    step(qi, True)
    lam = _diff_lambda(lam_ref, lam_init)
    outs = [_diff_finish(acc_sc[h], l_sc[h], acc_sc[D_HEADS + h], l_sc[D_HEADS + h], lam, sub_ref[...], lam_init)
            for h in range(D_HEADS)]
    o_ref[0] = jnp.concatenate(outs, axis=1).astype(o_ref.dtype)


def _diff_prompt(q, k, v, lam_par, subln, lam_init, t=512):
    b, s, w = q.shape
    t = min(t, s)
    return pl.pallas_call(
        functools.partial(_diff_prompt_body, lam_init=lam_init), grid=(b, s // t),
        in_specs=[pl.BlockSpec((1, t, w), lambda bi, i: (bi, i, 0)),
                  pl.BlockSpec((1, s, w), lambda bi, i: (bi, 0, 0)),
                  pl.BlockSpec((1, s, w), lambda bi, i: (bi, 0, 0)),
                  pl.BlockSpec(lam_par.shape, lambda bi, i: (0, 0)),
                  pl.BlockSpec((1, D_VDIM), lambda bi, i: (0, 0))],
        out_specs=pl.BlockSpec((1, t, w), lambda bi, i: (bi, i, 0)),
        out_shape=jax.ShapeDtypeStruct((b, s, w), BF16),
        scratch_shapes=[pltpu.VMEM((2 * D_HEADS, t, LANES), F32), pltpu.VMEM((2 * D_HEADS, t, LANES), F32),
                        pltpu.VMEM((2 * D_HEADS, t, D_VDIM), F32)],
        compiler_params=_cp("arbitrary", "arbitrary"), name="diff_prompt",
    )(q, k, v, lam_par, subln.reshape(1, D_VDIM))


def _page_specs(rows, layer, group, n_pages):
    def spec(g):
        def imap(b, j, pt):
            return (layer, pt[b, jnp.minimum(j * group + g, n_pages - 1)], 0, 0)
        return pl.BlockSpec((None, None, rows, PAGE_SIZE), imap)
    return [spec(g) for g in range(group)]


def _cat_lanes(refs):
    return jnp.concatenate([r[...] for r in refs], axis=1)


def _new_key_valid(nq, nk):
    rq = _iota((nq, nk), 0) % SLOT
    rk = _iota((nq, nk), 1)
    return (rk >= SLOT_LO) & (rk < SLOT_LO + (SLOT - 2 * SLOT_LO)) & (rk <= rq)


def _idx_scores(qi4, w, keys_t_bf16):
    lg = _dot(qi4, keys_t_bf16)
    s = jnp.zeros((SLOT, keys_t_bf16.shape[1]), F32)
    for h in range(IDX_HEADS):
        s = s + jnp.maximum(lg[h * SLOT:(h + 1) * SLOT], 0.0) * w[:, h:h + 1]
    return jnp.where(s == 0.0, 0.0, s)


def _dsa_sample_scores_body(pt_ref, qi_ref, kiw_ref, knew_ref, *rest, group):
    pages, o_ref = rest[:group], rest[group]
    j = pl.program_id(1)
    last = pl.num_programs(1) - 1
    qi = qi_ref[0] * IDX_DIM ** -0.5
    qi4 = jnp.concatenate([qi[:, h * IDX_DIM:(h + 1) * IDX_DIM] for h in range(IDX_HEADS)], axis=0).astype(BF16)
    w = kiw_ref[0][:, IDX_DIM:IDX_DIM + IDX_HEADS] * IDX_HEADS ** -0.5

    @pl.when(j < last)
    def _():
        o_ref[0, 0] = _idx_scores(qi4, w, _cat_lanes(pages).astype(BF16))

    @pl.when(j == last)
    def _():
        s = _idx_scores(qi4, w, knew_ref[0].astype(BF16))
        s = jnp.where(_new_key_valid(SLOT, PAGE_SIZE), s, -jnp.inf)
        pad = jnp.full((SLOT, (group - 1) * PAGE_SIZE), -jnp.inf, F32)
        o_ref[0, 0] = jnp.concatenate([s, pad], axis=1) if group > 1 else s


def _dsa_sample_scores(page_table, qi, kiw, knew_t, cache_idx_t, layer, group):
    db, n_pages = page_table.shape
    nch = n_pages // group + 1
    gw = group * PAGE_SIZE
    grid_spec = pltpu.PrefetchScalarGridSpec(
        num_scalar_prefetch=1, grid=(db, nch),
        in_specs=[pl.BlockSpec((1, SLOT, qi.shape[2]), lambda b, j, pt: (b, 0, 0)),
                  pl.BlockSpec((1, SLOT, LANES), lambda b, j, pt: (b, 0, 0)),
                  pl.BlockSpec((1, IDX_DIM, PAGE_SIZE), lambda b, j, pt: (b, 0, 0))]
                 + _page_specs(IDX_DIM, layer, group, n_pages),
        out_specs=pl.BlockSpec((1, 1, SLOT, gw), lambda b, j, pt: (b, j, 0, 0)))
    return pl.pallas_call(
        functools.partial(_dsa_sample_scores_body, group=group), grid_spec=grid_spec,
        out_shape=jax.ShapeDtypeStruct((db, nch, SLOT, gw), F32),
        compiler_params=_cp("arbitrary", "arbitrary"), name="dsa_sample_scores",
    )(page_table, qi, kiw, knew_t, *([cache_idx_t] * group))


def _dsa_sample_attend_body(pt_ref, qa_ref, sc_ref, kvnew_ref, tri_ref, *rest, group, topk):
    pages, o_ref = rest[:group], rest[group]
    t_sc, need_sc, carry_sc, m_sc, l_sc, acc_sc, ties_sm = rest[group + 1:]
    j = pl.program_id(1)
    last = pl.num_programs(1) - 1
    hpg = A_HEADS // A_KV_HEADS

    @pl.when(j == 0)
    def _():
        x = sc_ref[0]

        def stats(p):
            ge = x >= p[None]
            cnt = jnp.sum(jnp.sum(jnp.where(ge, 1.0, 0.0), axis=0), axis=1, keepdims=True)
            mnge = jnp.min(jnp.min(jnp.where(ge, x, jnp.inf), axis=0), axis=1, keepdims=True)
            mxlt = jnp.max(jnp.max(jnp.where(ge, -jnp.inf, x), axis=0), axis=1, keepdims=True)
            return cnt, mnge, mxlt

        finite = x > -jnp.inf
        cmax0 = jnp.max(jnp.max(x, axis=0), axis=1, keepdims=True)
        cmin0 = jnp.min(jnp.min(jnp.where(finite, x, jnp.inf), axis=0), axis=1, keepdims=True)
        def count_ge(p):
            return jnp.sum(jnp.sum(jnp.where(x >= p[None], 1.0, 0.0), axis=0), axis=1, keepdims=True)

        t = _kth_largest_bracketed(count_ge, stats, cmin0, cmax0, float(topk), 14)
        n_gt =jnp.sum(jnp.sum(jnp.where(x > t[None], 1.0, 0.0), axis=0), axis=1, keepdims=True)
        n_ge = jnp.sum(jnp.sum(jnp.where(x >= t[None], 1.0, 0.0), axis=0), axis=1, keepdims=True)
        ties_sm[0] = (jnp.max(n_ge) > float(topk)).astype(jnp.int32)
        t_sc[...] = jnp.broadcast_to(t, t_sc.shape)
        need_sc[...] = jnp.broadcast_to(float(topk) - n_gt, need_sc.shape)
        carry_sc[...] = jnp.zeros(carry_sc.shape, F32)
        m_sc[...] = jnp.full(m_sc.shape, NEG, F32)
        l_sc[...] = jnp.zeros(l_sc.shape, F32)
        acc_sc[...] = jnp.zeros(acc_sc.shape, F32)

    qa = qa_ref[0] * HEAD_DIM ** -0.5
    qg = [jnp.concatenate([qa[:, (g * hpg + h) * HEAD_DIM:(g * hpg + h + 1) * HEAD_DIM] for h in range(hpg)],
                          axis=0).astype(BF16) for g in range(A_KV_HEADS)]
    t = t_sc[...][:, :1]
    need = need_sc[...][:, :1]
    tri = tri_ref[...]

    def tie_bias(x):
        carry = carry_sc[...][:, :1]
        biases = []
        for xb in _lane_blocks(x):
            bias, carry = _tie_bias(xb, t, need, carry, tri)
            biases.append(bias)
        carry_sc[...] = jnp.broadcast_to(carry, carry_sc.shape)
        return jnp.concatenate(biases, axis=1) if len(biases) > 1 else biases[0]

    def attend(x, kv_t):
        bias = lax.cond(ties_sm[0] > 0, tie_bias, lambda xs: jnp.where(xs >= t, 0.0, NEG), x)
        bias4 = jnp.concatenate([bias] * hpg, axis=0)
        for g in range(A_KV_HEADS):
            kc = kv_t[g * HEAD_DIM:(g + 1) * HEAD_DIM].astype(BF16)
            vc = kv_t[(A_KV_HEADS + g) * HEAD_DIM:(A_KV_HEADS + g + 1) * HEAD_DIM].astype(BF16)
            _softmax_step(_dot(qg[g], kc) + bias4, vc, m_sc, l_sc, acc_sc, g, values_transposed=True)

    @pl.when(j < last)
    def _():
        attend(sc_ref[0, j], _cat_lanes(pages))

    @pl.when(j == last)
    def _():
        attend(sc_ref[0, j][:, :PAGE_SIZE], kvnew_ref[0])
        outs = []
        for g in range(A_KV_HEADS):
            og = acc_sc[g] / l_sc[g][:, :HEAD_DIM]
            outs += [og[h * SLOT:(h + 1) * SLOT] for h in range(hpg)]
        o_ref[0] = jnp.concatenate(outs, axis=1)


def _dsa_sample_attend(page_table, qa, scores, kvnew, cache_kv, layer, group, topk):
    db, n_pages = page_table.shape
    nch = n_pages // group + 1
    gw = group * PAGE_SIZE
    hpg = A_HEADS // A_KV_HEADS
    kvw = 2 * A_KV_HEADS * HEAD_DIM
    grid_spec = pltpu.PrefetchScalarGridSpec(
        num_scalar_prefetch=1, grid=(db, nch),
        in_specs=[pl.BlockSpec((1, SLOT, qa.shape[2]), lambda b, j, pt: (b, 0, 0)),
                  pl.BlockSpec((1, nch, SLOT, gw), lambda b, j, pt: (b, 0, 0, 0)),
                  pl.BlockSpec((1, kvw, PAGE_SIZE), lambda b, j, pt: (b, 0, 0)),
                  pl.BlockSpec((PAGE_SIZE, PAGE_SIZE), lambda b, j, pt: (0, 0))]
                 + _page_specs(kvw, layer, group, n_pages),
        out_specs=pl.BlockSpec((1, SLOT, A_HEADS * HEAD_DIM), lambda b, j, pt: (b, 0, 0)),
        scratch_shapes=[pltpu.VMEM((SLOT, LANES), F32), pltpu.VMEM((SLOT, LANES), F32), pltpu.VMEM((SLOT, LANES), F32),
                        pltpu.VMEM((A_KV_HEADS, hpg * SLOT, LANES), F32),
                        pltpu.VMEM((A_KV_HEADS, hpg * SLOT, LANES), F32),
                        pltpu.VMEM((A_KV_HEADS, hpg * SLOT, HEAD_DIM), F32),
                        pltpu.SMEM((1,), jnp.int32)])
    return pl.pallas_call(
        functools.partial(_dsa_sample_attend_body, group=group, topk=topk), grid_spec=grid_spec,
        out_shape=jax.ShapeDtypeStruct((db, SLOT, A_HEADS * HEAD_DIM), F32),
        compiler_params=_cp("arbitrary", "arbitrary"), name="dsa_sample_attend",
    )(page_table, qa, scores, kvnew, _upper_tri_bf16(PAGE_SIZE), *([cache_kv] * group))


def _diff_sample_body(pt_ref, q_ref, knew_ref, vnew_ref, lam_ref, sub_ref, *rest, group, lam_init):
    kpages, vpages, o_ref = rest[:group], rest[group:2 * group], rest[2 * group]
    m_sc, l_sc, acc_sc = rest[2 * group + 1:]
    j = pl.program_id(1)
    last = pl.num_programs(1) - 1
    nmap = 2 * D_HEADS
    w = nmap * HEAD_DIM

    @pl.when(j == 0)
    def _():
        m_sc[...] = jnp.full(m_sc.shape, NEG, F32)
        l_sc[...] = jnp.zeros(l_sc.shape, F32)
        acc_sc[...] = jnp.zeros(acc_sc.shape, F32)

    q = q_ref[0] * HEAD_DIM ** -0.5
    lane_map = _iota((SLOT, w), 1) // HEAD_DIM
    qb = jnp.concatenate([jnp.where(lane_map == m * D_HEADS + h, q, 0.0)
                          for h in range(D_HEADS) for m in range(2)], axis=0).astype(BF16)
    rows_h = 2 * SLOT

    def step(k_t, v_refs, mask):
        s = _dot(qb, k_t.astype(BF16))
        if mask is not None:
            s = jnp.where(mask, s, NEG)
        p, alpha, m_new, l_new = _softmax_probs(_lane_blocks(s), m_sc[...], l_sc[...])
        m_sc[...] = m_new
        l_sc[...] = l_new
        for h in range(D_HEADS):
            vh = jnp.concatenate([r[pl.ds(h, PAGE_SIZE, stride=D_HEADS), :] for r in v_refs], axis=0).astype(BF16)
            rs = slice(h * rows_h, (h + 1) * rows_h)
            acc_sc[h] = alpha[rs] * acc_sc[h] + _dot(p[rs], vh)

    @pl.when(j < last)
    def _():
        step(_cat_lanes(kpages), vpages, None)

    @pl.when(j == last)
    def _():
        step(knew_ref[0], [vnew_ref.at[0]], _new_key_valid(nmap * SLOT, PAGE_SIZE))
        lam = _diff_lambda(lam_ref, lam_init)
        l = l_sc[...]
        outs = []
        for h in range(D_HEADS):
            acc = acc_sc[h]
            lh = l[h * rows_h:(h + 1) * rows_h]
            outs.append(_diff_finish(acc[:SLOT], lh[:SLOT], acc[SLOT:], lh[SLOT:], lam, sub_ref[...], lam_init))
        o_ref[0] = jnp.concatenate(outs, axis=1)


def _diff_sample(page_table, q, knew_t, vnew, lam_par, subln, cache_k_t, cache_v, layer, group, lam_init):
    db, n_pages = page_table.shape
    nch = n_pages // group + 1
    nmap = 2 * D_HEADS
    w = q.shape[2]
    vw = D_HEADS * D_VDIM
    vrows = PAGE_SIZE * D_HEADS
    grid_spec = pltpu.PrefetchScalarGridSpec(
        num_scalar_prefetch=1, grid=(db, nch),
        in_specs=[pl.BlockSpec((1, SLOT, w), lambda b, j, pt: (b, 0, 0)),
                  pl.BlockSpec((1, w, PAGE_SIZE), lambda b, j, pt: (b, 0, 0)),
                  pl.BlockSpec((1, vrows, D_VDIM), lambda b, j, pt: (b, 0, 0)),
                  pl.BlockSpec(lam_par.shape, lambda b, j, pt: (0, 0)),
                  pl.BlockSpec((1, D_VDIM), lambda b, j, pt: (0, 0))]
                 + _page_specs(w, layer, group, n_pages) + _page_specs(vrows, layer, group, n_pages),
        out_specs=pl.BlockSpec((1, SLOT, vw), lambda b, j, pt: (b, 0, 0)),
        scratch_shapes=[pltpu.VMEM((nmap * SLOT, LANES), F32), pltpu.VMEM((nmap * SLOT, LANES), F32),
                        pltpu.VMEM((D_HEADS, 2 * SLOT, D_VDIM), F32)])
    return pl.pallas_call(
        functools.partial(_diff_sample_body, group=group, lam_init=lam_init), grid_spec=grid_spec,
        out_shape=jax.ShapeDtypeStruct((db, SLOT, vw), F32),
        compiler_params=_cp("arbitrary", "arbitrary"), name="diff_sample",
    )(page_table, q, knew_t, vnew, lam_par, subln.reshape(1, D_VDIM), *([cache_k_t] * group), *([cache_v] * group))


def _split_cols(w, sizes):
    out, o = [], 0
    for s in sizes:
        out.append(w[:, o:o + s])
        o += s
    return out


def _pad_cols(w, n):
    return jnp.pad(w, ((0, 0), (0, n - w.shape[1])))


def _pad_rows(a, n):
    return jnp.pad(a, ((0, 0), (0, n - a.shape[1]), (0, 0)))


def _keys_t(a):
    return jnp.pad(jnp.swapaxes(a, 1, 2), ((0, 0), (0, 0), (0, PAGE_SIZE - a.shape[1])))


def _pos_minor(cache):
    nd = cache.ndim
    t = jnp.transpose(cache, (0, 1) + tuple(range(3, nd)) + (2,))
    return t.reshape(cache.shape[0], cache.shape[1], -1, cache.shape[2])


def _ab_weights(w_in):
    hd = HEAD_DIM
    aq, ak, av, iq, ik, iw, bq, bk, bv, bg = _split_cols(
        w_in, (A_HEADS * hd, A_KV_HEADS * hd, A_KV_HEADS * hd, IDX_HEADS * IDX_DIM, IDX_DIM, IDX_HEADS,
               B_HEADS * hd, B_HEADS * hd, B_HEADS * hd, B_HEADS * hd))
    segs = [aq, jnp.concatenate([ak, av], axis=1), iq, _pad_cols(jnp.concatenate([ik, iw], axis=1), LANES), bq, bk, bv, bg]
    return segs


def _cd_weights(w_in):
    hd = HEAD_DIM
    cq, ck, cv, ci, cf, co, dq, dk, dv = _split_cols(
        w_in, (C_HEADS * hd, C_HEADS * hd, C_HEADS * hd, C_HEADS, C_HEADS, C_HEADS * hd,
               2 * D_HEADS * hd, 2 * D_HEADS * hd, D_HEADS * D_VDIM))
    segs = [cq, ck, cv, co, _pad_cols(jnp.concatenate([ci, cf], axis=1), LANES), dq, dk, dv]
    return segs


def kernel(x_prompt, x_sample, c_prompt, c_sample, page_table, cache_a_kv, cache_a_idx, cache_d_k, cache_d_v, state_ret, state_mlstm_c, state_mlstm_n, state_mlstm_m, state_ffn_conv, norm_pre_mix, norm_post_mix, norm_pre_ffn, norm_post_ffn, w_ada, b_ada, w_ab_in, w_ab_out, ret_gain, w_cd_in, w_cd_out, c_gate_bias, c_gain, d_lambda, d_subln, ffn_w_up, ffn_conv_w, ffn_conv_b, ffn_w_down):
    b, s, d = x_prompt.shape
    db, t, _ = x_sample.shape
    depth = w_ada.shape[0]
    n_pages = page_table.shape[1]
    past = n_pages * PAGE_SIZE
    d_ff = ffn_w_down.shape[1]
    assert t == SLOT - 2 * SLOT_LO and past >= TOPK_MAX and s % 1024 == 0
    hi = SLOT_LO + t
    ms = db * SLOT
    tm_in, tm_ffn, tf = 512, 256, 256
    group = math.gcd(n_pages, 16)
    group_a = math.gcd(n_pages, 32)
    topk_s = min(TOPK_MAX, (past + t) // 4)

    n_c = b + db
    c_all = jnp.pad(jnp.concatenate([c_prompt, c_sample], axis=0), ((0, -n_c % 8), (0, 0)))
    mods = _modulation(c_all, w_ada, b_ada)
    mods_p = mods[:, :b].reshape(depth, b, 1, 6 * d)
    mods_s = jnp.repeat(mods[:, b:n_c], SLOT, axis=1).reshape(depth, 1, ms, 6 * d)
    w_up, w_down = ffn_w_up.astype(BF16), ffn_w_down.astype(BF16)
    w_ab_in, w_ab_out = w_ab_in.astype(BF16), w_ab_out.astype(BF16)
    w_cd_in, w_cd_out = w_cd_in.astype(BF16), w_cd_out.astype(BF16)

    xp = x_prompt.reshape(b * s, d)
    xs = jnp.pad(x_sample, ((0, 0), (SLOT_LO, SLOT - hi), (0, 0))).reshape(ms, d)

    pos_s = past + (np.arange(ms) % SLOT) - SLOT_LO
    tp_p, tf_p = _rope_tables(np.arange(s), HEAD_DIM // 4, ROPE_THETA), _rope_tables(np.arange(s), HEAD_DIM, RET_THETA)
    tp_s, tf_s = _rope_tables(pos_s, HEAD_DIM // 4, ROPE_THETA), _rope_tables(pos_s, HEAD_DIM, RET_THETA)

    cache_a_kv = _pos_minor(cache_a_kv)
    cache_a_idx = _pos_minor(cache_a_idx)
    cache_d_k = _pos_minor(cache_d_k)
    cache_d_v = cache_d_v.reshape(*cache_d_v.shape[:2], -1, D_VDIM)

    hd = HEAD_DIM
    w512 = 8 * hd
    ab_defs = lambda dt: [(w512, dt), (4 * hd, F32), (4 * hd, dt), (LANES, F32), (w512, dt), (w512, F32), (w512, dt), (w512, F32)]
    cd_defs = lambda dt: [(w512, dt), (w512, F32), (w512, dt), (w512, F32), (LANES, F32), (w512, dt), (w512, F32), (w512, F32)]
    zeros = lambda *shape: jnp.zeros(shape, F32)

    outs = {k: [] for k in ("pa_kv", "pa_idx", "pd_k", "pd_v", "p_ret", "p_mc", "p_mn", "p_mm", "p_cv",
                            "sa_kv", "sa_idx", "sd_k", "sd_v", "s_ret", "s_mc", "s_mn", "s_mm", "s_cv")}
    for l in range(depth):
        p = l // 2
        if l % 2 == 0:
            wts = _ab_weights(w_ab_in[p])
            wo = w_ab_out[p]
            qa, kv, qi, kiw, bq, bk, bv, bg = _in_proj(_ab_in_body, "ab_in", xp, norm_pre_mix[l], mods_p, l,
                                                       [tp_p, tf_p], [], wts, ab_defs(BF16), tm_in, s // tm_in)
            r3 = lambda a: a.reshape(b, s, a.shape[-1])
            a_out = _dsa_prompt(r3(qa), r3(qi), r3(kiw), r3(kv))
            b_out, st = _retention(r3(bq), r3(bk), r3(bv), r3(bg), zeros(b, B_HEADS, hd, hd), ret_gain[p],
                                   CHUNK, 0, CHUNK, BF16)
            outs["pa_kv"].append(kv.reshape(b, s, 2, A_KV_HEADS, hd))
            outs["pa_idx"].append(r3(kiw)[:, :, :IDX_DIM])
            outs["p_ret"].append(st)
            mix_p = (a_out.reshape(b * s, -1), b_out.reshape(b * s, -1))
            qa, kv, qi, kiw, bq, bk, bv, bg = _in_proj(_ab_in_body, "ab_in", xs, norm_pre_mix[l], mods_s, l,
                                                       [tp_s, tf_s], [], wts, ab_defs(F32), ms, 1)
            r3 = lambda a: a.reshape(db, SLOT, a.shape[-1])
            scores = _dsa_sample_scores(page_table, r3(qi), r3(kiw), _keys_t(r3(kiw)[:, :, :IDX_DIM]), cache_a_idx, p, group_a)
            a_out = _dsa_sample_attend(page_table, r3(qa), scores, _keys_t(r3(kv)), cache_a_kv, p, group_a, topk_s)
            b_out, st = _retention(r3(bq), r3(bk), r3(bv), r3(bg), state_ret[p], ret_gain[p], SLOT, SLOT_LO, hi, F32)
            outs["sa_kv"].append(r3(kv)[:, SLOT_LO:hi].reshape(db, t, 2, A_KV_HEADS, hd))
            outs["sa_idx"].append(r3(kiw)[:, SLOT_LO:hi, :IDX_DIM])
            outs["s_ret"].append(st)
            mix_s = (a_out.reshape(ms, -1), b_out.reshape(ms, -1))
        else:
            lam_init = 0.8 - 0.6 * math.exp(-0.3 * l)
            wts = _cd_weights(w_cd_in[p])
            wo = w_cd_out[p]
            gbias = jnp.pad(c_gate_bias[p], (0, LANES - 2 * C_HEADS)).reshape(1, LANES)
            cq, ck, cv, co, gt, dq, dk, dv = _in_proj(_cd_in_body, "cd_in", xp, norm_pre_mix[l], mods_p, l,
                                                      [tp_p], [gbias], wts, cd_defs(BF16), tm_in, s // tm_in)
            r3 = lambda a: a.reshape(b, s, a.shape[-1])
            c_out, mc, mn, mm = _mlstm(r3(cq), r3(ck), r3(cv), r3(co), r3(gt), zeros(b, C_HEADS, hd, hd),
                                       zeros(b, C_HEADS, hd), zeros(b, C_HEADS), c_gain[p], CHUNK, 0, CHUNK, BF16)
            d_out = _diff_prompt(r3(dq), r3(dk), r3(dv), d_lambda[p], d_subln[p], lam_init)
            outs["pd_k"].append(dk.reshape(b, s, 2, D_HEADS, hd))
            outs["pd_v"].append(dv.reshape(b, s, D_HEADS, D_VDIM))
            outs["p_mc"].append(mc); outs["p_mn"].append(mn); outs["p_mm"].append(mm)
            mix_p = (c_out.reshape(b * s, -1), d_out.reshape(b * s, -1))
            cq, ck, cv, co, gt, dq, dk, dv = _in_proj(_cd_in_body, "cd_in", xs, norm_pre_mix[l], mods_s, l,
                                                      [tp_s], [gbias], wts, cd_defs(F32), ms, 1)
            r3 = lambda a: a.reshape(db, SLOT, a.shape[-1])
            c_out, mc, mn, mm = _mlstm(r3(cq), r3(ck), r3(cv), r3(co), r3(gt), state_mlstm_c[p], state_mlstm_n[p],
                                       state_mlstm_m[p], c_gain[p], SLOT, SLOT_LO, hi, F32)
            vnew = _pad_rows(dv.reshape(db, SLOT * D_HEADS, D_VDIM), PAGE_SIZE * D_HEADS)
            d_out = _diff_sample(page_table, r3(dq), _keys_t(r3(dk)), vnew,
                                 d_lambda[p], d_subln[p], cache_d_k, cache_d_v, p, group, lam_init)
            outs["sd_k"].append(r3(dk)[:, SLOT_LO:hi].reshape(db, t, 2, D_HEADS, hd))
            outs["sd_v"].append(r3(dv)[:, SLOT_LO:hi].reshape(db, t, D_HEADS, D_VDIM))
            outs["s_mc"].append(mc); outs["s_mn"].append(mn); outs["s_mm"].append(mm)
            mix_s = (c_out.reshape(ms, -1), d_out.reshape(ms, -1))

        xp, tails = _ffn(xp, *mix_p, wo, norm_post_mix[l], norm_pre_ffn[l], mods_p, w_up, w_down, l,
                         ffn_conv_w[l], ffn_conv_b[l], None, norm_post_ffn[l], tm_ffn, s // tm_ffn, tf)
        outs["p_cv"].append(tails.reshape(b, s // tm_ffn, 2, d_ff)[:, -1])
        inj = jnp.pad(state_ffn_conv[l], ((0, 0), (0, SLOT - SLOT_LO), (0, 0))).reshape(ms, d_ff)
        xs, a_all = _ffn(xs, *mix_s, wo, norm_post_mix[l], norm_pre_ffn[l], mods_s, w_up, w_down, l,
                         ffn_conv_w[l], ffn_conv_b[l], inj, norm_post_ffn[l], ms, 1, tf)
        outs["s_cv"].append(a_all.reshape(db, SLOT, d_ff)[:, hi - (CONV_W - 1):hi])

    st = lambda k: jnp.stack(outs[k])
    y_sample = xs.reshape(db, SLOT, d)[:, SLOT_LO:hi]
    return (xp.reshape(b, s, d), y_sample,
            st("pa_kv"), st("pa_idx"), st("pd_k"), st("pd_v"), st("p_ret"), st("p_mc"), st("p_mn"), st("p_mm"), st("p_cv"),
            st("sa_kv"), st("sa_idx"), st("sd_k"), st("sd_v"), st("s_ret"), st("s_mc"), st("s_mn"), st("s_mm"), st("s_cv"))
```
